```python
import math
import jax, jax.numpy as jnp
from jax import lax
import numpy as np

D_MODEL = 1024
BATCH = 16
SEQ = 2048
DEPTH = 1
DEC_BATCH = 2
DEC_SEQ = 16384
PAST_LEN = 128

HEAD_DIM = 64
A_HEADS = 8
A_KV_HEADS = 2
A_GROUP = A_HEADS // A_KV_HEADS
A_WIDTH = A_HEADS * HEAD_DIM
B_HEADS = 4
B_VDIM = 2 * HEAD_DIM
B_WIDTH = B_HEADS * B_VDIM
GRID_W = 64
ROPE_THETA = 10000.0
Q_BLOCK = 128
N_EXPERTS = 16
EXPERT_FF = 2048
CAPACITY_FACTOR = 2
PLE_DIM = 256
LN_EPS = 1e-5
RMS_EPS = 1e-6
ALPHA = (2 * DEPTH) ** 0.25
BETA = (8 * DEPTH) ** -0.25

QA_COLS = A_WIDTH
KA_COLS = A_KV_HEADS * HEAD_DIM
VA_COLS = A_KV_HEADS * HEAD_DIM
QB_COLS = 2 * B_HEADS * HEAD_DIM
KB_COLS = 2 * B_HEADS * HEAD_DIM
VB_COLS = B_WIDTH
GATE_COLS = D_MODEL
IN_COLS = QA_COLS + KA_COLS + VA_COLS + QB_COLS + KB_COLS + VB_COLS + 2 * GATE_COLS

kernel_name = "hybrid_gqa_diffattn_ecmoe_encoder"


def _split_points():
    sizes = [QA_COLS, KA_COLS, VA_COLS, QB_COLS, KB_COLS, VB_COLS, GATE_COLS]
    pts, acc = [], 0
    for sz in sizes:
        acc += sz
        pts.append(acc)
    return pts


def layer_norm(x, g, b):
    xf = x.astype(jnp.float32)
    mu = jnp.mean(xf, -1, keepdims=True)
    var = jnp.mean(jnp.square(xf - mu), -1, keepdims=True)
    return ((xf - mu) * lax.rsqrt(var + LN_EPS) * g.astype(jnp.float32) + b.astype(jnp.float32)).astype(x.dtype)


def rms_norm(x, g):
    xf = x.astype(jnp.float32)
    ms = jnp.mean(jnp.square(xf), -1, keepdims=True)
    return (xf * lax.rsqrt(ms + RMS_EPS) * g.astype(jnp.float32)).astype(x.dtype)


def axial_rope_tables(seq):
    n_rows = seq // GRID_W
    rows = jnp.repeat(jnp.arange(n_rows, dtype=jnp.float32), GRID_W)
    cols = jnp.tile(jnp.arange(GRID_W, dtype=jnp.float32), n_rows)
    sec = HEAD_DIM // 2
    inv_freq = ROPE_THETA ** (-jnp.arange(0, sec, 2, dtype=jnp.float32) / sec)
    ang_r = rows[:, None] * inv_freq[None, :]
    ang_c = cols[:, None] * inv_freq[None, :]
    return jnp.cos(ang_r), jnp.sin(ang_r), jnp.cos(ang_c), jnp.sin(ang_c)


def _rotate_section(x, cos, sin):
    x1, x2 = jnp.split(x, 2, -1)
    c = cos[:, None, :]
    s = sin[:, None, :]
    return jnp.concatenate([x1 * c - x2 * s, x1 * s + x2 * c], -1)


def apply_axial_rope(x, tables):
    cos_r, sin_r, cos_c, sin_c = tables
    xf = x.astype(jnp.float32)
    xr, xc = jnp.split(xf, 2, -1)
    out = jnp.concatenate([_rotate_section(xr, cos_r, sin_r), _rotate_section(xc, cos_c, sin_c)], -1)
    return out.astype(x.dtype)


def to_blocks(t):
    b, s = t.shape[0], t.shape[1]
    t = t.reshape((b, s // Q_BLOCK, Q_BLOCK) + t.shape[2:])
    return jnp.moveaxis(t, 1, 0)


def from_blocks(t):
    t = jnp.moveaxis(t, 0, 1)
    return t.reshape((t.shape[0], -1) + t.shape[3:])


def gqa_attention(q, k, v):
    b, s = q.shape[0], q.shape[1]
    qb = to_blocks(q.reshape(b, s, A_KV_HEADS, A_GROUP, HEAD_DIM))
    scale = HEAD_DIM ** -0.5

    def one_block(qblk):
        sc = jnp.einsum('bqhgd,bkhd->bhgqk', qblk, k).astype(jnp.float32) * scale
        pr = jax.nn.softmax(sc, axis=-1).astype(v.dtype)
        return jnp.einsum('bhgqk,bkhd->bqhgd', pr, v)

    out = lax.map(one_block, qb)
    return from_blocks(out).reshape(b, s, A_WIDTH)


def diff_attention(q, k, v, lam):
    b, s = q.shape[0], q.shape[1]
    qb = to_blocks(q)
    starts = jnp.arange(s // Q_BLOCK, dtype=jnp.int32) * Q_BLOCK
    kpos = jnp.arange(s, dtype=jnp.int32)
    slopes = 2.0 ** (-8.0 * jnp.arange(1, B_HEADS + 1, dtype=jnp.float32) / B_HEADS)
    scale = HEAD_DIM ** -0.5

    def one_block(args):
        qblk, start = args
        sc = jnp.einsum('bqmhd,bkmhd->bmhqk', qblk, k).astype(jnp.float32) * scale
        qpos = start + jnp.arange(Q_BLOCK, dtype=jnp.int32)
        dist = jnp.abs(qpos[:, None] - kpos[None, :]).astype(jnp.float32)
        sc = sc - slopes[:, None, None] * dist[None]
        pr = jax.nn.softmax(sc, axis=-1)
        a = pr[:, 0] - lam * pr[:, 1]
        return jnp.einsum('bhqk,bkhe->bqhe', a.astype(v.dtype), v)

    out = lax.map(one_block, (qb, starts))
    return from_blocks(out)


def expert_choice_ffn(h, w_router, w_gate, w_up, w_down):
    b, s, d = h.shape
    n = b * s
    cap = max(1, CAPACITY_FACTOR * n // N_EXPERTS)
    hf = h.reshape(n, d)
    aff = jax.nn.softmax((hf @ w_router).astype(jnp.float32), axis=-1)
    g, idx = lax.top_k(aff.T, cap)
    xe = hf[idx]
    a = jnp.einsum('ecd,edf->ecf', xe, w_gate)
    u = jnp.einsum('ecd,edf->ecf', xe, w_up)
    ye = jnp.einsum('ecf,efd->ecd', jax.nn.silu(a) * u, w_down)
    ye = ye * g[..., None].astype(ye.dtype)
    out = jnp.zeros_like(hf).at[idx.reshape(-1)].add(ye.reshape(-1, d))
    return out.reshape(b, s, d)


def run_trunk(x, p, ln_emb_g, ln_emb_b, w_in, q_norm_g, k_norm_g, lambda_q1, lambda_k1,
              lambda_q2, lambda_k2, subln_g, w_branch_a, w_branch_b, w_out, ln1_g, ln1_b,
              w_router, w_gate, w_up, w_down, w_ple_gate, w_ple_proj, ln2_g, ln2_b):
    b, s, _ = x.shape
    tables = axial_rope_tables(s)
    pts = _split_points()
    x = layer_norm(x, ln_emb_g, ln_emb_b)
    for i in range(DEPTH):
        h = x
        proj = h @ w_in[i]
        qa, ka, va, qbm, kbm, vbm, ga, gb = jnp.split(proj, pts, axis=-1)
        qa = apply_axial_rope(rms_norm(qa.reshape(b, s, A_HEADS, HEAD_DIM), q_norm_g[i]), tables)
        ka = apply_axial_rope(rms_norm(ka.reshape(b, s, A_KV_HEADS, HEAD_DIM), k_norm_g[i]), tables)
        ya = gqa_attention(qa, ka, va.reshape(b, s, A_KV_HEADS, HEAD_DIM))
        lam_init = 0.8 - 0.6 * math.exp(-0.3 * i)
        lam = (jnp.exp(jnp.sum(lambda_q1[i] * lambda_k1[i]).astype(jnp.float32))
               - jnp.exp(jnp.sum(lambda_q2[i] * lambda_k2[i]).astype(jnp.float32)) + lam_init)
        yb = diff_attention(qbm.reshape(b, s, 2, B_HEADS, HEAD_DIM),
                            kbm.reshape(b, s, 2, B_HEADS, HEAD_DIM),
                            vbm.reshape(b, s, B_HEADS, B_VDIM), lam)
        yb = (rms_norm(yb, subln_g[i]) * (1.0 - lam_init)).reshape(b, s, B_WIDTH)
        merged = jax.nn.sigmoid(ga) * (ya @ w_branch_a[i]) + jax.nn.sigmoid(gb) * (yb @ w_branch_b[i])
        x = layer_norm(ALPHA * h + merged @ w_out[i], ln1_g[i], ln1_b[i])
        h = x
        moe = expert_choice_ffn(h, w_router[i], w_gate[i], w_up[i], w_down[i])
        ple = jax.nn.sigmoid(h @ w_ple_gate[i]) * (p[i] @ w_ple_proj[i])
        x = layer_norm(ALPHA * h + moe + ple, ln2_g[i], ln2_b[i])
    return x


def setup_inputs(seed: int = 0) -> dict:
    key = jax.random.key(seed)
    ks = jax.random.split(key, 32)
    f32 = jnp.float32

    def nrm(k, shape, scale):
        return jax.random.normal(k, shape, f32) * scale

    def gain(k, shape):
        return 1.0 + 0.02 * jax.random.normal(k, shape, f32)

    return {
        'x_prompt': nrm(ks[0], (BATCH, SEQ, D_MODEL), 1.0),
        'x_sample': nrm(ks[1], (DEC_BATCH, DEC_SEQ, D_MODEL), 1.0),
        'p_prompt': nrm(ks[2], (DEPTH, BATCH, SEQ, PLE_DIM), 1.0),
        'p_sample': nrm(ks[3], (DEPTH, DEC_BATCH, DEC_SEQ, PLE_DIM), 1.0),
        'ln_emb_g': gain(ks[4], (D_MODEL,)),
        'ln_emb_b': nrm(ks[5], (D_MODEL,), 0.02),
        'w_in': nrm(ks[6], (DEPTH, D_MODEL, IN_COLS), D_MODEL ** -0.5),
        'q_norm_g': gain(ks[7], (DEPTH, HEAD_DIM)),
        'k_norm_g': gain(ks[8], (DEPTH, HEAD_DIM)),
        'lambda_q1': nrm(ks[9], (DEPTH, HEAD_DIM), 0.1),
        'lambda_k1': nrm(ks[10], (DEPTH, HEAD_DIM), 0.1),
        'lambda_q2': nrm(ks[11], (DEPTH, HEAD_DIM), 0.1),
        'lambda_k2': nrm(ks[12], (DEPTH, HEAD_DIM), 0.1),
        'subln_g': gain(ks[13], (DEPTH, B_VDIM)),
        'w_branch_a': nrm(ks[14], (DEPTH, A_WIDTH, D_MODEL), A_WIDTH ** -0.5),
        'w_branch_b': nrm(ks[15], (DEPTH, B_WIDTH, D_MODEL), B_WIDTH ** -0.5),
        'w_out': nrm(ks[16], (DEPTH, D_MODEL, D_MODEL), BETA * D_MODEL ** -0.5),
        'ln1_g': gain(ks[17], (DEPTH, D_MODEL)),
        'ln1_b': nrm(ks[18], (DEPTH, D_MODEL), 0.02),
        'w_router': nrm(ks[19], (DEPTH, D_MODEL, N_EXPERTS), D_MODEL ** -0.5),
        'w_gate': nrm(ks[20], (DEPTH, N_EXPERTS, D_MODEL, EXPERT_FF), D_MODEL ** -0.5),
        'w_up': nrm(ks[21], (DEPTH, N_EXPERTS, D_MODEL, EXPERT_FF), D_MODEL ** -0.5),
        'w_down': nrm(ks[22], (DEPTH, N_EXPERTS, EXPERT_FF, D_MODEL), BETA * EXPERT_FF ** -0.5),
        'w_ple_gate': nrm(ks[23], (DEPTH, D_MODEL, D_MODEL), D_MODEL ** -0.5),
        'w_ple_proj': nrm(ks[24], (DEPTH, PLE_DIM, D_MODEL), BETA * PLE_DIM ** -0.5),
        'ln2_g': gain(ks[25], (DEPTH, D_MODEL)),
        'ln2_b': nrm(ks[26], (DEPTH, D_MODEL), 0.02),
    }


def reference(x_prompt, x_sample, p_prompt, p_sample, ln_emb_g, ln_emb_b, w_in, q_norm_g, k_norm_g,
              lambda_q1, lambda_k1, lambda_q2, lambda_k2, subln_g, w_branch_a, w_branch_b, w_out,
              ln1_g, ln1_b, w_router, w_gate, w_up, w_down, w_ple_gate, w_ple_proj, ln2_g, ln2_b):
    y_prompt = run_trunk(x_prompt, p_prompt, ln_emb_g, ln_emb_b, w_in, q_norm_g, k_norm_g,
                         lambda_q1, lambda_k1, lambda_q2, lambda_k2, subln_g, w_branch_a, w_branch_b,
                         w_out, ln1_g, ln1_b, w_router, w_gate, w_up, w_down, w_ple_gate, w_ple_proj,
                         ln2_g, ln2_b)
    y_sample = run_trunk(x_sample, p_sample, ln_emb_g, ln_emb_b, w_in, q_norm_g, k_norm_g,
                         lambda_q1, lambda_k1, lambda_q2, lambda_k2, subln_g, w_branch_a, w_branch_b,
                         w_out, ln1_g, ln1_b, w_router, w_gate, w_up, w_down, w_ple_gate, w_ple_proj,
                         ln2_g, ln2_b)
    return (y_prompt, y_sample)
```

```python
import functools
import math

import jax
import jax.numpy as jnp
from jax import lax
from jax.experimental import pallas as pl
from jax.experimental.pallas import tpu as pltpu

F32 = jnp.float32
BF16 = jnp.bfloat16
I32 = jnp.int32

D_MODEL = 1024
DEPTH = 1
HEAD_DIM = 64
A_HEADS = 8
A_KV_HEADS = 2
A_GROUP = A_HEADS // A_KV_HEADS
A_WIDTH = A_HEADS * HEAD_DIM
B_HEADS = 4
B_VDIM = 2 * HEAD_DIM
B_WIDTH = B_HEADS * B_VDIM
GRID_W = 64
ROPE_THETA = 10000.0
N_EXPERTS = 16
EXPERT_FF = 2048
CAPACITY_FACTOR = 2
PLE_DIM = 256
LN_EPS = 1e-5
RMS_EPS = 1e-6
ALPHA = (2 * DEPTH) ** 0.25
SCALE = HEAD_DIM ** -0.5
LANES = 128
ROPE_HALF = HEAD_DIM // 4

C_QA = 0
C_KA = C_QA + A_WIDTH
C_VA = C_KA + A_KV_HEADS * HEAD_DIM
C_QB = C_VA + A_KV_HEADS * HEAD_DIM
C_KB = C_QB + 2 * B_HEADS * HEAD_DIM
C_VB = C_KB + 2 * B_HEADS * HEAD_DIM
C_G = C_VB + B_WIDTH
IN_COLS = C_G + 2 * D_MODEL

VMEM_LIMIT = 56 * 1024 * 1024
NT_DIMS = (((1,), (1,)), ((), ()))


def _params(*sem):
    return pltpu.CompilerParams(dimension_semantics=sem, vmem_limit_bytes=VMEM_LIMIT)


def _full(shape):
    n = len(shape)
    return pl.BlockSpec(shape, lambda *_: (0,) * n)


def _layer_norm(x, g, b):
    mu = jnp.mean(x, axis=-1, keepdims=True)
    xc = x - mu
    var = jnp.mean(xc * xc, axis=-1, keepdims=True)
    return xc * lax.rsqrt(var + LN_EPS) * g + b


def _sigmoid(x):
    return 1.0 / (1.0 + jnp.exp(-x))


def _split_bf16(x):
    hi = x.astype(BF16)
    lo = (x - hi.astype(F32)).astype(BF16)
    return hi, lo


def _head_mean_sq(x, seg):
    hi, lo = _split_bf16(x * x)
    s = jnp.dot(hi, seg, preferred_element_type=F32) + jnp.dot(lo, seg, preferred_element_type=F32)
    return s * (1.0 / HEAD_DIM)


def _rope(x, cos, sin, first_half):
    fwd = pltpu.roll(x, LANES - ROPE_HALF, 1)
    bwd = pltpu.roll(x, ROPE_HALF, 1)
    return x * cos + jnp.where(first_half, fwd, bwd) * sin


def _in_proj_kernel(x_ref, lng_ref, lnb_ref, w_ref, qg_ref, kg_ref, cos_ref, sin_ref, seg_ref,
                    h_ref, qa_ref, ka_ref, va_ref, qb_ref, kb_ref, vb_ref, gate_ref):
    xn = _layer_norm(x_ref[...], lng_ref[...], lnb_ref[...])
    h_ref[...] = xn
    xb = xn.astype(BF16)

    def proj(lo, hi):
        return jnp.dot(xb, w_ref[:, lo:hi], preferred_element_type=F32)

    cos = cos_ref[...]
    sin = sin_ref[...]
    lane = lax.broadcasted_iota(I32, cos.shape, 1)
    first_half = (lane & ROPE_HALF) == 0

    qa = proj(C_QA, C_KA)
    qn = qa * lax.rsqrt(_head_mean_sq(qa, seg_ref[...]) + RMS_EPS)
    for c in range(A_WIDTH // LANES):
        sl = slice(c * LANES, (c + 1) * LANES)
        qa_ref[:, sl] = (_rope(qn[:, sl] * qg_ref[...], cos, sin, first_half) * SCALE).astype(BF16)

    ka = proj(C_KA, C_VA)
    kn = ka * lax.rsqrt(_head_mean_sq(ka, seg_ref[0:LANES, 0:LANES]) + RMS_EPS)
    ka_ref[...] = _rope(kn * kg_ref[...], cos, sin, first_half).astype(BF16)

    va_ref[...] = proj(C_VA, C_QB).astype(BF16)
    qb_ref[...] = (proj(C_QB, C_KB) * SCALE).astype(BF16)
    kb_ref[...] = proj(C_KB, C_VB).astype(BF16)
    vb_ref[...] = proj(C_VB, C_G).astype(BF16)
    gate_ref[...] = proj(C_G, IN_COLS)


def _in_proj(x2, lng, lnb, w_in, qg, kg, cos_t, sin_t, seg, seq):
    n = x2.shape[0]
    tm = 256
    tiles_per_seq = seq // tm

    def rows(w):
        return pl.BlockSpec((tm, w), lambda i: (i, 0))

    tab = pl.BlockSpec((tm, LANES), lambda i: (i % tiles_per_seq, 0))
    outs = [(D_MODEL, F32), (A_WIDTH, BF16), (LANES, BF16), (LANES, BF16), (2 * B_HEADS * HEAD_DIM, BF16),
            (2 * B_HEADS * HEAD_DIM, BF16), (B_WIDTH, BF16), (2 * D_MODEL, F32)]
    return pl.pallas_call(
        _in_proj_kernel,
        grid=(n // tm,),
        in_specs=[rows(D_MODEL), _full((1, D_MODEL)), _full((1, D_MODEL)), _full((D_MODEL, IN_COLS)),
                  _full((1, LANES)), _full((1, LANES)), tab, tab, _full((A_WIDTH, A_WIDTH))],
        out_specs=[rows(w) for w, _ in outs],
        out_shape=[jax.ShapeDtypeStruct((n, w), dt) for w, dt in outs],
        compiler_params=_params("parallel"),
        name="in_proj",
    )(x2, lng, lnb, w_in, qg, kg, cos_t, sin_t, seg)


def _online_softmax_step(s, v, m_ref, l_ref, acc_ref, i):
    m_prev = m_ref[i]
    m_new = jnp.maximum(m_prev, jnp.max(s, axis=1, keepdims=True))
    alpha = jnp.exp(m_prev - m_new)
    p = jnp.exp(s - m_new)
    l_ref[i] = alpha * l_ref[i] + jnp.sum(p, axis=1, keepdims=True)
    acc_ref[i] = alpha * acc_ref[i] + jnp.dot(p.astype(BF16), v, preferred_element_type=F32)
    m_ref[i] = m_new


def _init_softmax_state(m_ref, l_ref, acc_ref):
    m_ref[...] = jnp.full(m_ref.shape, -jnp.inf, F32)
    l_ref[...] = jnp.zeros(l_ref.shape, F32)
    acc_ref[...] = jnp.zeros(acc_ref.shape, F32)


def _gqa_kernel(q_ref, k_ref, v_ref, o_ref, m_ref, l_ref, acc_ref, *, nk):
    ki = pl.program_id(2)

    @pl.when(ki == 0)
    def _():
        _init_softmax_state(m_ref, l_ref, acc_ref)

    for h in range(A_HEADS):
        kv = h // A_GROUP
        q = q_ref[:, h * HEAD_DIM:(h + 1) * HEAD_DIM]
        k = k_ref[:, kv * HEAD_DIM:(kv + 1) * HEAD_DIM]
        v = v_ref[:, kv * HEAD_DIM:(kv + 1) * HEAD_DIM]
        s = lax.dot_general(q, k, NT_DIMS, preferred_element_type=F32)
        _online_softmax_step(s, v, m_ref, l_ref, acc_ref, h)

    @pl.when(ki == nk - 1)
    def _():
        for h in range(A_HEADS):
            o_ref[:, h * HEAD_DIM:(h + 1) * HEAD_DIM] = (acc_ref[h] / l_ref[h]).astype(BF16)


def _attn_tiles(seq):
    return min(512, seq), min(1024, seq)


def _gqa_attention(qa, ka, va):
    b, seq, _ = qa.shape
    tq, tk = _attn_tiles(seq)
    nk = seq // tk
    return pl.pallas_call(
        functools.partial(_gqa_kernel, nk=nk),
        grid=(b, seq // tq, nk),
        in_specs=[pl.BlockSpec((None, tq, A_WIDTH), lambda bi, qi, ki: (bi, qi, 0)),
                  pl.BlockSpec((None, tk, LANES), lambda bi, qi, ki: (bi, ki, 0)),
                  pl.BlockSpec((None, tk, LANES), lambda bi, qi, ki: (bi, ki, 0))],
        out_specs=pl.BlockSpec((None, tq, A_WIDTH), lambda bi, qi, ki: (bi, qi, 0)),
        out_shape=jax.ShapeDtypeStruct((b, seq, A_WIDTH), BF16),
        scratch_shapes=[pltpu.VMEM((A_HEADS, tq, 1), F32), pltpu.VMEM((A_HEADS, tq, 1), F32),
                        pltpu.VMEM((A_HEADS, tq, HEAD_DIM), F32)],
        compiler_params=_params("parallel", "parallel", "arbitrary"),
        name="gqa_attn",
    )(qa, ka, va)


def _diff_kernel(q_ref, k_ref, v_ref, lq1_ref, lk1_ref, lq2_ref, lk2_ref, g_ref, o_ref,
                 m_ref, l_ref, acc_ref, *, tq, tk, nk, lam_init):
    qi = pl.program_id(1)
    ki = pl.program_id(2)

    @pl.when(ki == 0)
    def _():
        _init_softmax_state(m_ref, l_ref, acc_ref)

    row = lax.broadcasted_iota(I32, (tq, tk), 0)
    col = lax.broadcasted_iota(I32, (tq, tk), 1)
    dist = jnp.abs(row - col + (qi * tq - ki * tk)).astype(F32)
    for h in range(B_HEADS):
        slope = 2.0 ** (-8.0 * (h + 1) / B_HEADS)
        bias = dist * slope
        v = v_ref[:, h * B_VDIM:(h + 1) * B_VDIM]
        for m in range(2):
            c0 = (m * B_HEADS + h) * HEAD_DIM
            s = lax.dot_general(q_ref[:, c0:c0 + HEAD_DIM], k_ref[:, c0:c0 + HEAD_DIM], NT_DIMS,
                                preferred_element_type=F32) - bias
            _online_softmax_step(s, v, m_ref, l_ref, acc_ref, m * B_HEADS + h)

    @pl.when(ki == nk - 1)
    def _():
        lam = (jnp.exp(jnp.sum(lq1_ref[...] * lk1_ref[...], axis=1, keepdims=True))
               - jnp.exp(jnp.sum(lq2_ref[...] * lk2_ref[...], axis=1, keepdims=True)) + lam_init)
        for h in range(B_HEADS):
            y = acc_ref[h] / l_ref[h] - lam * (acc_ref[B_HEADS + h] / l_ref[B_HEADS + h])
            ms = jnp.mean(y * y, axis=1, keepdims=True)
            y = y * lax.rsqrt(ms + RMS_EPS) * g_ref[...] * (1.0 - lam_init)
            o_ref[:, h * B_VDIM:(h + 1) * B_VDIM] = y.astype(BF16)


def _diff_attention(qb, kb, vb, lq1, lk1, lq2, lk2, subln_g, lam_init):
    b, seq, _ = qb.shape
    tq, tk = _attn_tiles(seq)
    nk = seq // tk
    w = 2 * B_HEADS * HEAD_DIM
    vec = _full((1, HEAD_DIM))
    return pl.pallas_call(
        functools.partial(_diff_kernel, tq=tq, tk=tk, nk=nk, lam_init=lam_init),
        grid=(b, seq // tq, nk),
        in_specs=[pl.BlockSpec((None, tq, w), lambda bi, qi, ki: (bi, qi, 0)),
                  pl.BlockSpec((None, tk, w), lambda bi, qi, ki: (bi, ki, 0)),
                  pl.BlockSpec((None, tk, B_WIDTH), lambda bi, qi, ki: (bi, ki, 0)),
                  vec, vec, vec, vec, _full((1, B_VDIM))],
        out_specs=pl.BlockSpec((None, tq, B_WIDTH), lambda bi, qi, ki: (bi, qi, 0)),
        out_shape=jax.ShapeDtypeStruct((b, seq, B_WIDTH), BF16),
        scratch_shapes=[pltpu.VMEM((2 * B_HEADS, tq, 1), F32), pltpu.VMEM((2 * B_HEADS, tq, 1), F32),
                        pltpu.VMEM((2 * B_HEADS, tq, B_VDIM), F32)],
        compiler_params=_params("parallel", "parallel", "arbitrary"),
        name="diff_attn",
    )(qb, kb, vb, lq1, lk1, lq2, lk2, subln_g)


def _merge_kernel(ya_ref, yb_ref, gate_ref, h_ref, wa_ref, wb_ref, wo_ref, g_ref, b_ref,
                  wrh_ref, wrl_ref, x1_ref, aff_ref):
    ma = jnp.dot(ya_ref[...], wa_ref[...], preferred_element_type=F32)
    mb = jnp.dot(yb_ref[...], wb_ref[...], preferred_element_type=F32)
    merged = _sigmoid(gate_ref[:, 0:D_MODEL]) * ma + _sigmoid(gate_ref[:, D_MODEL:2 * D_MODEL]) * mb
    out = jnp.dot(merged.astype(BF16), wo_ref[...], preferred_element_type=F32)
    x1 = _layer_norm(ALPHA * h_ref[...] + out, g_ref[...], b_ref[...])
    x1_ref[...] = x1
    xh, xl = _split_bf16(x1)
    wh = wrh_ref[...]
    logits = (lax.dot_general(wh, xh, NT_DIMS, preferred_element_type=F32)
              + lax.dot_general(wh, xl, NT_DIMS, preferred_element_type=F32)
              + lax.dot_general(wrl_ref[...], xh, NT_DIMS, preferred_element_type=F32))
    e = jnp.exp(logits - jnp.max(logits, axis=0, keepdims=True))
    aff_ref[...] = e / jnp.sum(e, axis=0, keepdims=True)


def _merge(ya, yb, gates, h, wa, wb, wo, g, b, wrh, wrl):
    n = h.shape[0]
    tm = 256

    def rows(w):
        return pl.BlockSpec((tm, w), lambda i: (i, 0))

    return pl.pallas_call(
        _merge_kernel,
        grid=(n // tm,),
        in_specs=[rows(A_WIDTH), rows(B_WIDTH), rows(2 * D_MODEL), rows(D_MODEL),
                  _full((A_WIDTH, D_MODEL)), _full((B_WIDTH, D_MODEL)), _full((D_MODEL, D_MODEL)),
                  _full((1, D_MODEL)), _full((1, D_MODEL)),
                  _full((N_EXPERTS, D_MODEL)), _full((N_EXPERTS, D_MODEL))],
        out_specs=[rows(D_MODEL), pl.BlockSpec((N_EXPERTS, tm), lambda i: (0, i))],
        out_shape=[jax.ShapeDtypeStruct((n, D_MODEL), F32), jax.ShapeDtypeStruct((N_EXPERTS, n), F32)],
        compiler_params=_params("parallel"),
        name="merge",
    )(ya, yb, gates, h, wa, wb, wo, g, b, wrh, wrl)


def _route_kernel(aff_ref, rank_ref, idx_ref, blkp_ref, sel_scr, wi_scr, p_scr, *, nb, cap):
    ne = N_EXPERTS
    shape3 = (ne, nb, LANES)
    bits = pltpu.bitcast(aff_ref[...], I32).reshape(shape3)
    capf = float(cap)

    def count(mask3):
        part = jnp.sum(jnp.where(mask3, 1.0, 0.0), axis=1, keepdims=True)
        return jnp.sum(part, axis=2, keepdims=True)

    def search(i, thr):
        cand = thr | jnp.left_shift(jnp.int32(1), 30 - i)
        return jnp.where(count(bits >= cand) >= capf, cand, thr)

    thr = lax.fori_loop(0, 31, search, jnp.zeros((ne, 1, 1), I32))
    gt = bits > thr
    eq = bits == thr
    need = capf - count(gt)

    li = lax.broadcasted_iota(I32, (LANES, LANES), 0)
    lj = lax.broadcasted_iota(I32, (LANES, LANES), 1)
    upper_incl = jnp.where(li <= lj, 1.0, 0.0).astype(BF16)
    lower_incl = jnp.where(lj <= li, 1.0, 0.0).astype(BF16)
    bi = lax.broadcasted_iota(I32, (nb, nb), 0)
    bj = lax.broadcasted_iota(I32, (nb, nb), 1)
    lower_strict = jnp.where(bj < bi, 1.0, 0.0).astype(BF16)
    upper_strict = jnp.where(bi < bj, 1.0, 0.0).astype(BF16)

    def prefix(mask2):
        wi = jnp.dot(mask2.astype(BF16), upper_incl, preferred_element_type=F32)
        tot = jnp.broadcast_to(wi[:, LANES - 1:LANES], wi.shape).astype(BF16)
        blk = [jnp.dot(lower_strict, tot[e * nb:(e + 1) * nb], preferred_element_type=F32)
               for e in range(ne)]
        return wi, jnp.concatenate(blk, axis=0)

    eqf = jnp.where(eq, 1.0, 0.0).reshape(ne * nb, LANES)
    wi, bp = prefix(eqf)
    eq_rank = (wi + bp - eqf).reshape(shape3)
    sel = gt | (eq & (eq_rank < need))
    self_ = jnp.where(sel, 1.0, 0.0).reshape(ne * nb, LANES)
    wi, bp = prefix(self_)
    rank = wi + bp - self_
    rank_ref[...] = jnp.where(self_ > 0.0, rank, -1.0).astype(I32)
    sel_scr[...] = self_
    wi_scr[...] = wi
    p_scr[...] = bp

    chunk = min(cap, 1024)
    ones8 = jnp.ones((8, LANES), BF16)
    b_col = lax.broadcasted_iota(I32, (nb, 1), 0).astype(F32)

    def per_expert(e, carry):
        r0 = pl.multiple_of(e * nb, 8)
        sel_e = sel_scr[pl.ds(r0, nb), :].astype(BF16)
        wi_e = wi_scr[pl.ds(r0, nb), :]
        p_e = p_scr[pl.ds(r0, nb), :][:, 0:1]
        p_end = p_e + wi_e[:, LANES - 1:LANES]
        wi_t = lax.dot_general(lower_incl, sel_e, NT_DIMS, preferred_element_type=F32).astype(BF16)
        tot_row = lax.dot_general(ones8, sel_e, NT_DIMS, preferred_element_type=F32)
        p_row = jnp.dot(tot_row.astype(BF16), upper_strict, preferred_element_type=F32)
        blkp_ref[e] = p_row[0:1, :].astype(I32)
        for c in range(cap // chunk):
            slot = (lax.broadcasted_iota(I32, (1, chunk), 1) + c * chunk).astype(F32)
            hot = (p_e <= slot) & (slot < p_end)
            w_slot = jnp.dot(wi_t, jnp.where(hot, 1.0, 0.0).astype(BF16), preferred_element_type=F32)
            p_slot = jnp.sum(jnp.where(hot, p_e, 0.0), axis=0, keepdims=True)
            b_slot = jnp.sum(jnp.where(hot, b_col, 0.0), axis=0, keepdims=True)
            lane = jnp.sum(jnp.where(w_slot <= slot - p_slot, 1.0, 0.0), axis=0, keepdims=True)
            idx_ref[e, :, c * chunk:(c + 1) * chunk] = (b_slot * LANES + lane).astype(I32)
        return carry

    lax.fori_loop(0, ne, per_expert, 0)


def _route(aff2, n, cap):
    nb = n // LANES
    return pl.pallas_call(
        functools.partial(_route_kernel, nb=nb, cap=cap),
        out_shape=[jax.ShapeDtypeStruct((N_EXPERTS * nb, LANES), I32),
                   jax.ShapeDtypeStruct((N_EXPERTS, 1, cap), I32),
                   jax.ShapeDtypeStruct((N_EXPERTS, 1, nb), I32)],
        scratch_shapes=[pltpu.VMEM((N_EXPERTS * nb, LANES), F32)] * 3,
        compiler_params=pltpu.CompilerParams(vmem_limit_bytes=VMEM_LIMIT),
        name="route",
    )(aff2)


def _gather_kernel(idx_ref, x_hbm, o_ref, sem, *, rows):
    base = pl.program_id(0) * rows

    def row_copy(r):
        return pltpu.make_async_copy(x_hbm.at[pl.ds(idx_ref[base + r], 1), :], o_ref.at[pl.ds(r, 1), :], sem.at[0])

    def start(r, c):
        row_copy(r).start()
        return c

    def wait(r, c):
        row_copy(r).wait()
        return c

    lax.fori_loop(0, rows, start, 0)
    lax.fori_loop(0, rows, wait, 0)


def _gather(idx_flat, x1):
    rows = 256
    total = idx_flat.shape[0]
    return pl.pallas_call(
        functools.partial(_gather_kernel, rows=rows),
        grid_spec=pltpu.PrefetchScalarGridSpec(
            num_scalar_prefetch=1,
            grid=(total // rows,),
            in_specs=[pl.BlockSpec(memory_space=pl.ANY)],
            out_specs=pl.BlockSpec((rows, D_MODEL), lambda i, idx: (i, 0)),
            scratch_shapes=[pltpu.SemaphoreType.DMA((1,))],
        ),
        out_shape=jax.ShapeDtypeStruct((total, D_MODEL), F32),
        compiler_params=_params("arbitrary"),
        name="gather",
    )(idx_flat, x1)


def _ffn_kernel(x_ref, wg_ref, wu_ref, wd_ref, y_ref):
    x = x_ref[...].astype(BF16)
    fc = 512
    acc = None
    for c in range(EXPERT_FF // fc):
        sl = slice(c * fc, (c + 1) * fc)
        a = jnp.dot(x, wg_ref[:, sl], preferred_element_type=F32)
        u = jnp.dot(x, wu_ref[:, sl], preferred_element_type=F32)
        mid = (a * _sigmoid(a) * u).astype(BF16)
        part = jnp.dot(mid, wd_ref[sl, :], preferred_element_type=F32)
        acc = part if acc is None else acc + part
    y_ref[...] = acc.astype(BF16)


def _ffn(xe, wg, wu, wd, cap):
    tm = min(512, cap)
    tiles = cap // tm
    return pl.pallas_call(
        _ffn_kernel,
        grid=(N_EXPERTS, tiles),
        in_specs=[pl.BlockSpec((tm, D_MODEL), lambda e, i: (e * tiles + i, 0)),
                  pl.BlockSpec((None, D_MODEL, EXPERT_FF), lambda e, i: (e, 0, 0)),
                  pl.BlockSpec((None, D_MODEL, EXPERT_FF), lambda e, i: (e, 0, 0)),
                  pl.BlockSpec((None, EXPERT_FF, D_MODEL), lambda e, i: (e, 0, 0))],
        out_specs=pl.BlockSpec((tm, D_MODEL), lambda e, i: (e * tiles + i, 0)),
        out_shape=jax.ShapeDtypeStruct((N_EXPERTS * cap, D_MODEL), BF16),
        compiler_params=_params("parallel", "parallel"),
        name="ffn",
    )(xe, wg, wu, wd)


TOK_TILE = 128
ROW_ALIGN = 16
WIN = TOK_TILE + ROW_ALIGN


def _final_kernel(start_ref, cnt_ref, x1_ref, p_ref, rank_ref, aff_ref, wpg_ref, wpp_ref, g_ref, b_ref,
                  ye_hbm, o_ref, buf, moe_ref, sem, *, cap):
    t = pl.program_id(0)

    def window(e):
        first = start_ref[t * N_EXPERTS + e]
        ws = jnp.minimum((first // ROW_ALIGN) * ROW_ALIGN, cap - WIN)
        return pl.multiple_of(ws, ROW_ALIGN)

    def copy(e):
        return pltpu.make_async_copy(ye_hbm.at[pl.ds(e * cap + window(e), WIN), :], buf.at[e], sem.at[e])

    for e in range(N_EXPERTS):
        @pl.when(cnt_ref[t * N_EXPERTS + e] > 0)
        def _():
            copy(e).start()

    x1 = x1_ref[...]
    gate = _sigmoid(jnp.dot(x1.astype(BF16), wpg_ref[...], preferred_element_type=F32))
    ple = gate * jnp.dot(p_ref[...].astype(BF16), wpp_ref[...], preferred_element_type=F32)
    moe_ref[...] = ALPHA * x1 + ple

    slot = lax.broadcasted_iota(I32, (1, WIN), 1)
    for e in range(N_EXPERTS):
        @pl.when(cnt_ref[t * N_EXPERTS + e] > 0)
        def _():
            copy(e).wait()
            hot = rank_ref[:, e:e + 1] == slot + window(e)
            y = jnp.dot(jnp.where(hot, 1.0, 0.0).astype(BF16), buf[e], preferred_element_type=F32)
            moe_ref[...] += aff_ref[:, e:e + 1] * y

    o_ref[...] = _layer_norm(moe_ref[...], g_ref[...], b_ref[...])


def _final(starts, cnts, x1, p2, rank_t, aff_t, wpg, wpp, g, b, ye, cap):
    n = x1.shape[0]
    tm = TOK_TILE

    def rows(w):
        return pl.BlockSpec((tm, w), lambda i, *_: (i, 0))

    def full(shape):
        return pl.BlockSpec(shape, lambda i, *_: (0,) * len(shape))

    return pl.pallas_call(
        functools.partial(_final_kernel, cap=cap),
        grid_spec=pltpu.PrefetchScalarGridSpec(
            num_scalar_prefetch=2,
            grid=(n // tm,),
            in_specs=[rows(D_MODEL), rows(PLE_DIM), rows(N_EXPERTS), rows(N_EXPERTS),
                      full((D_MODEL, D_MODEL)), full((PLE_DIM, D_MODEL)), full((1, D_MODEL)), full((1, D_MODEL)),
                      pl.BlockSpec(memory_space=pl.ANY)],
            out_specs=rows(D_MODEL),
            scratch_shapes=[pltpu.VMEM((N_EXPERTS, WIN, D_MODEL), BF16), pltpu.VMEM((tm, D_MODEL), F32),
                            pltpu.SemaphoreType.DMA((N_EXPERTS,))],
        ),
        out_shape=jax.ShapeDtypeStruct((n, D_MODEL), F32),
        compiler_params=_params("arbitrary"),
        name="final",
    )(starts, cnts, x1, p2, rank_t, aff_t, wpg, wpp, g, b, ye)


def _rope_tables(seq):
    n_rows = seq // GRID_W
    rows = jnp.repeat(jnp.arange(n_rows, dtype=F32), GRID_W)
    cols = jnp.tile(jnp.arange(GRID_W, dtype=F32), n_rows)
    sec = HEAD_DIM // 2
    inv_freq = ROPE_THETA ** (-jnp.arange(0, sec, 2, dtype=F32) / sec)
    ang_r = rows[:, None] * inv_freq[None, :]
    ang_c = cols[:, None] * inv_freq[None, :]
    cos = jnp.concatenate([jnp.cos(ang_r)] * 2 + [jnp.cos(ang_c)] * 2, axis=-1)
    sin = jnp.concatenate([-jnp.sin(ang_r), jnp.sin(ang_r), -jnp.sin(ang_c), jnp.sin(ang_c)], axis=-1)
    return jnp.tile(cos, (1, 2)), jnp.tile(sin, (1, 2))


def _trunk(x, p, w):
    b, seq, _ = x.shape
    n = b * seq
    cap = max(1, CAPACITY_FACTOR * n // N_EXPERTS)
    assert seq % 256 == 0 and n % (8 * LANES) == 0 and cap % TOK_TILE == 0 and cap >= WIN
    cos_t, sin_t = _rope_tables(seq)

    h, qa, ka, va, qb, kb, vb, gates = _in_proj(
        x.reshape(n, D_MODEL), w["ln_emb_g"], w["ln_emb_b"], w["w_in"], w["q_norm_g"], w["k_norm_g"],
        cos_t, sin_t, w["seg"], seq)

    ya = _gqa_attention(qa.reshape(b, seq, -1), ka.reshape(b, seq, -1), va.reshape(b, seq, -1))
    lam_init = 0.8 - 0.6 * math.exp(-0.3 * 0)
    yb = _diff_attention(qb.reshape(b, seq, -1), kb.reshape(b, seq, -1), vb.reshape(b, seq, -1),
                         w["lambda_q1"], w["lambda_k1"], w["lambda_q2"], w["lambda_k2"], w["subln_g"], lam_init)

    x1, aff = _merge(ya.reshape(n, -1), yb.reshape(n, -1), gates, h, w["w_branch_a"], w["w_branch_b"],
                     w["w_out"], w["ln1_g"], w["ln1_b"], w["wr_hi"], w["wr_lo"])

    nb = n // LANES
    rank2, idx, blkp = _route(aff.reshape(N_EXPERTS * nb, LANES), n, cap)
    xe = _gather(idx.reshape(N_EXPERTS * cap), x1)
    ye = _ffn(xe, w["w_gate"], w["w_up"], w["w_down"], cap)

    starts = blkp.reshape(N_EXPERTS, nb)[:, ::TOK_TILE // LANES]
    ends = jnp.concatenate([starts[:, 1:], jnp.full((N_EXPERTS, 1), cap, I32)], axis=1)
    out = _final(starts.T.reshape(-1), (ends - starts).T.reshape(-1), x1, p.reshape(n, PLE_DIM),
                 rank2.reshape(N_EXPERTS, n).T, aff.T, w["w_ple_gate"], w["w_ple_proj"],
                 w["ln2_g"], w["ln2_b"], ye, cap)
    return out.reshape(b, seq, D_MODEL)


def _prep_weights(ln_emb_g, ln_emb_b, w_in, q_norm_g, k_norm_g, lambda_q1, lambda_k1, lambda_q2, lambda_k2,
                  subln_g, w_branch_a, w_branch_b, w_out, ln1_g, ln1_b, w_router, w_gate, w_up, w_down,
                  w_ple_gate, w_ple_proj, ln2_g, ln2_b):
    i = 0
    seg_i = lax.broadcasted_iota(I32, (A_WIDTH, A_WIDTH), 0) // HEAD_DIM
    seg_j = lax.broadcasted_iota(I32, (A_WIDTH, A_WIDTH), 1) // HEAD_DIM
    wr_t = w_router[i].T
    wr_hi = wr_t.astype(BF16)
    return {
        "ln_emb_g": ln_emb_g.reshape(1, D_MODEL), "ln_emb_b": ln_emb_b.reshape(1, D_MODEL),
        "w_in": w_in[i].astype(BF16),
        "q_norm_g": jnp.tile(q_norm_g[i], 2).reshape(1, LANES), "k_norm_g": jnp.tile(k_norm_g[i], 2).reshape(1, LANES),
        "seg": (seg_i == seg_j).astype(BF16),
        "lambda_q1": lambda_q1[i].reshape(1, HEAD_DIM), "lambda_k1": lambda_k1[i].reshape(1, HEAD_DIM),
        "lambda_q2": lambda_q2[i].reshape(1, HEAD_DIM), "lambda_k2": lambda_k2[i].reshape(1, HEAD_DIM),
        "subln_g": subln_g[i].reshape(1, B_VDIM),
        "w_branch_a": w_branch_a[i].astype(BF16), "w_branch_b": w_branch_b[i].astype(BF16),
        "w_out": w_out[i].astype(BF16),
        "ln1_g": ln1_g[i].reshape(1, D_MODEL), "ln1_b": ln1_b[i].reshape(1, D_MODEL),
        "wr_hi": wr_hi, "wr_lo": (wr_t - wr_hi.astype(F32)).astype(BF16),
        "w_gate": w_gate[i].astype(BF16), "w_up": w_up[i].astype(BF16), "w_down": w_down[i].astype(BF16),
        "w_ple_gate": w_ple_gate[i].astype(BF16), "w_ple_proj": w_ple_proj[i].astype(BF16),
        "ln2_g": ln2_g[i].reshape(1, D_MODEL), "ln2_b": ln2_b[i].reshape(1, D_MODEL),
    }


def kernel(x_prompt, x_sample, p_prompt, p_sample, ln_emb_g, ln_emb_b, w_in, q_norm_g, k_norm_g, lambda_q1, lambda_k1, lambda_q2, lambda_k2, subln_g, w_branch_a, w_branch_b, w_out, ln1_g, ln1_b, w_router, w_gate, w_up, w_down, w_ple_gate, w_ple_proj, ln2_g, ln2_b):
    w = _prep_weights(ln_emb_g, ln_emb_b, w_in, q_norm_g, k_norm_g, lambda_q1, lambda_k1, lambda_q2, lambda_k2,
                      subln_g, w_branch_a, w_branch_b, w_out, ln1_g, ln1_b, w_router, w_gate, w_up, w_down,
                      w_ple_gate, w_ple_proj, ln2_g, ln2_b)
    return (_trunk(x_prompt, p_prompt[0], w), _trunk(x_sample, p_sample[0], w))
```

```python
import functools
import math

import jax
import jax.numpy as jnp
from jax import lax
from jax.experimental import pallas as pl
from jax.experimental.pallas import tpu as pltpu

F32 = jnp.float32
BF16 = jnp.bfloat16
I32 = jnp.int32

D_MODEL = 1024
DEPTH = 1
HEAD_DIM = 64
A_HEADS = 8
A_KV_HEADS = 2
A_GROUP = A_HEADS // A_KV_HEADS
A_WIDTH = A_HEADS * HEAD_DIM
B_HEADS = 4
B_VDIM = 2 * HEAD_DIM
B_WIDTH = B_HEADS * B_VDIM
GRID_W = 64
ROPE_THETA = 10000.0
N_EXPERTS = 16
EXPERT_FF = 2048
CAPACITY_FACTOR = 2
PLE_DIM = 256
LN_EPS = 1e-5
RMS_EPS = 1e-6
ALPHA = (2 * DEPTH) ** 0.25
SCALE = HEAD_DIM ** -0.5
LANES = 128
ROPE_HALF = HEAD_DIM // 4

C_QA = 0
C_KA = C_QA + A_WIDTH
C_VA = C_KA + A_KV_HEADS * HEAD_DIM
C_QB = C_VA + A_KV_HEADS * HEAD_DIM
C_KB = C_QB + 2 * B_HEADS * HEAD_DIM
C_VB = C_KB + 2 * B_HEADS * HEAD_DIM
C_G = C_VB + B_WIDTH
IN_COLS = C_G + 2 * D_MODEL

VMEM_LIMIT = 56 * 1024 * 1024
NT_DIMS = (((1,), (1,)), ((), ()))


def _params(*sem):
    return pltpu.CompilerParams(dimension_semantics=sem, vmem_limit_bytes=VMEM_LIMIT)


def _full(shape):
    n = len(shape)
    return pl.BlockSpec(shape, lambda *_: (0,) * n)


def _layer_norm(x, g, b):
    mu = jnp.mean(x, axis=-1, keepdims=True)
    xc = x - mu
    var = jnp.mean(xc * xc, axis=-1, keepdims=True)
    return xc * lax.rsqrt(var + LN_EPS) * g + b


def _sigmoid(x):
    return 1.0 / (1.0 + jnp.exp(-x))


def _split_bf16(x):
    hi = x.astype(BF16)
    lo = (x - hi.astype(F32)).astype(BF16)
    return hi, lo


def _head_mean_sq(x, seg):
    hi, lo = _split_bf16(x * x)
    s = jnp.dot(hi, seg, preferred_element_type=F32) + jnp.dot(lo, seg, preferred_element_type=F32)
    return s * (1.0 / HEAD_DIM)


def _rope(x, cos, sin, first_half):
    fwd = pltpu.roll(x, LANES - ROPE_HALF, 1)
    bwd = pltpu.roll(x, ROPE_HALF, 1)
    return x * cos + jnp.where(first_half, fwd, bwd) * sin


def _in_proj_kernel(x_ref, lng_ref, lnb_ref, w_ref, qg_ref, kg_ref, cos_ref, sin_ref, seg_ref,
                    h_ref, qa_ref, ka_ref, va_ref, qb_ref, kb_ref, vb_ref, gate_ref):
    xn = _layer_norm(x_ref[...], lng_ref[...], lnb_ref[...])
    h_ref[...] = xn
    xb = xn.astype(BF16)

    def proj(lo, hi):
        return jnp.dot(xb, w_ref[:, lo:hi], preferred_element_type=F32)

    def store_heads(ref, x):
        for i in range(x.shape[1] // HEAD_DIM):
            ref[i] = x[:, i * HEAD_DIM:(i + 1) * HEAD_DIM].astype(BF16)

    cos = cos_ref[...]
    sin = sin_ref[...]
    lane = lax.broadcasted_iota(I32, cos.shape, 1)
    first_half = (lane & ROPE_HALF) == 0

    qa = proj(C_QA, C_KA)
    qn = qa * lax.rsqrt(_head_mean_sq(qa, seg_ref[...]) + RMS_EPS)
    for c in range(A_WIDTH // LANES):
        pair = _rope(qn[:, c * LANES:(c + 1) * LANES] * qg_ref[...], cos, sin, first_half) * SCALE
        qa_ref[2 * c] = pair[:, 0:HEAD_DIM].astype(BF16)
        qa_ref[2 * c + 1] = pair[:, HEAD_DIM:LANES].astype(BF16)

    ka = proj(C_KA, C_VA)
    kn = ka * lax.rsqrt(_head_mean_sq(ka, seg_ref[0:LANES, 0:LANES]) + RMS_EPS)
    store_heads(ka_ref, _rope(kn * kg_ref[...], cos, sin, first_half))
    store_heads(va_ref, proj(C_VA, C_QB))
    store_heads(qb_ref, proj(C_QB, C_KB) * SCALE)
    store_heads(kb_ref, proj(C_KB, C_VB))
    vb = proj(C_VB, C_G)
    for i in range(B_HEADS):
        vb_ref[i] = vb[:, i * B_VDIM:(i + 1) * B_VDIM].astype(BF16)
    gate_ref[...] = proj(C_G, IN_COLS)


def _in_proj(x2, lng, lnb, w_in, qg, kg, cos_t, sin_t, seg, seq):
    n = x2.shape[0]
    tm = 256
    tiles_per_seq = seq // tm

    def rows(w):
        return pl.BlockSpec((tm, w), lambda i: (i, 0))

    def heads(nh, w):
        return pl.BlockSpec((nh, tm, w), lambda i: (0, i, 0))

    tab = pl.BlockSpec((tm, LANES), lambda i: (i % tiles_per_seq, 0))
    head_outs = [(A_HEADS, HEAD_DIM), (A_KV_HEADS, HEAD_DIM), (A_KV_HEADS, HEAD_DIM),
                 (2 * B_HEADS, HEAD_DIM), (2 * B_HEADS, HEAD_DIM), (B_HEADS, B_VDIM)]
    return pl.pallas_call(
        _in_proj_kernel,
        grid=(n // tm,),
        in_specs=[rows(D_MODEL), _full((1, D_MODEL)), _full((1, D_MODEL)), _full((D_MODEL, IN_COLS)),
                  _full((1, LANES)), _full((1, LANES)), tab, tab, _full((A_WIDTH, A_WIDTH))],
        out_specs=[rows(D_MODEL)] + [heads(nh, w) for nh, w in head_outs] + [rows(2 * D_MODEL)],
        out_shape=([jax.ShapeDtypeStruct((n, D_MODEL), F32)]
                   + [jax.ShapeDtypeStruct((nh, n, w), BF16) for nh, w in head_outs]
                   + [jax.ShapeDtypeStruct((n, 2 * D_MODEL), F32)]),
        compiler_params=_params("parallel"),
        name="in_proj",
    )(x2, lng, lnb, w_in, qg, kg, cos_t, sin_t, seg)


def _run_units(n_units, scores, softmax, values):
    assert n_units % 2 == 0 and n_units >= 4
    scores(0, 0)
    scores(1, 1)
    softmax(0, 0)

    def two_units(i, carry):
        t = 2 * i + 2
        scores(t, 0)
        softmax(t - 1, 1)
        values(t - 2, 0)
        scores(t + 1, 1)
        softmax(t, 0)
        values(t - 1, 1)
        return carry

    lax.fori_loop(0, (n_units - 2) // 2, two_units, 0)
    softmax(n_units - 1, 1)
    values(n_units - 2, 0)
    values(n_units - 1, 1)


def _softmax_update(s, m_ref, l_ref, i):
    m_prev = m_ref[i]
    m_new = jnp.maximum(m_prev, jnp.max(s, axis=1, keepdims=True))
    alpha = jnp.exp(m_prev - m_new)
    p = jnp.exp(s - jnp.tile(m_new, (1, s.shape[1] // LANES)))
    l_ref[i] = alpha * l_ref[i] + jnp.sum(p, axis=1, keepdims=True)
    m_ref[i] = m_new
    return p.astype(BF16), alpha


def _init_softmax_state(m_ref, l_ref, acc_ref):
    m_ref[...] = jnp.full(m_ref.shape, -jnp.inf, F32)
    l_ref[...] = jnp.zeros(l_ref.shape, F32)
    acc_ref[...] = jnp.zeros(acc_ref.shape, F32)


def _gqa_kernel(q_ref, k_ref, v_ref, o_ref, s_buf, p_buf, a_buf, m_ref, l_ref, acc_ref, *, tk, n_sub, n_steps):
    step = pl.program_id(2)

    @pl.when(step == 0)
    def _():
        _init_softmax_state(m_ref, l_ref, acc_ref)

    def keys(u):
        h = u & (A_HEADS - 1)
        return h, h >> 2, pl.ds(pl.multiple_of((u >> 3) * tk, tk), tk)

    def scores(u, slot):
        h, kv, rows = keys(u)
        s_buf[slot] = lax.dot_general(q_ref[h], k_ref[kv, rows, :], NT_DIMS, preferred_element_type=F32)

    def softmax(u, slot):
        p, alpha = _softmax_update(s_buf[slot], m_ref, l_ref, u & (A_HEADS - 1))
        p_buf[slot] = p
        a_buf[slot] = alpha

    def values(u, slot):
        h, kv, rows = keys(u)
        pv = jnp.dot(p_buf[slot], v_ref[kv, rows, :], preferred_element_type=F32)
        acc_ref[h] = a_buf[slot][:, 0:HEAD_DIM] * acc_ref[h] + pv

    _run_units(n_sub * A_HEADS, scores, softmax, values)

    @pl.when(step == n_steps - 1)
    def _():
        for h in range(A_HEADS):
            o_ref[:, h * HEAD_DIM:(h + 1) * HEAD_DIM] = (acc_ref[h] / l_ref[h][:, 0:HEAD_DIM]).astype(BF16)


def _gqa_attention(qa, ka, va):
    _, b, seq, _ = qa.shape
    tq = min(512, seq)
    tk = min(1024, seq)
    span = min(4096, seq)
    n_steps = seq // span
    return pl.pallas_call(
        functools.partial(_gqa_kernel, tk=tk, n_sub=span // tk, n_steps=n_steps),
        grid=(b, seq // tq, n_steps),
        in_specs=[pl.BlockSpec((A_HEADS, None, tq, HEAD_DIM), lambda bi, qi, si: (0, bi, qi, 0)),
                  pl.BlockSpec((A_KV_HEADS, None, span, HEAD_DIM), lambda bi, qi, si: (0, bi, si, 0)),
                  pl.BlockSpec((A_KV_HEADS, None, span, HEAD_DIM), lambda bi, qi, si: (0, bi, si, 0))],
        out_specs=pl.BlockSpec((None, tq, A_WIDTH), lambda bi, qi, si: (bi, qi, 0)),
        out_shape=jax.ShapeDtypeStruct((b, seq, A_WIDTH), BF16),
        scratch_shapes=[pltpu.VMEM((2, tq, tk), F32), pltpu.VMEM((2, tq, tk), BF16), pltpu.VMEM((2, tq, LANES), F32),
                        pltpu.VMEM((A_HEADS, tq, LANES), F32), pltpu.VMEM((A_HEADS, tq, LANES), F32),
                        pltpu.VMEM((A_HEADS, tq, HEAD_DIM), F32)],
        compiler_params=_params("parallel", "parallel", "arbitrary"),
        name="gqa_attn",
    )(qa, ka, va)


def _diff_kernel(slope_ref, q_ref, k_ref, v_ref, lq1_ref, lk1_ref, lq2_ref, lk2_ref, g_ref, o_ref,
                 s_buf, p_buf, a_buf, d_ref, m_ref, l_ref, acc_ref, *, tq, tk, span, n_steps, lam_init):
    qi = pl.program_id(1)
    step = pl.program_id(2)

    @pl.when(step == 0)
    def _():
        _init_softmax_state(m_ref, l_ref, acc_ref)
        d_ref[...] = (lax.broadcasted_iota(I32, (tq, tk), 0) - lax.broadcasted_iota(I32, (tq, tk), 1)).astype(F32)

    def keys(u):
        j = u >> 2
        return u & (B_HEADS - 1), j, pl.ds(pl.multiple_of(j * tk, tk), tk)

    def scores(u, slot):
        h, j, rows = keys(u)
        offset = (qi * tq - step * span - j * tk).astype(F32)
        bias = jnp.abs(d_ref[...] + offset) * slope_ref[h]
        for m in range(2):
            i = m * B_HEADS + h
            s = lax.dot_general(q_ref[i], k_ref[i, rows, :], NT_DIMS, preferred_element_type=F32)
            s_buf[slot, m] = s - bias

    def softmax(u, slot):
        h = u & (B_HEADS - 1)
        for m in range(2):
            p, alpha = _softmax_update(s_buf[slot, m], m_ref, l_ref, m * B_HEADS + h)
            p_buf[slot, m] = p
            a_buf[slot, m] = alpha

    def values(u, slot):
        h, _, rows = keys(u)
        v = v_ref[h, rows, :]
        for m in range(2):
            i = m * B_HEADS + h
            acc_ref[i] = a_buf[slot, m] * acc_ref[i] + jnp.dot(p_buf[slot, m], v, preferred_element_type=F32)

    _run_units((span // tk) * B_HEADS, scores, softmax, values)

    @pl.when(step == n_steps - 1)
    def _():
        lam = (jnp.exp(jnp.sum(lq1_ref[...] * lk1_ref[...], axis=1, keepdims=True))
               - jnp.exp(jnp.sum(lq2_ref[...] * lk2_ref[...], axis=1, keepdims=True)) + lam_init)
        for h in range(B_HEADS):
            y = acc_ref[h] / l_ref[h] - lam * (acc_ref[B_HEADS + h] / l_ref[B_HEADS + h])
            ms = jnp.mean(y * y, axis=1, keepdims=True)
            y = y * lax.rsqrt(ms + RMS_EPS) * g_ref[...] * (1.0 - lam_init)
            o_ref[:, h * B_VDIM:(h + 1) * B_VDIM] = y.astype(BF16)


def _diff_attention(qb, kb, vb, lq1, lk1, lq2, lk2, subln_g, lam_init):
    _, b, seq, _ = qb.shape
    tq = min(512, seq)
    tk = min(512, seq)
    span = min(2048, seq)
    n_steps = seq // span
    nh = 2 * B_HEADS
    slopes = jnp.asarray([2.0 ** (-8.0 * (h + 1) / B_HEADS) for h in range(B_HEADS)], F32)
    vec = pl.BlockSpec((1, HEAD_DIM), lambda bi, qi, si, *_: (0, 0))
    return pl.pallas_call(
        functools.partial(_diff_kernel, tq=tq, tk=tk, span=span, n_steps=n_steps, lam_init=lam_init),
        grid_spec=pltpu.PrefetchScalarGridSpec(
            num_scalar_prefetch=1,
            grid=(b, seq // tq, n_steps),
            in_specs=[pl.BlockSpec((nh, None, tq, HEAD_DIM), lambda bi, qi, si, *_: (0, bi, qi, 0)),
                      pl.BlockSpec((nh, None, span, HEAD_DIM), lambda bi, qi, si, *_: (0, bi, si, 0)),
                      pl.BlockSpec((B_HEADS, None, span, B_VDIM), lambda bi, qi, si, *_: (0, bi, si, 0)),
                      vec, vec, vec, vec, pl.BlockSpec((1, B_VDIM), lambda bi, qi, si, *_: (0, 0))],
            out_specs=pl.BlockSpec((None, tq, B_WIDTH), lambda bi, qi, si, *_: (bi, qi, 0)),
            scratch_shapes=[pltpu.VMEM((2, 2, tq, tk), F32), pltpu.VMEM((2, 2, tq, tk), BF16),
                            pltpu.VMEM((2, 2, tq, LANES), F32), pltpu.VMEM((tq, tk), F32),
                            pltpu.VMEM((nh, tq, LANES), F32), pltpu.VMEM((nh, tq, LANES), F32),
                            pltpu.VMEM((nh, tq, B_VDIM), F32)],
        ),
        out_shape=jax.ShapeDtypeStruct((b, seq, B_WIDTH), BF16),
        compiler_params=_params("parallel", "parallel", "arbitrary"),
        name="diff_attn",
    )(slopes, qb, kb, vb, lq1, lk1, lq2, lk2, subln_g)


def _merge_kernel(ya_ref, yb_ref, gate_ref, h_ref, wa_ref, wb_ref, wo_ref, g_ref, b_ref,
                  wrh_ref, wrl_ref, x1_ref, aff_ref):
    ma = jnp.dot(ya_ref[...], wa_ref[...], preferred_element_type=F32)
    mb = jnp.dot(yb_ref[...], wb_ref[...], preferred_element_type=F32)
    merged = _sigmoid(gate_ref[:, 0:D_MODEL]) * ma + _sigmoid(gate_ref[:, D_MODEL:2 * D_MODEL]) * mb
    out = jnp.dot(merged.astype(BF16), wo_ref[...], preferred_element_type=F32)
    x1 = _layer_norm(ALPHA * h_ref[...] + out, g_ref[...], b_ref[...])
    x1_ref[...] = x1
    xh, xl = _split_bf16(x1)
    wh = wrh_ref[...]
    logits = (lax.dot_general(wh, xh, NT_DIMS, preferred_element_type=F32)
              + lax.dot_general(wh, xl, NT_DIMS, preferred_element_type=F32)
              + lax.dot_general(wrl_ref[...], xh, NT_DIMS, preferred_element_type=F32))
    e = jnp.exp(logits - jnp.max(logits, axis=0, keepdims=True))
    aff_ref[...] = e / jnp.sum(e, axis=0, keepdims=True)


def _merge(ya, yb, gates, h, wa, wb, wo, g, b, wrh, wrl):
    n = h.shape[0]
    tm = 256

    def rows(w):
        return pl.BlockSpec((tm, w), lambda i: (i, 0))

    return pl.pallas_call(
        _merge_kernel,
        grid=(n // tm,),
        in_specs=[rows(A_WIDTH), rows(B_WIDTH), rows(2 * D_MODEL), rows(D_MODEL),
                  _full((A_WIDTH, D_MODEL)), _full((B_WIDTH, D_MODEL)), _full((D_MODEL, D_MODEL)),
                  _full((1, D_MODEL)), _full((1, D_MODEL)),
                  _full((N_EXPERTS, D_MODEL)), _full((N_EXPERTS, D_MODEL))],
        out_specs=[rows(D_MODEL), pl.BlockSpec((N_EXPERTS, tm), lambda i: (0, i))],
        out_shape=[jax.ShapeDtypeStruct((n, D_MODEL), F32), jax.ShapeDtypeStruct((N_EXPERTS, n), F32)],
        compiler_params=_params("parallel"),
        name="merge",
    )(ya, yb, gates, h, wa, wb, wo, g, b, wrh, wrl)


def _route_kernel(aff_ref, rank_ref, idx_ref, blkp_ref, sel_scr, wi_scr, p_scr, *, nb, cap):
    ne = N_EXPERTS
    shape3 = (ne, nb, LANES)
    bits = pltpu.bitcast(aff_ref[...], I32).reshape(shape3)
    capf = float(cap)

    def count(mask3):
        part = jnp.sum(jnp.where(mask3, 1.0, 0.0), axis=1, keepdims=True)
        return jnp.sum(part, axis=2, keepdims=True)

    def search(i, thr):
        cand = thr | jnp.left_shift(jnp.int32(1), 30 - i)
        return jnp.where(count(bits >= cand) >= capf, cand, thr)

    thr = lax.fori_loop(0, 31, search, jnp.zeros((ne, 1, 1), I32))
    gt = bits > thr
    eq = bits == thr
    need = capf - count(gt)

    li = lax.broadcasted_iota(I32, (LANES, LANES), 0)
    lj = lax.broadcasted_iota(I32, (LANES, LANES), 1)
    upper_incl = jnp.where(li <= lj, 1.0, 0.0).astype(BF16)
    lower_incl = jnp.where(lj <= li, 1.0, 0.0).astype(BF16)
    bi = lax.broadcasted_iota(I32, (nb, nb), 0)
    bj = lax.broadcasted_iota(I32, (nb, nb), 1)
    lower_strict = jnp.where(bj < bi, 1.0, 0.0).astype(BF16)
    upper_strict = jnp.where(bi < bj, 1.0, 0.0).astype(BF16)

    def prefix(mask2):
        wi = jnp.dot(mask2.astype(BF16), upper_incl, preferred_element_type=F32)
        tot = jnp.broadcast_to(wi[:, LANES - 1:LANES], wi.shape).astype(BF16)
        blk = [jnp.dot(lower_strict, tot[e * nb:(e + 1) * nb], preferred_element_type=F32)
               for e in range(ne)]
        return wi, jnp.concatenate(blk, axis=0)

    eqf = jnp.where(eq, 1.0, 0.0).reshape(ne * nb, LANES)
    wi, bp = prefix(eqf)
    eq_rank = (wi + bp - eqf).reshape(shape3)
    sel = gt | (eq & (eq_rank < need))
    self_ = jnp.where(sel, 1.0, 0.0).reshape(ne * nb, LANES)
    wi, bp = prefix(self_)
    rank = wi + bp - self_
    rank_ref[...] = jnp.where(self_ > 0.0, rank, -1.0).astype(I32)
    sel_scr[...] = self_
    wi_scr[...] = wi
    p_scr[...] = bp

    chunk = min(cap, 1024)
    ones8 = jnp.ones((8, LANES), BF16)
    b_col = lax.broadcasted_iota(I32, (nb, 1), 0).astype(F32)

    def per_expert(e, carry):
        r0 = pl.multiple_of(e * nb, 8)
        sel_e = sel_scr[pl.ds(r0, nb), :].astype(BF16)
        wi_e = wi_scr[pl.ds(r0, nb), :]
        p_e = p_scr[pl.ds(r0, nb), :][:, 0:1]
        p_end = p_e + wi_e[:, LANES - 1:LANES]
        wi_t = lax.dot_general(lower_incl, sel_e, NT_DIMS, preferred_element_type=F32).astype(BF16)
        tot_row = lax.dot_general(ones8, sel_e, NT_DIMS, preferred_element_type=F32)
        p_row = jnp.dot(tot_row.astype(BF16), upper_strict, preferred_element_type=F32)
        blkp_ref[e] = p_row[0:1, :].astype(I32)
        for c in range(cap // chunk):
            slot = (lax.broadcasted_iota(I32, (1, chunk), 1) + c * chunk).astype(F32)
            hot = (p_e <= slot) & (slot < p_end)
            w_slot = jnp.dot(wi_t, jnp.where(hot, 1.0, 0.0).astype(BF16), preferred_element_type=F32)
            p_slot = jnp.sum(jnp.where(hot, p_e, 0.0), axis=0, keepdims=True)
            b_slot = jnp.sum(jnp.where(hot, b_col, 0.0), axis=0, keepdims=True)
            lane = jnp.sum(jnp.where(w_slot <= slot - p_slot, 1.0, 0.0), axis=0, keepdims=True)
            idx_ref[e, :, c * chunk:(c + 1) * chunk] = (b_slot * LANES + lane).astype(I32)
        return carry

    lax.fori_loop(0, ne, per_expert, 0)


def _route(aff2, n, cap):
    nb = n // LANES
    return pl.pallas_call(
        functools.partial(_route_kernel, nb=nb, cap=cap),
        out_shape=[jax.ShapeDtypeStruct((N_EXPERTS * nb, LANES), I32),
                   jax.ShapeDtypeStruct((N_EXPERTS, 1, cap), I32),
                   jax.ShapeDtypeStruct((N_EXPERTS, 1, nb), I32)],
        scratch_shapes=[pltpu.VMEM((N_EXPERTS * nb, LANES), F32)] * 3,
        compiler_params=pltpu.CompilerParams(vmem_limit_bytes=VMEM_LIMIT),
        name="route",
    )(aff2)


def _gather_kernel(idx_ref, x_hbm, o_ref, sem, *, rows):
    base = pl.program_id(0) * rows

    def row_copy(r):
        return pltpu.make_async_copy(x_hbm.at[pl.ds(idx_ref[base + r], 1), :], o_ref.at[pl.ds(r, 1), :], sem.at[0])

    def start(r, c):
        row_copy(r).start()
        return c

    def wait(r, c):
        row_copy(r).wait()
        return c

    lax.fori_loop(0, rows, start, 0)
    lax.fori_loop(0, rows, wait, 0)


def _gather(idx_flat, x1):
    rows = 256
    total = idx_flat.shape[0]
    return pl.pallas_call(
        functools.partial(_gather_kernel, rows=rows),
        grid_spec=pltpu.PrefetchScalarGridSpec(
            num_scalar_prefetch=1,
            grid=(total // rows,),
            in_specs=[pl.BlockSpec(memory_space=pl.ANY)],
            out_specs=pl.BlockSpec((rows, D_MODEL), lambda i, idx: (i, 0)),
            scratch_shapes=[pltpu.SemaphoreType.DMA((1,))],
        ),
        out_shape=jax.ShapeDtypeStruct((total, D_MODEL), F32),
        compiler_params=_params("arbitrary"),
        name="gather",
    )(idx_flat, x1)


def _ffn_kernel(x_ref, wg_ref, wu_ref, wd_ref, y_ref):
    x = x_ref[...].astype(BF16)
    fc = 512
    acc = None
    for c in range(EXPERT_FF // fc):
        sl = slice(c * fc, (c + 1) * fc)
        a = jnp.dot(x, wg_ref[:, sl], preferred_element_type=F32)
        u = jnp.dot(x, wu_ref[:, sl], preferred_element_type=F32)
        mid = (a * _sigmoid(a) * u).astype(BF16)
        part = jnp.dot(mid, wd_ref[sl, :], preferred_element_type=F32)
        acc = part if acc is None else acc + part
    y_ref[...] = acc.astype(BF16)


def _ffn(xe, wg, wu, wd, cap):
    tm = min(512, cap)
    tiles = cap // tm
    return pl.pallas_call(
        _ffn_kernel,
        grid=(N_EXPERTS, tiles),
        in_specs=[pl.BlockSpec((tm, D_MODEL), lambda e, i: (e * tiles + i, 0)),
                  pl.BlockSpec((None, D_MODEL, EXPERT_FF), lambda e, i: (e, 0, 0)),
                  pl.BlockSpec((None, D_MODEL, EXPERT_FF), lambda e, i: (e, 0, 0)),
                  pl.BlockSpec((None, EXPERT_FF, D_MODEL), lambda e, i: (e, 0, 0))],
        out_specs=pl.BlockSpec((tm, D_MODEL), lambda e, i: (e * tiles + i, 0)),
        out_shape=jax.ShapeDtypeStruct((N_EXPERTS * cap, D_MODEL), BF16),
        compiler_params=_params("parallel", "parallel"),
        name="ffn",
    )(xe, wg, wu, wd)


TOK_TILE = 128
ROW_ALIGN = 16
WIN = TOK_TILE + ROW_ALIGN


def _final_kernel(start_ref, cnt_ref, x1_ref, p_ref, rank_ref, aff_ref, wpg_ref, wpp_ref, g_ref, b_ref,
                  ye_hbm, o_ref, buf, moe_ref, sem, *, cap):
    t = pl.program_id(0)

    def window(e):
        first = start_ref[t * N_EXPERTS + e]
        ws = jnp.minimum((first // ROW_ALIGN) * ROW_ALIGN, cap - WIN)
        return pl.multiple_of(ws, ROW_ALIGN)

    def copy(e):
        return pltpu.make_async_copy(ye_hbm.at[pl.ds(e * cap + window(e), WIN), :], buf.at[e], sem.at[e])

    for e in range(N_EXPERTS):
        @pl.when(cnt_ref[t * N_EXPERTS + e] > 0)
        def _():
            copy(e).start()

    x1 = x1_ref[...]
    gate = _sigmoid(jnp.dot(x1.astype(BF16), wpg_ref[...], preferred_element_type=F32))
    ple = gate * jnp.dot(p_ref[...].astype(BF16), wpp_ref[...], preferred_element_type=F32)
    moe_ref[...] = ALPHA * x1 + ple

    slot = lax.broadcasted_iota(I32, (1, WIN), 1)
    for e in range(N_EXPERTS):
        @pl.when(cnt_ref[t * N_EXPERTS + e] > 0)
        def _():
            copy(e).wait()
            hot = rank_ref[:, e:e + 1] == slot + window(e)
            y = jnp.dot(jnp.where(hot, 1.0, 0.0).astype(BF16), buf[e], preferred_element_type=F32)
            moe_ref[...] += aff_ref[:, e:e + 1] * y

    o_ref[...] = _layer_norm(moe_ref[...], g_ref[...], b_ref[...])


def _final(starts, cnts, x1, p2, rank_t, aff_t, wpg, wpp, g, b, ye, cap):
    n = x1.shape[0]
    tm = TOK_TILE

    def rows(w):
        return pl.BlockSpec((tm, w), lambda i, *_: (i, 0))

    def full(shape):
        return pl.BlockSpec(shape, lambda i, *_: (0,) * len(shape))

    return pl.pallas_call(
        functools.partial(_final_kernel, cap=cap),
        grid_spec=pltpu.PrefetchScalarGridSpec(
            num_scalar_prefetch=2,
            grid=(n // tm,),
            in_specs=[rows(D_MODEL), rows(PLE_DIM), rows(N_EXPERTS), rows(N_EXPERTS),
                      full((D_MODEL, D_MODEL)), full((PLE_DIM, D_MODEL)), full((1, D_MODEL)), full((1, D_MODEL)),
                      pl.BlockSpec(memory_space=pl.ANY)],
            out_specs=rows(D_MODEL),
            scratch_shapes=[pltpu.VMEM((N_EXPERTS, WIN, D_MODEL), BF16), pltpu.VMEM((tm, D_MODEL), F32),
                            pltpu.SemaphoreType.DMA((N_EXPERTS,))],
        ),
        out_shape=jax.ShapeDtypeStruct((n, D_MODEL), F32),
        compiler_params=_params("arbitrary"),
        name="final",
    )(starts, cnts, x1, p2, rank_t, aff_t, wpg, wpp, g, b, ye)


def _rope_tables(seq):
    n_rows = seq // GRID_W
    rows = jnp.repeat(jnp.arange(n_rows, dtype=F32), GRID_W)
    cols = jnp.tile(jnp.arange(GRID_W, dtype=F32), n_rows)
    sec = HEAD_DIM // 2
    inv_freq = ROPE_THETA ** (-jnp.arange(0, sec, 2, dtype=F32) / sec)
    ang_r = rows[:, None] * inv_freq[None, :]
    ang_c = cols[:, None] * inv_freq[None, :]
    cos = jnp.concatenate([jnp.cos(ang_r)] * 2 + [jnp.cos(ang_c)] * 2, axis=-1)
    sin = jnp.concatenate([-jnp.sin(ang_r), jnp.sin(ang_r), -jnp.sin(ang_c), jnp.sin(ang_c)], axis=-1)
    return jnp.tile(cos, (1, 2)), jnp.tile(sin, (1, 2))


def _trunk(x, p, w):
    b, seq, _ = x.shape
    n = b * seq
    cap = max(1, CAPACITY_FACTOR * n // N_EXPERTS)
    assert seq % 256 == 0 and n % (8 * LANES) == 0 and cap % TOK_TILE == 0 and cap >= WIN
    cos_t, sin_t = _rope_tables(seq)

    h, qa, ka, va, qb, kb, vb, gates = _in_proj(
        x.reshape(n, D_MODEL), w["ln_emb_g"], w["ln_emb_b"], w["w_in"], w["q_norm_g"], w["k_norm_g"],
        cos_t, sin_t, w["seg"], seq)

    def per_batch(t):
        return t.reshape(t.shape[0], b, seq, t.shape[2])

    ya = _gqa_attention(per_batch(qa), per_batch(ka), per_batch(va))
    lam_init = 0.8 - 0.6 * math.exp(-0.3 * 0)
    yb = _diff_attention(per_batch(qb), per_batch(kb), per_batch(vb),
                         w["lambda_q1"], w["lambda_k1"], w["lambda_q2"], w["lambda_k2"], w["subln_g"], lam_init)

    x1, aff = _merge(ya.reshape(n, -1), yb.reshape(n, -1), gates, h, w["w_branch_a"], w["w_branch_b"],
                     w["w_out"], w["ln1_g"], w["ln1_b"], w["wr_hi"], w["wr_lo"])

    nb = n // LANES
    rank2, idx, blkp = _route(aff.reshape(N_EXPERTS * nb, LANES), n, cap)
    xe = _gather(idx.reshape(N_EXPERTS * cap), x1)
    ye = _ffn(xe, w["w_gate"], w["w_up"], w["w_down"], cap)

    starts = blkp.reshape(N_EXPERTS, nb)[:, ::TOK_TILE // LANES]
    ends = jnp.concatenate([starts[:, 1:], jnp.full((N_EXPERTS, 1), cap, I32)], axis=1)
    out = _final(starts.T.reshape(-1), (ends - starts).T.reshape(-1), x1, p.reshape(n, PLE_DIM),
                 rank2.reshape(N_EXPERTS, n).T, aff.T, w["w_ple_gate"], w["w_ple_proj"],
                 w["ln2_g"], w["ln2_b"], ye, cap)
    return out.reshape(b, seq, D_MODEL)


def _prep_weights(ln_emb_g, ln_emb_b, w_in, q_norm_g, k_norm_g, lambda_q1, lambda_k1, lambda_q2, lambda_k2,
                  subln_g, w_branch_a, w_branch_b, w_out, ln1_g, ln1_b, w_router, w_gate, w_up, w_down,
                  w_ple_gate, w_ple_proj, ln2_g, ln2_b):
    i = 0
    seg_i = lax.broadcasted_iota(I32, (A_WIDTH, A_WIDTH), 0) // HEAD_DIM
    seg_j = lax.broadcasted_iota(I32, (A_WIDTH, A_WIDTH), 1) // HEAD_DIM
    wr_t = w_router[i].T
    wr_hi = wr_t.astype(BF16)
    return {
        "ln_emb_g": ln_emb_g.reshape(1, D_MODEL), "ln_emb_b": ln_emb_b.reshape(1, D_MODEL),
        "w_in": w_in[i].astype(BF16),
        "q_norm_g": jnp.tile(q_norm_g[i], 2).reshape(1, LANES), "k_norm_g": jnp.tile(k_norm_g[i], 2).reshape(1, LANES),
        "seg": (seg_i == seg_j).astype(BF16),
        "lambda_q1": lambda_q1[i].reshape(1, HEAD_DIM), "lambda_k1": lambda_k1[i].reshape(1, HEAD_DIM),
        "lambda_q2": lambda_q2[i].reshape(1, HEAD_DIM), "lambda_k2": lambda_k2[i].reshape(1, HEAD_DIM),
        "subln_g": subln_g[i].reshape(1, B_VDIM),
        "w_branch_a": w_branch_a[i].astype(BF16), "w_branch_b": w_branch_b[i].astype(BF16),
        "w_out": w_out[i].astype(BF16),
        "ln1_g": ln1_g[i].reshape(1, D_MODEL), "ln1_b": ln1_b[i].reshape(1, D_MODEL),
        "wr_hi": wr_hi, "wr_lo": (wr_t - wr_hi.astype(F32)).astype(BF16),
        "w_gate": w_gate[i].astype(BF16), "w_up": w_up[i].astype(BF16), "w_down": w_down[i].astype(BF16),
        "w_ple_gate": w_ple_gate[i].astype(BF16), "w_ple_proj": w_ple_proj[i].astype(BF16),
        "ln2_g": ln2_g[i].reshape(1, D_MODEL), "ln2_b": ln2_b[i].reshape(1, D_MODEL),
    }


def kernel(x_prompt, x_sample, p_prompt, p_sample, ln_emb_g, ln_emb_b, w_in, q_norm_g, k_norm_g, lambda_q1, lambda_k1, lambda_q2, lambda_k2, subln_g, w_branch_a, w_branch_b, w_out, ln1_g, ln1_b, w_router, w_gate, w_up, w_down, w_ple_gate, w_ple_proj, ln2_g, ln2_b):
    w = _prep_weights(ln_emb_g, ln_emb_b, w_in, q_norm_g, k_norm_g, lambda_q1, lambda_k1, lambda_q2, lambda_k2,
                      subln_g, w_branch_a, w_branch_b, w_out, ln1_g, ln1_b, w_router, w_gate, w_up, w_down,
                      w_ple_gate, w_ple_proj, ln2_g, ln2_b)
    return (_trunk(x_prompt, p_prompt[0], w), _trunk(x_sample, p_sample[0], w))
```

```python
import functools
import math

import jax
import jax.numpy as jnp
from jax import lax
from jax.experimental import pallas as pl
from jax.experimental.pallas import tpu as pltpu

F32 = jnp.float32
BF16 = jnp.bfloat16
I32 = jnp.int32

D_MODEL = 1024
DEPTH = 1
HEAD_DIM = 64
A_HEADS = 8
A_KV_HEADS = 2
A_GROUP = A_HEADS // A_KV_HEADS
A_WIDTH = A_HEADS * HEAD_DIM
B_HEADS = 4
B_VDIM = 2 * HEAD_DIM
B_WIDTH = B_HEADS * B_VDIM
GRID_W = 64
ROPE_THETA = 10000.0
N_EXPERTS = 16
EXPERT_FF = 2048
CAPACITY_FACTOR = 2
PLE_DIM = 256
LN_EPS = 1e-5
RMS_EPS = 1e-6
ALPHA = (2 * DEPTH) ** 0.25
LOG2E = math.log2(math.e)
QSCALE = HEAD_DIM ** -0.5 * LOG2E
LANES = 128
ROPE_HALF = HEAD_DIM // 4

C_QA = 0
C_KA = C_QA + A_WIDTH
C_VA = C_KA + A_KV_HEADS * HEAD_DIM
C_QB = C_VA + A_KV_HEADS * HEAD_DIM
C_KB = C_QB + 2 * B_HEADS * HEAD_DIM
C_VB = C_KB + 2 * B_HEADS * HEAD_DIM
C_G = C_VB + B_WIDTH
IN_COLS = C_G + 2 * D_MODEL

VMEM_LIMIT = 56 * 1024 * 1024
NT_DIMS = (((1,), (1,)), ((), ()))


def _params(*sem):
    return pltpu.CompilerParams(dimension_semantics=sem, vmem_limit_bytes=VMEM_LIMIT)


def _full(shape):
    n = len(shape)
    return pl.BlockSpec(shape, lambda *_: (0,) * n)


def _layer_norm(x, g, b):
    mu = jnp.mean(x, axis=-1, keepdims=True)
    xc = x - mu
    var = jnp.mean(xc * xc, axis=-1, keepdims=True)
    return xc * lax.rsqrt(var + LN_EPS) * g + b


def _sigmoid(x):
    return 1.0 / (1.0 + jnp.exp(-x))


def _split_bf16(x):
    hi = x.astype(BF16)
    lo = (x - hi.astype(F32)).astype(BF16)
    return hi, lo


def _head_mean_sq(x, seg):
    hi, lo = _split_bf16(x * x)
    s = jnp.dot(hi, seg, preferred_element_type=F32) + jnp.dot(lo, seg, preferred_element_type=F32)
    return s * (1.0 / HEAD_DIM)


def _rope(x, cos, sin, first_half):
    fwd = pltpu.roll(x, LANES - ROPE_HALF, 1)
    bwd = pltpu.roll(x, ROPE_HALF, 1)
    return x * cos + jnp.where(first_half, fwd, bwd) * sin


def _in_proj_kernel(x_ref, lng_ref, lnb_ref, w_ref, qg_ref, kg_ref, cos_ref, sin_ref, seg_ref,
                    h_ref, qa_ref, ka_ref, va_ref, qb_ref, kb_ref, vb_ref, gate_ref):
    xn = _layer_norm(x_ref[...], lng_ref[...], lnb_ref[...])
    h_ref[...] = xn
    xb = xn.astype(BF16)

    def proj(lo, hi):
        return jnp.dot(xb, w_ref[:, lo:hi], preferred_element_type=F32)

    def store_heads(ref, x):
        for i in range(x.shape[1] // HEAD_DIM):
            ref[i] = x[:, i * HEAD_DIM:(i + 1) * HEAD_DIM].astype(BF16)

    cos = cos_ref[...]
    sin = sin_ref[...]
    lane = lax.broadcasted_iota(I32, cos.shape, 1)
    first_half = (lane & ROPE_HALF) == 0

    qa = proj(C_QA, C_KA)
    qn = qa * lax.rsqrt(_head_mean_sq(qa, seg_ref[...]) + RMS_EPS)
    for c in range(A_WIDTH // LANES):
        pair = _rope(qn[:, c * LANES:(c + 1) * LANES] * qg_ref[...], cos, sin, first_half) * QSCALE
        qa_ref[2 * c] = pair[:, 0:HEAD_DIM].astype(BF16)
        qa_ref[2 * c + 1] = pair[:, HEAD_DIM:LANES].astype(BF16)

    ka = proj(C_KA, C_VA)
    kn = ka * lax.rsqrt(_head_mean_sq(ka, seg_ref[0:LANES, 0:LANES]) + RMS_EPS)
    store_heads(ka_ref, _rope(kn * kg_ref[...], cos, sin, first_half))
    one_at_0 = jnp.where(lane == 0, 1.0, 0.0)
    va = proj(C_VA, C_QB)
    for g in range(A_KV_HEADS):
        v_first = va if g == 0 else pltpu.roll(va, HEAD_DIM, 1)
        va_ref[g] = jnp.where(lane < HEAD_DIM, v_first, jnp.where(lane == HEAD_DIM, 1.0, 0.0)).astype(BF16)
    store_heads(qb_ref, proj(C_QB, C_KB) * QSCALE)
    store_heads(kb_ref, proj(C_KB, C_VB))
    vb = proj(C_VB, C_G)
    for i in range(B_HEADS):
        vb_ref[i, :, 0:B_VDIM] = vb[:, i * B_VDIM:(i + 1) * B_VDIM].astype(BF16)
        vb_ref[i, :, B_VDIM:2 * B_VDIM] = one_at_0.astype(BF16)
    gate_ref[...] = proj(C_G, IN_COLS)


def _in_proj(x2, lng, lnb, w_in, qg, kg, cos_t, sin_t, seg, seq):
    n = x2.shape[0]
    tm = 256
    tiles_per_seq = seq // tm

    def rows(w):
        return pl.BlockSpec((tm, w), lambda i: (i, 0))

    def heads(nh, w):
        return pl.BlockSpec((nh, tm, w), lambda i: (0, i, 0))

    tab = pl.BlockSpec((tm, LANES), lambda i: (i % tiles_per_seq, 0))
    head_outs = [(A_HEADS, HEAD_DIM), (A_KV_HEADS, HEAD_DIM), (A_KV_HEADS, LANES),
                 (2 * B_HEADS, HEAD_DIM), (2 * B_HEADS, HEAD_DIM), (B_HEADS, 2 * B_VDIM)]
    return pl.pallas_call(
        _in_proj_kernel,
        grid=(n // tm,),
        in_specs=[rows(D_MODEL), _full((1, D_MODEL)), _full((1, D_MODEL)), _full((D_MODEL, IN_COLS)),
                  _full((1, LANES)), _full((1, LANES)), tab, tab, _full((A_WIDTH, A_WIDTH))],
        out_specs=[rows(D_MODEL)] + [heads(nh, w) for nh, w in head_outs] + [rows(2 * D_MODEL)],
        out_shape=([jax.ShapeDtypeStruct((n, D_MODEL), F32)]
                   + [jax.ShapeDtypeStruct((nh, n, w), BF16) for nh, w in head_outs]
                   + [jax.ShapeDtypeStruct((n, 2 * D_MODEL), F32)]),
        compiler_params=_params("parallel"),
        name="in_proj",
    )(x2, lng, lnb, w_in, qg, kg, cos_t, sin_t, seg)


PIPE_SLOTS = 2


def _run_units(n_units, scores, softmax, values):
    assert n_units % PIPE_SLOTS == 0 and n_units >= 2 * PIPE_SLOTS
    scores(0, 0)
    scores(1, 1)
    softmax(0, 0)

    def two_units(i, carry):
        t = PIPE_SLOTS * (i + 1)
        scores(t, 0)
        softmax(t - 1, 1)
        values(t - 2, 0)
        scores(t + 1, 1)
        softmax(t, 0)
        values(t - 1, 1)
        return carry

    lax.fori_loop(0, n_units // PIPE_SLOTS - 1, two_units, 0)
    softmax(n_units - 1, 1)
    values(n_units - 2, 0)
    values(n_units - 1, 1)


def _softmax_update(load_s, m_ref, i):
    m_prev = m_ref[i]
    m_new = jnp.maximum(m_prev, jnp.max(load_s(), axis=1, keepdims=True))
    alpha = jnp.exp2(m_prev - m_new)
    s = load_s()
    p = jnp.exp2(s - jnp.tile(m_new, (1, s.shape[1] // LANES)))
    m_ref[i] = m_new
    return p.astype(BF16), alpha


def _init_softmax_state(m_ref, acc_ref):
    m_ref[...] = jnp.full(m_ref.shape, -jnp.inf, F32)
    acc_ref[...] = jnp.zeros(acc_ref.shape, F32)


def _gqa_kernel(q_ref, k_ref, v_ref, o_ref, s_buf, p_buf, a_buf, m_ref, acc_ref, *, tq, tk, n_sub, n_steps):
    step = pl.program_id(2)

    @pl.when(step == 0)
    def _():
        _init_softmax_state(m_ref, acc_ref)

    def keys(u):
        return u & (A_KV_HEADS - 1), pl.ds(pl.multiple_of((u >> 1) * tk, tk), tk)

    def scores(u, slot):
        g, rows = keys(u)
        q = q_ref[pl.ds(g * A_GROUP, A_GROUP)].reshape(A_GROUP * tq, HEAD_DIM)
        s_buf[slot] = lax.dot_general(q, k_ref[g, rows, :], NT_DIMS, preferred_element_type=F32)

    def softmax(u, slot):
        p, alpha = _softmax_update(lambda: s_buf[slot], m_ref, u & (A_KV_HEADS - 1))
        p_buf[slot] = p
        a_buf[slot] = alpha

    def values(u, slot):
        g, rows = keys(u)
        pv = jnp.dot(p_buf[slot], v_ref[g, rows, :], preferred_element_type=F32)
        acc_ref[g] = a_buf[slot] * acc_ref[g] + pv

    _run_units(n_sub * A_KV_HEADS, scores, softmax, values)

    @pl.when(step == n_steps - 1)
    def _():
        for h in range(A_HEADS):
            g, r = divmod(h, A_GROUP)
            acc = acc_ref[g, r * tq:(r + 1) * tq, :]
            y = acc[:, 0:HEAD_DIM] / acc[:, HEAD_DIM:HEAD_DIM + 1]
            o_ref[:, h * HEAD_DIM:(h + 1) * HEAD_DIM] = y.astype(BF16)


def _gqa_attention(qa, ka, va):
    _, b, seq, _ = qa.shape
    tq = min(256, seq)
    tk = min(1024, seq)
    span = min(4096, seq)
    n_steps = seq // span
    rows = A_GROUP * tq
    return pl.pallas_call(
        functools.partial(_gqa_kernel, tq=tq, tk=tk, n_sub=span // tk, n_steps=n_steps),
        grid=(b, seq // tq, n_steps),
        in_specs=[pl.BlockSpec((A_HEADS, None, tq, HEAD_DIM), lambda bi, qi, si: (0, bi, qi, 0)),
                  pl.BlockSpec((A_KV_HEADS, None, span, HEAD_DIM), lambda bi, qi, si: (0, bi, si, 0)),
                  pl.BlockSpec((A_KV_HEADS, None, span, LANES), lambda bi, qi, si: (0, bi, si, 0))],
        out_specs=pl.BlockSpec((None, tq, A_WIDTH), lambda bi, qi, si: (bi, qi, 0)),
        out_shape=jax.ShapeDtypeStruct((b, seq, A_WIDTH), BF16),
        scratch_shapes=[pltpu.VMEM((PIPE_SLOTS, rows, tk), F32), pltpu.VMEM((PIPE_SLOTS, rows, tk), BF16),
                        pltpu.VMEM((PIPE_SLOTS, rows, LANES), F32),
                        pltpu.VMEM((A_KV_HEADS, rows, LANES), F32), pltpu.VMEM((A_KV_HEADS, rows, LANES), F32)],
        compiler_params=_params("parallel", "parallel", "arbitrary"),
        name="gqa_attn",
    )(qa, ka, va)


def _diff_kernel(slope_ref, q_ref, k_ref, v_ref, lq1_ref, lk1_ref, lq2_ref, lk2_ref, g_ref, o_ref,
                 s_buf, p_buf, a_buf, d_ref, m_ref, acc_ref, *, tq, tk, span, n_steps, lam_init):
    qi = pl.program_id(1)
    step = pl.program_id(2)

    @pl.when(step == 0)
    def _():
        _init_softmax_state(m_ref, acc_ref)
        d_ref[...] = (lax.broadcasted_iota(I32, (tq, tk), 0) - lax.broadcasted_iota(I32, (tq, tk), 1)).astype(F32)

    def keys(u):
        j = u >> 2
        return u & (B_HEADS - 1), j, pl.ds(pl.multiple_of(j * tk, tk), tk)

    def scores(u, slot):
        h, j, rows = keys(u)
        offset = (qi * tq - step * span - j * tk).astype(F32)
        bias = jnp.abs(d_ref[...] + offset) * slope_ref[h]
        for m in range(2):
            i = m * B_HEADS + h
            s = lax.dot_general(q_ref[i], k_ref[i, rows, :], NT_DIMS, preferred_element_type=F32)
            s_buf[slot, m] = s - bias

    def softmax(u, slot):
        h = u & (B_HEADS - 1)
        for m in range(2):
            p, alpha = _softmax_update(lambda: s_buf[slot, m], m_ref, m * B_HEADS + h)
            p_buf[slot, m] = p
            a_buf[slot, m] = alpha

    def values(u, slot):
        h, _, rows = keys(u)
        v = v_ref[h, rows, :]
        for m in range(2):
            i = m * B_HEADS + h
            pv = jnp.dot(p_buf[slot, m], v, preferred_element_type=F32)
            acc_ref[i] = jnp.tile(a_buf[slot, m], (1, 2)) * acc_ref[i] + pv

    _run_units((span // tk) * B_HEADS, scores, softmax, values)

    @pl.when(step == n_steps - 1)
    def _():
        lam = (jnp.exp(jnp.sum(lq1_ref[...] * lk1_ref[...], axis=1, keepdims=True))
               - jnp.exp(jnp.sum(lq2_ref[...] * lk2_ref[...], axis=1, keepdims=True)) + lam_init)

        def normalized(i):
            acc = acc_ref[i]
            return acc[:, 0:B_VDIM] / acc[:, B_VDIM:B_VDIM + 1]

        for h in range(B_HEADS):
            y = normalized(h) - lam * normalized(B_HEADS + h)
            ms = jnp.mean(y * y, axis=1, keepdims=True)
            y = y * lax.rsqrt(ms + RMS_EPS) * g_ref[...] * (1.0 - lam_init)
            o_ref[:, h * B_VDIM:(h + 1) * B_VDIM] = y.astype(BF16)


def _diff_attention(qb, kb, vb, lq1, lk1, lq2, lk2, subln_g, lam_init):
    _, b, seq, _ = qb.shape
    tq = min(512, seq)
    tk = min(1024, seq)
    span = min(2048, seq)
    n_steps = seq // span
    nh = 2 * B_HEADS
    slopes = jnp.asarray([2.0 ** (-8.0 * (h + 1) / B_HEADS) * LOG2E for h in range(B_HEADS)], F32)
    vec = pl.BlockSpec((1, HEAD_DIM), lambda bi, qi, si, *_: (0, 0))
    return pl.pallas_call(
        functools.partial(_diff_kernel, tq=tq, tk=tk, span=span, n_steps=n_steps, lam_init=lam_init),
        grid_spec=pltpu.PrefetchScalarGridSpec(
            num_scalar_prefetch=1,
            grid=(b, seq // tq, n_steps),
            in_specs=[pl.BlockSpec((nh, None, tq, HEAD_DIM), lambda bi, qi, si, *_: (0, bi, qi, 0)),
                      pl.BlockSpec((nh, None, span, HEAD_DIM), lambda bi, qi, si, *_: (0, bi, si, 0)),
                      pl.BlockSpec((B_HEADS, None, span, 2 * B_VDIM), lambda bi, qi, si, *_: (0, bi, si, 0)),
                      vec, vec, vec, vec, pl.BlockSpec((1, B_VDIM), lambda bi, qi, si, *_: (0, 0))],
            out_specs=pl.BlockSpec((None, tq, B_WIDTH), lambda bi, qi, si, *_: (bi, qi, 0)),
            scratch_shapes=[pltpu.VMEM((PIPE_SLOTS, 2, tq, tk), F32), pltpu.VMEM((PIPE_SLOTS, 2, tq, tk), BF16),
                            pltpu.VMEM((PIPE_SLOTS, 2, tq, LANES), F32), pltpu.VMEM((tq, tk), F32),
                            pltpu.VMEM((nh, tq, LANES), F32), pltpu.VMEM((nh, tq, 2 * B_VDIM), F32)],
        ),
        out_shape=jax.ShapeDtypeStruct((b, seq, B_WIDTH), BF16),
        compiler_params=_params("parallel", "parallel", "arbitrary"),
        name="diff_attn",
    )(slopes, qb, kb, vb, lq1, lk1, lq2, lk2, subln_g)


def _merge_kernel(ya_ref, yb_ref, gate_ref, h_ref, wa_ref, wb_ref, wo_ref, g_ref, b_ref,
                  wrh_ref, wrl_ref, x1_ref, aff_ref):
    ma = jnp.dot(ya_ref[...], wa_ref[...], preferred_element_type=F32)
    mb = jnp.dot(yb_ref[...], wb_ref[...], preferred_element_type=F32)
    merged = _sigmoid(gate_ref[:, 0:D_MODEL]) * ma + _sigmoid(gate_ref[:, D_MODEL:2 * D_MODEL]) * mb
    out = jnp.dot(merged.astype(BF16), wo_ref[...], preferred_element_type=F32)
    x1 = _layer_norm(ALPHA * h_ref[...] + out, g_ref[...], b_ref[...])
    x1_ref[...] = x1
    xh, xl = _split_bf16(x1)
    wh = wrh_ref[...]
    logits = (lax.dot_general(wh, xh, NT_DIMS, preferred_element_type=F32)
              + lax.dot_general(wh, xl, NT_DIMS, preferred_element_type=F32)
              + lax.dot_general(wrl_ref[...], xh, NT_DIMS, preferred_element_type=F32))
    e = jnp.exp(logits - jnp.max(logits, axis=0, keepdims=True))
    aff_ref[...] = e / jnp.sum(e, axis=0, keepdims=True)


def _merge(ya, yb, gates, h, wa, wb, wo, g, b, wrh, wrl):
    n = h.shape[0]
    tm = 256

    def rows(w):
        return pl.BlockSpec((tm, w), lambda i: (i, 0))

    return pl.pallas_call(
        _merge_kernel,
        grid=(n // tm,),
        in_specs=[rows(A_WIDTH), rows(B_WIDTH), rows(2 * D_MODEL), rows(D_MODEL),
                  _full((A_WIDTH, D_MODEL)), _full((B_WIDTH, D_MODEL)), _full((D_MODEL, D_MODEL)),
                  _full((1, D_MODEL)), _full((1, D_MODEL)),
                  _full((N_EXPERTS, D_MODEL)), _full((N_EXPERTS, D_MODEL))],
        out_specs=[rows(D_MODEL), pl.BlockSpec((N_EXPERTS, tm), lambda i: (0, i))],
        out_shape=[jax.ShapeDtypeStruct((n, D_MODEL), F32), jax.ShapeDtypeStruct((N_EXPERTS, n), F32)],
        compiler_params=_params("parallel"),
        name="merge",
    )(ya, yb, gates, h, wa, wb, wo, g, b, wrh, wrl)


def _route_kernel(aff_ref, rank_ref, idx_ref, blkp_ref, sel_scr, wi_scr, p_scr, *, nb, cap):
    ne = N_EXPERTS
    shape3 = (ne, nb, LANES)
    bits = pltpu.bitcast(aff_ref[...], I32).reshape(shape3)
    capf = float(cap)

    def count(mask3):
        part = jnp.sum(jnp.where(mask3, 1.0, 0.0), axis=1, keepdims=True)
        return jnp.sum(part, axis=2, keepdims=True)

    def search(i, thr):
        cand = thr | jnp.left_shift(jnp.int32(1), 30 - i)
        return jnp.where(count(bits >= cand) >= capf, cand, thr)

    thr = lax.fori_loop(0, 31, search, jnp.zeros((ne, 1, 1), I32))
    gt = bits > thr
    eq = bits == thr
    need = capf - count(gt)

    li = lax.broadcasted_iota(I32, (LANES, LANES), 0)
    lj = lax.broadcasted_iota(I32, (LANES, LANES), 1)
    upper_incl = jnp.where(li <= lj, 1.0, 0.0).astype(BF16)
    lower_incl = jnp.where(lj <= li, 1.0, 0.0).astype(BF16)
    bi = lax.broadcasted_iota(I32, (nb, nb), 0)
    bj = lax.broadcasted_iota(I32, (nb, nb), 1)
    lower_strict = jnp.where(bj < bi, 1.0, 0.0).astype(BF16)
    upper_strict = jnp.where(bi < bj, 1.0, 0.0).astype(BF16)

    def prefix(mask2):
        wi = jnp.dot(mask2.astype(BF16), upper_incl, preferred_element_type=F32)
        tot = jnp.broadcast_to(wi[:, LANES - 1:LANES], wi.shape).astype(BF16)
        blk = [jnp.dot(lower_strict, tot[e * nb:(e + 1) * nb], preferred_element_type=F32)
               for e in range(ne)]
        return wi, jnp.concatenate(blk, axis=0)

    eqf = jnp.where(eq, 1.0, 0.0).reshape(ne * nb, LANES)
    wi, bp = prefix(eqf)
    eq_rank = (wi + bp - eqf).reshape(shape3)
    sel = gt | (eq & (eq_rank < need))
    self_ = jnp.where(sel, 1.0, 0.0).reshape(ne * nb, LANES)
    wi, bp = prefix(self_)
    rank = wi + bp - self_
    rank_ref[...] = jnp.where(self_ > 0.0, rank, -1.0).astype(I32)
    sel_scr[...] = self_
    wi_scr[...] = wi
    p_scr[...] = bp

    chunk = min(cap, 1024)
    ones8 = jnp.ones((8, LANES), BF16)
    b_col = lax.broadcasted_iota(I32, (nb, 1), 0).astype(F32)

    def per_expert(e, carry):
        r0 = pl.multiple_of(e * nb, 8)
        sel_e = sel_scr[pl.ds(r0, nb), :].astype(BF16)
        wi_e = wi_scr[pl.ds(r0, nb), :]
        p_e = p_scr[pl.ds(r0, nb), :][:, 0:1]
        p_end = p_e + wi_e[:, LANES - 1:LANES]
        wi_t = lax.dot_general(lower_incl, sel_e, NT_DIMS, preferred_element_type=F32).astype(BF16)
        tot_row = lax.dot_general(ones8, sel_e, NT_DIMS, preferred_element_type=F32)
        p_row = jnp.dot(tot_row.astype(BF16), upper_strict, preferred_element_type=F32)
        blkp_ref[e] = p_row[0:1, :].astype(I32)
        for c in range(cap // chunk):
            slot = (lax.broadcasted_iota(I32, (1, chunk), 1) + c * chunk).astype(F32)
            hot = (p_e <= slot) & (slot < p_end)
            w_slot = jnp.dot(wi_t, jnp.where(hot, 1.0, 0.0).astype(BF16), preferred_element_type=F32)
            p_slot = jnp.sum(jnp.where(hot, p_e, 0.0), axis=0, keepdims=True)
            b_slot = jnp.sum(jnp.where(hot, b_col, 0.0), axis=0, keepdims=True)
            lane = jnp.sum(jnp.where(w_slot <= slot - p_slot, 1.0, 0.0), axis=0, keepdims=True)
            idx_ref[e, :, c * chunk:(c + 1) * chunk] = (b_slot * LANES + lane).astype(I32)
        return carry

    lax.fori_loop(0, ne, per_expert, 0)


def _route(aff2, n, cap):
    nb = n // LANES
    return pl.pallas_call(
        functools.partial(_route_kernel, nb=nb, cap=cap),
        out_shape=[jax.ShapeDtypeStruct((N_EXPERTS * nb, LANES), I32),
                   jax.ShapeDtypeStruct((N_EXPERTS, 1, cap), I32),
                   jax.ShapeDtypeStruct((N_EXPERTS, 1, nb), I32)],
        scratch_shapes=[pltpu.VMEM((N_EXPERTS * nb, LANES), F32)] * 3,
        compiler_params=pltpu.CompilerParams(vmem_limit_bytes=VMEM_LIMIT),
        name="route",
    )(aff2)


def _gather_kernel(idx_ref, x_hbm, o_ref, sem, *, rows):
    base = pl.program_id(0) * rows

    def start(r, c):
        pltpu.make_async_copy(x_hbm.at[pl.ds(idx_ref[base + r], 1), :], o_ref.at[pl.ds(r, 1), :], sem.at[0]).start()
        return c

    lax.fori_loop(0, rows, start, 0, unroll=8)
    pltpu.make_async_copy(x_hbm.at[pl.ds(0, rows), :], o_ref, sem.at[0]).wait()


def _gather(idx_flat, x1):
    rows = 512
    total = idx_flat.shape[0]
    return pl.pallas_call(
        functools.partial(_gather_kernel, rows=rows),
        grid_spec=pltpu.PrefetchScalarGridSpec(
            num_scalar_prefetch=1,
            grid=(total // rows,),
            in_specs=[pl.BlockSpec(memory_space=pl.ANY)],
            out_specs=pl.BlockSpec((rows, D_MODEL), lambda i, idx: (i, 0)),
            scratch_shapes=[pltpu.SemaphoreType.DMA((1,))],
        ),
        out_shape=jax.ShapeDtypeStruct((total, D_MODEL), F32),
        compiler_params=_params("arbitrary"),
        name="gather",
    )(idx_flat, x1)


def _ffn_kernel(x_ref, wg_ref, wu_ref, wd_ref, y_ref):
    x = x_ref[...].astype(BF16)
    fc = 512
    acc = None
    for c in range(EXPERT_FF // fc):
        sl = slice(c * fc, (c + 1) * fc)
        a = jnp.dot(x, wg_ref[:, sl], preferred_element_type=F32)
        u = jnp.dot(x, wu_ref[:, sl], preferred_element_type=F32)
        mid = (a * _sigmoid(a) * u).astype(BF16)
        part = jnp.dot(mid, wd_ref[sl, :], preferred_element_type=F32)
        acc = part if acc is None else acc + part
    y_ref[...] = acc.astype(BF16)


def _ffn(xe, wg, wu, wd, cap):
    tm = min(512, cap)
    tiles = cap // tm
    return pl.pallas_call(
        _ffn_kernel,
        grid=(N_EXPERTS, tiles),
        in_specs=[pl.BlockSpec((tm, D_MODEL), lambda e, i: (e * tiles + i, 0)),
                  pl.BlockSpec((None, D_MODEL, EXPERT_FF), lambda e, i: (e, 0, 0)),
                  pl.BlockSpec((None, D_MODEL, EXPERT_FF), lambda e, i: (e, 0, 0)),
                  pl.BlockSpec((None, EXPERT_FF, D_MODEL), lambda e, i: (e, 0, 0))],
        out_specs=pl.BlockSpec((tm, D_MODEL), lambda e, i: (e * tiles + i, 0)),
        out_shape=jax.ShapeDtypeStruct((N_EXPERTS * cap, D_MODEL), BF16),
        compiler_params=_params("parallel", "parallel"),
        name="ffn",
    )(xe, wg, wu, wd)


TOK_TILE = 128
ROW_ALIGN = 16
WIN = TOK_TILE + ROW_ALIGN


def _final_kernel(start_ref, x1_ref, p_ref, rank_ref, aff_ref, wpg_ref, wpp_ref, g_ref, b_ref,
                  ye_hbm, o_ref, buf, sem, *, cap, n_steps):
    t = pl.program_id(0)
    slot = t & 1

    def window(step, e):
        first = start_ref[step * N_EXPERTS + e]
        ws = jnp.minimum((first // ROW_ALIGN) * ROW_ALIGN, cap - WIN)
        return pl.multiple_of(ws, ROW_ALIGN)

    def fetch(step, dst_slot):
        for e in range(N_EXPERTS):
            pltpu.make_async_copy(ye_hbm.at[pl.ds(e * cap + window(step, e), WIN), :],
                                  buf.at[dst_slot, pl.ds(e * WIN, WIN), :], sem.at[dst_slot]).start()

    @pl.when(t == 0)
    def _():
        fetch(0, 0)

    @pl.when(t + 1 < n_steps)
    def _():
        fetch(t + 1, 1 - slot)

    x1 = x1_ref[...]
    gate = _sigmoid(jnp.dot(x1.astype(BF16), wpg_ref[...], preferred_element_type=F32))
    ple = gate * jnp.dot(p_ref[...].astype(BF16), wpp_ref[...], preferred_element_type=F32)
    total = ALPHA * x1 + ple

    pltpu.make_async_copy(ye_hbm.at[pl.ds(0, N_EXPERTS * WIN), :], buf.at[slot], sem.at[slot]).wait()
    lane = lax.broadcasted_iota(I32, (1, WIN), 1)
    for e in range(N_EXPERTS):
        hot = rank_ref[:, e:e + 1] == lane + window(t, e)
        y = jnp.dot(jnp.where(hot, 1.0, 0.0).astype(BF16), buf[slot, e * WIN:(e + 1) * WIN, :],
                    preferred_element_type=F32)
        total = total + aff_ref[:, e:e + 1] * y

    o_ref[...] = _layer_norm(total, g_ref[...], b_ref[...])


def _final(starts, x1, p2, rank_t, aff_t, wpg, wpp, g, b, ye, cap):
    n = x1.shape[0]
    tm = TOK_TILE

    def rows(w):
        return pl.BlockSpec((tm, w), lambda i, *_: (i, 0))

    def full(shape):
        return pl.BlockSpec(shape, lambda i, *_: (0,) * len(shape))

    return pl.pallas_call(
        functools.partial(_final_kernel, cap=cap, n_steps=n // tm),
        grid_spec=pltpu.PrefetchScalarGridSpec(
            num_scalar_prefetch=1,
            grid=(n // tm,),
            in_specs=[rows(D_MODEL), rows(PLE_DIM), rows(N_EXPERTS), rows(N_EXPERTS),
                      full((D_MODEL, D_MODEL)), full((PLE_DIM, D_MODEL)), full((1, D_MODEL)), full((1, D_MODEL)),
                      pl.BlockSpec(memory_space=pl.ANY)],
            out_specs=rows(D_MODEL),
            scratch_shapes=[pltpu.VMEM((2, N_EXPERTS * WIN, D_MODEL), BF16), pltpu.SemaphoreType.DMA((2,))],
        ),
        out_shape=jax.ShapeDtypeStruct((n, D_MODEL), F32),
        compiler_params=_params("arbitrary"),
        name="final",
    )(starts, x1, p2, rank_t, aff_t, wpg, wpp, g, b, ye)


def _rope_tables(seq):
    n_rows = seq // GRID_W
    rows = jnp.repeat(jnp.arange(n_rows, dtype=F32), GRID_W)
    cols = jnp.tile(jnp.arange(GRID_W, dtype=F32), n_rows)
    sec = HEAD_DIM // 2
    inv_freq = ROPE_THETA ** (-jnp.arange(0, sec, 2, dtype=F32) / sec)
    ang_r = rows[:, None] * inv_freq[None, :]
    ang_c = cols[:, None] * inv_freq[None, :]
    cos = jnp.concatenate([jnp.cos(ang_r)] * 2 + [jnp.cos(ang_c)] * 2, axis=-1)
    sin = jnp.concatenate([-jnp.sin(ang_r), jnp.sin(ang_r), -jnp.sin(ang_c), jnp.sin(ang_c)], axis=-1)
    return jnp.tile(cos, (1, 2)), jnp.tile(sin, (1, 2))


def _trunk(x, p, w):
    b, seq, _ = x.shape
    n = b * seq
    cap = max(1, CAPACITY_FACTOR * n // N_EXPERTS)
    assert seq % 256 == 0 and n % (8 * LANES) == 0 and cap % TOK_TILE == 0 and cap >= WIN
    cos_t, sin_t = _rope_tables(seq)

    h, qa, ka, va, qb, kb, vb, gates = _in_proj(
        x.reshape(n, D_MODEL), w["ln_emb_g"], w["ln_emb_b"], w["w_in"], w["q_norm_g"], w["k_norm_g"],
        cos_t, sin_t, w["seg"], seq)

    def per_batch(t):
        return t.reshape(t.shape[0], b, seq, t.shape[2])

    ya = _gqa_attention(per_batch(qa), per_batch(ka), per_batch(va))
    lam_init = 0.8 - 0.6 * math.exp(-0.3 * 0)
    yb = _diff_attention(per_batch(qb), per_batch(kb), per_batch(vb),
                         w["lambda_q1"], w["lambda_k1"], w["lambda_q2"], w["lambda_k2"], w["subln_g"], lam_init)

    x1, aff = _merge(ya.reshape(n, -1), yb.reshape(n, -1), gates, h, w["w_branch_a"], w["w_branch_b"],
                     w["w_out"], w["ln1_g"], w["ln1_b"], w["wr_hi"], w["wr_lo"])

    nb = n // LANES
    rank2, idx, blkp = _route(aff.reshape(N_EXPERTS * nb, LANES), n, cap)
    xe = _gather(idx.reshape(N_EXPERTS * cap), x1)
    ye = _ffn(xe, w["w_gate"], w["w_up"], w["w_down"], cap)

    starts = blkp.reshape(N_EXPERTS, nb)[:, ::TOK_TILE // LANES]
    out = _final(starts.T.reshape(-1), x1, p.reshape(n, PLE_DIM),
                 rank2.reshape(N_EXPERTS, n).T, aff.T, w["w_ple_gate"], w["w_ple_proj"],
                 w["ln2_g"], w["ln2_b"], ye, cap)
    return out.reshape(b, seq, D_MODEL)


def _prep_weights(ln_emb_g, ln_emb_b, w_in, q_norm_g, k_norm_g, lambda_q1, lambda_k1, lambda_q2, lambda_k2,
                  subln_g, w_branch_a, w_branch_b, w_out, ln1_g, ln1_b, w_router, w_gate, w_up, w_down,
                  w_ple_gate, w_ple_proj, ln2_g, ln2_b):
    i = 0
    seg_i = lax.broadcasted_iota(I32, (A_WIDTH, A_WIDTH), 0) // HEAD_DIM
    seg_j = lax.broadcasted_iota(I32, (A_WIDTH, A_WIDTH), 1) // HEAD_DIM
    wr_t = w_router[i].T
    wr_hi = wr_t.astype(BF16)
    return {
        "ln_emb_g": ln_emb_g.reshape(1, D_MODEL), "ln_emb_b": ln_emb_b.reshape(1, D_MODEL),
        "w_in": w_in[i].astype(BF16),
        "q_norm_g": jnp.tile(q_norm_g[i], 2).reshape(1, LANES), "k_norm_g": jnp.tile(k_norm_g[i], 2).reshape(1, LANES),
        "seg": (seg_i == seg_j).astype(BF16),
        "lambda_q1": lambda_q1[i].reshape(1, HEAD_DIM), "lambda_k1": lambda_k1[i].reshape(1, HEAD_DIM),
        "lambda_q2": lambda_q2[i].reshape(1, HEAD_DIM), "lambda_k2": lambda_k2[i].reshape(1, HEAD_DIM),
        "subln_g": subln_g[i].reshape(1, B_VDIM),
        "w_branch_a": w_branch_a[i].astype(BF16), "w_branch_b": w_branch_b[i].astype(BF16),
        "w_out": w_out[i].astype(BF16),
        "ln1_g": ln1_g[i].reshape(1, D_MODEL), "ln1_b": ln1_b[i].reshape(1, D_MODEL),
        "wr_hi": wr_hi, "wr_lo": (wr_t - wr_hi.astype(F32)).astype(BF16),
        "w_gate": w_gate[i].astype(BF16), "w_up": w_up[i].astype(BF16), "w_down": w_down[i].astype(BF16),
        "w_ple_gate": w_ple_gate[i].astype(BF16), "w_ple_proj": w_ple_proj[i].astype(BF16),
        "ln2_g": ln2_g[i].reshape(1, D_MODEL), "ln2_b": ln2_b[i].reshape(1, D_MODEL),
    }


def kernel(x_prompt, x_sample, p_prompt, p_sample, ln_emb_g, ln_emb_b, w_in, q_norm_g, k_norm_g, lambda_q1, lambda_k1, lambda_q2, lambda_k2, subln_g, w_branch_a, w_branch_b, w_out, ln1_g, ln1_b, w_router, w_gate, w_up, w_down, w_ple_gate, w_ple_proj, ln2_g, ln2_b):
    w = _prep_weights(ln_emb_g, ln_emb_b, w_in, q_norm_g, k_norm_g, lambda_q1, lambda_k1, lambda_q2, lambda_k2,
                      subln_g, w_branch_a, w_branch_b, w_out, ln1_g, ln1_b, w_router, w_gate, w_up, w_down,
                      w_ple_gate, w_ple_proj, ln2_g, ln2_b)
    return (_trunk(x_prompt, p_prompt[0], w), _trunk(x_sample, p_sample[0], w))
```

```python
import functools
import math

import jax
import jax.numpy as jnp
from jax import lax
from jax.experimental import pallas as pl
from jax.experimental.pallas import tpu as pltpu

F32 = jnp.float32
BF16 = jnp.bfloat16
I32 = jnp.int32

D_MODEL = 1024
DEPTH = 1
HEAD_DIM = 64
A_HEADS = 8
A_KV_HEADS = 2
A_GROUP = A_HEADS // A_KV_HEADS
A_WIDTH = A_HEADS * HEAD_DIM
B_HEADS = 4
B_VDIM = 2 * HEAD_DIM
B_WIDTH = B_HEADS * B_VDIM
GRID_W = 64
ROPE_THETA = 10000.0
N_EXPERTS = 16
EXPERT_FF = 2048
CAPACITY_FACTOR = 2
PLE_DIM = 256
LN_EPS = 1e-5
RMS_EPS = 1e-6
ALPHA = (2 * DEPTH) ** 0.25
LOG2E = math.log2(math.e)
QSCALE = HEAD_DIM ** -0.5 * LOG2E
LANES = 128
ROPE_HALF = HEAD_DIM // 4

C_QA = 0
C_KA = C_QA + A_WIDTH
C_VA = C_KA + A_KV_HEADS * HEAD_DIM
C_QB = C_VA + A_KV_HEADS * HEAD_DIM
C_KB = C_QB + 2 * B_HEADS * HEAD_DIM
C_VB = C_KB + 2 * B_HEADS * HEAD_DIM
C_G = C_VB + B_WIDTH
IN_COLS = C_G + 2 * D_MODEL

VMEM_LIMIT = 56 * 1024 * 1024
ROW_TILE = 512
NT_DIMS = (((1,), (1,)), ((), ()))


def _params(*sem):
    return pltpu.CompilerParams(dimension_semantics=sem, vmem_limit_bytes=VMEM_LIMIT)


def _full(shape):
    n = len(shape)
    return pl.BlockSpec(shape, lambda *_: (0,) * n, pipeline_mode=pl.Buffered(1))


def _layer_norm(x, g, b):
    mu = jnp.mean(x, axis=-1, keepdims=True)
    xc = x - mu
    var = jnp.mean(xc * xc, axis=-1, keepdims=True)
    return xc * lax.rsqrt(var + LN_EPS) * g + b


def _sigmoid(x):
    return 1.0 / (1.0 + jnp.exp(-x))


def _split_bf16(x):
    hi = x.astype(BF16)
    lo = (x - hi.astype(F32)).astype(BF16)
    return hi, lo


def _head_mean_sq(x, seg):
    hi, lo = _split_bf16(x * x)
    s = jnp.dot(hi, seg, preferred_element_type=F32) + jnp.dot(lo, seg, preferred_element_type=F32)
    return s * (1.0 / HEAD_DIM)


def _rope(x, cos, sin, first_half):
    fwd = pltpu.roll(x, LANES - ROPE_HALF, 1)
    bwd = pltpu.roll(x, ROPE_HALF, 1)
    return x * cos + jnp.where(first_half, fwd, bwd) * sin


def _in_proj_kernel(x_ref, lng_ref, lnb_ref, w_ref, qg_ref, kg_ref, cos_ref, sin_ref, seg_ref,
                    h_ref, qa_ref, ka_ref, va_ref, qb_ref, kb_ref, vb_ref, gate_ref):
    xn = _layer_norm(x_ref[...], lng_ref[...], lnb_ref[...])
    h_ref[...] = xn
    xb = xn.astype(BF16)

    def proj(lo, hi):
        return jnp.dot(xb, w_ref[:, lo:hi], preferred_element_type=F32)

    def store_heads(ref, x):
        for i in range(x.shape[1] // HEAD_DIM):
            ref[i] = x[:, i * HEAD_DIM:(i + 1) * HEAD_DIM].astype(BF16)

    cos = cos_ref[...]
    sin = sin_ref[...]
    lane = lax.broadcasted_iota(I32, cos.shape, 1)
    first_half = (lane & ROPE_HALF) == 0

    qa = proj(C_QA, C_KA)
    qn = qa * lax.rsqrt(_head_mean_sq(qa, seg_ref[...]) + RMS_EPS)
    for c in range(A_WIDTH // LANES):
        pair = _rope(qn[:, c * LANES:(c + 1) * LANES] * qg_ref[...], cos, sin, first_half) * QSCALE
        qa_ref[2 * c] = pair[:, 0:HEAD_DIM].astype(BF16)
        qa_ref[2 * c + 1] = pair[:, HEAD_DIM:LANES].astype(BF16)

    ka = proj(C_KA, C_VA)
    kn = ka * lax.rsqrt(_head_mean_sq(ka, seg_ref[0:LANES, 0:LANES]) + RMS_EPS)
    store_heads(ka_ref, _rope(kn * kg_ref[...], cos, sin, first_half))
    one_at_0 = jnp.where(lane == 0, 1.0, 0.0)
    va = proj(C_VA, C_QB)
    for g in range(A_KV_HEADS):
        v_first = va if g == 0 else pltpu.roll(va, HEAD_DIM, 1)
        va_ref[g] = jnp.where(lane < HEAD_DIM, v_first, jnp.where(lane == HEAD_DIM, 1.0, 0.0)).astype(BF16)
    store_heads(qb_ref, proj(C_QB, C_KB) * QSCALE)
    store_heads(kb_ref, proj(C_KB, C_VB))
    vb = proj(C_VB, C_G)
    for i in range(B_HEADS):
        vb_ref[i, :, 0:B_VDIM] = vb[:, i * B_VDIM:(i + 1) * B_VDIM].astype(BF16)
        vb_ref[i, :, B_VDIM:2 * B_VDIM] = one_at_0.astype(BF16)
    gate_ref[...] = proj(C_G, IN_COLS)


def _in_proj(x2, lng, lnb, w_in, qg, kg, cos_t, sin_t, seg, seq):
    n = x2.shape[0]
    tm = ROW_TILE
    tiles_per_seq = seq // tm

    def rows(w):
        return pl.BlockSpec((tm, w), lambda i: (i, 0))

    def heads(nh, w):
        return pl.BlockSpec((nh, tm, w), lambda i: (0, i, 0))

    tab = pl.BlockSpec((tm, LANES), lambda i: (i % tiles_per_seq, 0))
    head_outs = [(A_HEADS, HEAD_DIM), (A_KV_HEADS, HEAD_DIM), (A_KV_HEADS, LANES),
                 (2 * B_HEADS, HEAD_DIM), (2 * B_HEADS, HEAD_DIM), (B_HEADS, 2 * B_VDIM)]
    return pl.pallas_call(
        _in_proj_kernel,
        grid=(n // tm,),
        in_specs=[rows(D_MODEL), _full((1, D_MODEL)), _full((1, D_MODEL)), _full((D_MODEL, IN_COLS)),
                  _full((1, LANES)), _full((1, LANES)), tab, tab, _full((A_WIDTH, A_WIDTH))],
        out_specs=[rows(D_MODEL)] + [heads(nh, w) for nh, w in head_outs] + [rows(2 * D_MODEL)],
        out_shape=([jax.ShapeDtypeStruct((n, D_MODEL), F32)]
                   + [jax.ShapeDtypeStruct((nh, n, w), BF16) for nh, w in head_outs]
                   + [jax.ShapeDtypeStruct((n, 2 * D_MODEL), F32)]),
        compiler_params=_params("parallel"),
        name="in_proj",
    )(x2, lng, lnb, w_in, qg, kg, cos_t, sin_t, seg)


PIPE_SLOTS = 2


def _run_units(n_units, scores, softmax, values):
    assert n_units % PIPE_SLOTS == 0 and n_units >= 2 * PIPE_SLOTS
    scores(0, 0)
    scores(1, 1)
    softmax(0, 0)

    def two_units(i, carry):
        t = PIPE_SLOTS * (i + 1)
        scores(t, 0)
        softmax(t - 1, 1)
        values(t - 2, 0)
        scores(t + 1, 1)
        softmax(t, 0)
        values(t - 1, 1)
        return carry

    lax.fori_loop(0, n_units // PIPE_SLOTS - 1, two_units, 0)
    softmax(n_units - 1, 1)
    values(n_units - 2, 0)
    values(n_units - 1, 1)


def _softmax_update(load_s, m_ref, i):
    m_prev = m_ref[i]
    m_new = jnp.maximum(m_prev, jnp.max(load_s(), axis=1, keepdims=True))
    alpha = jnp.exp2(m_prev - m_new)
    s = load_s()
    p = jnp.exp2(s - jnp.tile(m_new, (1, s.shape[1] // LANES)))
    m_ref[i] = m_new
    return p.astype(BF16), alpha


def _init_softmax_state(m_ref, acc_ref):
    m_ref[...] = jnp.full(m_ref.shape, -jnp.inf, F32)
    acc_ref[...] = jnp.zeros(acc_ref.shape, F32)


def _gqa_kernel(q_ref, k_ref, v_ref, o_ref, s_buf, p_buf, a_buf, m_ref, acc_ref, *, tq, tk, n_sub, n_steps):
    step = pl.program_id(2)

    @pl.when(step == 0)
    def _():
        _init_softmax_state(m_ref, acc_ref)

    def keys(u):
        return u & (A_KV_HEADS - 1), pl.ds(pl.multiple_of((u >> 1) * tk, tk), tk)

    def scores(u, slot):
        g, rows = keys(u)
        q = q_ref[pl.ds(g * A_GROUP, A_GROUP)].reshape(A_GROUP * tq, HEAD_DIM)
        s_buf[slot] = lax.dot_general(q, k_ref[g, rows, :], NT_DIMS, preferred_element_type=F32)

    def softmax(u, slot):
        p, alpha = _softmax_update(lambda: s_buf[slot], m_ref, u & (A_KV_HEADS - 1))
        p_buf[slot] = p
        a_buf[slot] = alpha

    def values(u, slot):
        g, rows = keys(u)
        pv = jnp.dot(p_buf[slot], v_ref[g, rows, :], preferred_element_type=F32)
        acc_ref[g] = a_buf[slot] * acc_ref[g] + pv

    _run_units(n_sub * A_KV_HEADS, scores, softmax, values)

    @pl.when(step == n_steps - 1)
    def _():
        for h in range(A_HEADS):
            g, r = divmod(h, A_GROUP)
            acc = acc_ref[g, r * tq:(r + 1) * tq, :]
            y = acc[:, 0:HEAD_DIM] / acc[:, HEAD_DIM:HEAD_DIM + 1]
            o_ref[:, h * HEAD_DIM:(h + 1) * HEAD_DIM] = y.astype(BF16)


def _gqa_attention(qa, ka, va):
    _, b, seq, _ = qa.shape
    tq = min(256, seq)
    tk = min(1024, seq)
    span = min(4096, seq)
    n_steps = seq // span
    rows = A_GROUP * tq
    return pl.pallas_call(
        functools.partial(_gqa_kernel, tq=tq, tk=tk, n_sub=span // tk, n_steps=n_steps),
        grid=(b, seq // tq, n_steps),
        in_specs=[pl.BlockSpec((A_HEADS, None, tq, HEAD_DIM), lambda bi, qi, si: (0, bi, qi, 0)),
                  pl.BlockSpec((A_KV_HEADS, None, span, HEAD_DIM), lambda bi, qi, si: (0, bi, si, 0)),
                  pl.BlockSpec((A_KV_HEADS, None, span, LANES), lambda bi, qi, si: (0, bi, si, 0))],
        out_specs=pl.BlockSpec((None, tq, A_WIDTH), lambda bi, qi, si: (bi, qi, 0)),
        out_shape=jax.ShapeDtypeStruct((b, seq, A_WIDTH), BF16),
        scratch_shapes=[pltpu.VMEM((PIPE_SLOTS, rows, tk), F32), pltpu.VMEM((PIPE_SLOTS, rows, tk), BF16),
                        pltpu.VMEM((PIPE_SLOTS, rows, LANES), F32),
                        pltpu.VMEM((A_KV_HEADS, rows, LANES), F32), pltpu.VMEM((A_KV_HEADS, rows, LANES), F32)],
        compiler_params=_params("parallel", "parallel", "arbitrary"),
        name="gqa_attn",
    )(qa, ka, va)


def _diff_kernel(slope_ref, q_ref, k_ref, v_ref, lq1_ref, lk1_ref, lq2_ref, lk2_ref, g_ref, o_ref,
                 s_buf, p_buf, a_buf, d_ref, m_ref, acc_ref, *, tq, tk, span, n_steps, lam_init):
    qi = pl.program_id(1)
    step = pl.program_id(2)

    @pl.when(step == 0)
    def _():
        _init_softmax_state(m_ref, acc_ref)
        d_ref[...] = (lax.broadcasted_iota(I32, (tq, tk), 0) - lax.broadcasted_iota(I32, (tq, tk), 1)).astype(F32)

    def keys(u):
        j = u >> 2
        return u & (B_HEADS - 1), j, pl.ds(pl.multiple_of(j * tk, tk), tk)

    def scores(u, slot):
        h, j, rows = keys(u)
        offset = (qi * tq - step * span - j * tk).astype(F32)
        bias = jnp.abs(d_ref[...] + offset) * slope_ref[h]
        for m in range(2):
            i = m * B_HEADS + h
            s = lax.dot_general(q_ref[i], k_ref[i, rows, :], NT_DIMS, preferred_element_type=F32)
            s_buf[slot, m] = s - bias

    def softmax(u, slot):
        h = u & (B_HEADS - 1)
        for m in range(2):
            p, alpha = _softmax_update(lambda: s_buf[slot, m], m_ref, m * B_HEADS + h)
            p_buf[slot, m] = p
            a_buf[slot, m] = alpha

    def values(u, slot):
        h, _, rows = keys(u)
        v = v_ref[h, rows, :]
        for m in range(2):
            i = m * B_HEADS + h
            pv = jnp.dot(p_buf[slot, m], v, preferred_element_type=F32)
            acc_ref[i] = jnp.tile(a_buf[slot, m], (1, 2)) * acc_ref[i] + pv

    _run_units((span // tk) * B_HEADS, scores, softmax, values)

    @pl.when(step == n_steps - 1)
    def _():
        lam = (jnp.exp(jnp.sum(lq1_ref[...] * lk1_ref[...], axis=1, keepdims=True))
               - jnp.exp(jnp.sum(lq2_ref[...] * lk2_ref[...], axis=1, keepdims=True)) + lam_init)

        def normalized(i):
            acc = acc_ref[i]
            return acc[:, 0:B_VDIM] / acc[:, B_VDIM:B_VDIM + 1]

        for h in range(B_HEADS):
            y = normalized(h) - lam * normalized(B_HEADS + h)
            ms = jnp.mean(y * y, axis=1, keepdims=True)
            y = y * lax.rsqrt(ms + RMS_EPS) * g_ref[...] * (1.0 - lam_init)
            o_ref[:, h * B_VDIM:(h + 1) * B_VDIM] = y.astype(BF16)


def _diff_attention(qb, kb, vb, lq1, lk1, lq2, lk2, subln_g, lam_init):
    _, b, seq, _ = qb.shape
    tq = min(512, seq)
    tk = min(1024, seq)
    span = min(2048, seq)
    n_steps = seq // span
    nh = 2 * B_HEADS
    slopes = jnp.asarray([2.0 ** (-8.0 * (h + 1) / B_HEADS) * LOG2E for h in range(B_HEADS)], F32)
    vec = pl.BlockSpec((1, HEAD_DIM), lambda bi, qi, si, *_: (0, 0))
    return pl.pallas_call(
        functools.partial(_diff_kernel, tq=tq, tk=tk, span=span, n_steps=n_steps, lam_init=lam_init),
        grid_spec=pltpu.PrefetchScalarGridSpec(
            num_scalar_prefetch=1,
            grid=(b, seq // tq, n_steps),
            in_specs=[pl.BlockSpec((nh, None, tq, HEAD_DIM), lambda bi, qi, si, *_: (0, bi, qi, 0)),
                      pl.BlockSpec((nh, None, span, HEAD_DIM), lambda bi, qi, si, *_: (0, bi, si, 0)),
                      pl.BlockSpec((B_HEADS, None, span, 2 * B_VDIM), lambda bi, qi, si, *_: (0, bi, si, 0)),
                      vec, vec, vec, vec, pl.BlockSpec((1, B_VDIM), lambda bi, qi, si, *_: (0, 0))],
            out_specs=pl.BlockSpec((None, tq, B_WIDTH), lambda bi, qi, si, *_: (bi, qi, 0)),
            scratch_shapes=[pltpu.VMEM((PIPE_SLOTS, 2, tq, tk), F32), pltpu.VMEM((PIPE_SLOTS, 2, tq, tk), BF16),
                            pltpu.VMEM((PIPE_SLOTS, 2, tq, LANES), F32), pltpu.VMEM((tq, tk), F32),
                            pltpu.VMEM((nh, tq, LANES), F32), pltpu.VMEM((nh, tq, 2 * B_VDIM), F32)],
        ),
        out_shape=jax.ShapeDtypeStruct((b, seq, B_WIDTH), BF16),
        compiler_params=_params("parallel", "parallel", "arbitrary"),
        name="diff_attn",
    )(slopes, qb, kb, vb, lq1, lk1, lq2, lk2, subln_g)


def _merge_kernel(ya_ref, yb_ref, gate_ref, h_ref, p_ref, wa_ref, wb_ref, wo_ref, g_ref, b_ref,
                  wrh_ref, wrl_ref, wpg_ref, wpp_ref, x1_ref, base_ref, aff_ref):
    ma = jnp.dot(ya_ref[...], wa_ref[...], preferred_element_type=F32)
    mb = jnp.dot(yb_ref[...], wb_ref[...], preferred_element_type=F32)
    merged = _sigmoid(gate_ref[:, 0:D_MODEL]) * ma + _sigmoid(gate_ref[:, D_MODEL:2 * D_MODEL]) * mb
    out = jnp.dot(merged.astype(BF16), wo_ref[...], preferred_element_type=F32)
    x1 = _layer_norm(ALPHA * h_ref[...] + out, g_ref[...], b_ref[...])
    x1_ref[...] = x1
    xh, xl = _split_bf16(x1)
    ple_gate = _sigmoid(jnp.dot(xh, wpg_ref[...], preferred_element_type=F32))
    base_ref[...] = ALPHA * x1 + ple_gate * jnp.dot(p_ref[...].astype(BF16), wpp_ref[...], preferred_element_type=F32)
    wh = wrh_ref[...]
    logits = (lax.dot_general(wh, xh, NT_DIMS, preferred_element_type=F32)
              + lax.dot_general(wh, xl, NT_DIMS, preferred_element_type=F32)
              + lax.dot_general(wrl_ref[...], xh, NT_DIMS, preferred_element_type=F32))
    e = jnp.exp(logits - jnp.max(logits, axis=0, keepdims=True))
    aff_ref[...] = e / jnp.sum(e, axis=0, keepdims=True)


def _merge(ya, yb, gates, h, p2, wa, wb, wo, g, b, wrh, wrl, wpg, wpp):
    n = h.shape[0]
    tm = ROW_TILE

    def rows(w):
        return pl.BlockSpec((tm, w), lambda i: (i, 0))

    return pl.pallas_call(
        _merge_kernel,
        grid=(n // tm,),
        in_specs=[rows(A_WIDTH), rows(B_WIDTH), rows(2 * D_MODEL), rows(D_MODEL), rows(PLE_DIM),
                  _full((A_WIDTH, D_MODEL)), _full((B_WIDTH, D_MODEL)), _full((D_MODEL, D_MODEL)),
                  _full((1, D_MODEL)), _full((1, D_MODEL)),
                  _full((N_EXPERTS, D_MODEL)), _full((N_EXPERTS, D_MODEL)),
                  _full((D_MODEL, D_MODEL)), _full((PLE_DIM, D_MODEL))],
        out_specs=[rows(D_MODEL), rows(D_MODEL), pl.BlockSpec((N_EXPERTS, tm), lambda i: (0, i))],
        out_shape=[jax.ShapeDtypeStruct((n, D_MODEL), F32), jax.ShapeDtypeStruct((n, D_MODEL), F32),
                   jax.ShapeDtypeStruct((N_EXPERTS, n), F32)],
        compiler_params=_params("parallel"),
        name="merge",
    )(ya, yb, gates, h, p2, wa, wb, wo, g, b, wrh, wrl, wpg, wpp)


def _route_kernel(aff_ref, rank_ref, idx_ref, blkp_ref, sel_scr, wi_scr, p_scr, *, nb, cap):
    ne = N_EXPERTS
    shape3 = (ne, nb, LANES)
    bits = pltpu.bitcast(aff_ref[...], I32).reshape(shape3)
    capf = float(cap)

    def count(mask3):
        part = jnp.sum(jnp.where(mask3, 1.0, 0.0), axis=1, keepdims=True)
        return jnp.sum(part, axis=2, keepdims=True)

    def search(i, thr):
        cand = thr | jnp.left_shift(jnp.int32(1), 30 - i)
        return jnp.where(count(bits >= cand) >= capf, cand, thr)

    thr = lax.fori_loop(0, 31, search, jnp.zeros((ne, 1, 1), I32))
    gt = bits > thr
    eq = bits == thr
    need = capf - count(gt)

    li = lax.broadcasted_iota(I32, (LANES, LANES), 0)
    lj = lax.broadcasted_iota(I32, (LANES, LANES), 1)
    upper_incl = jnp.where(li <= lj, 1.0, 0.0).astype(BF16)
    lower_incl = jnp.where(lj <= li, 1.0, 0.0).astype(BF16)
    bi = lax.broadcasted_iota(I32, (nb, nb), 0)
    bj = lax.broadcasted_iota(I32, (nb, nb), 1)
    lower_strict = jnp.where(bj < bi, 1.0, 0.0).astype(BF16)
    upper_strict = jnp.where(bi < bj, 1.0, 0.0).astype(BF16)

    def prefix(mask2):
        wi = jnp.dot(mask2.astype(BF16), upper_incl, preferred_element_type=F32)
        tot = jnp.broadcast_to(wi[:, LANES - 1:LANES], wi.shape).astype(BF16)
        blk = [jnp.dot(lower_strict, tot[e * nb:(e + 1) * nb], preferred_element_type=F32)
               for e in range(ne)]
        return wi, jnp.concatenate(blk, axis=0)

    eqf = jnp.where(eq, 1.0, 0.0).reshape(ne * nb, LANES)
    wi, bp = prefix(eqf)
    eq_rank = (wi + bp - eqf).reshape(shape3)
    sel = gt | (eq & (eq_rank < need))
    self_ = jnp.where(sel, 1.0, 0.0).reshape(ne * nb, LANES)
    wi, bp = prefix(self_)
    rank = wi + bp - self_
    rank_ref[...] = jnp.where(self_ > 0.0, rank, -1.0).astype(I32)
    sel_scr[...] = self_
    wi_scr[...] = wi
    p_scr[...] = bp

    chunk = min(cap, 1024)
    ones8 = jnp.ones((8, LANES), BF16)
    b_col = lax.broadcasted_iota(I32, (nb, 1), 0).astype(F32)

    def per_expert(e, carry):
        r0 = pl.multiple_of(e * nb, 8)
        sel_e = sel_scr[pl.ds(r0, nb), :].astype(BF16)
        wi_e = wi_scr[pl.ds(r0, nb), :]
        p_e = p_scr[pl.ds(r0, nb), :][:, 0:1]
        p_end = p_e + wi_e[:, LANES - 1:LANES]
        wi_t = lax.dot_general(lower_incl, sel_e, NT_DIMS, preferred_element_type=F32).astype(BF16)
        tot_row = lax.dot_general(ones8, sel_e, NT_DIMS, preferred_element_type=F32)
        p_row = jnp.dot(tot_row.astype(BF16), upper_strict, preferred_element_type=F32)
        blkp_ref[e] = p_row[0:1, :].astype(I32)
        for c in range(cap // chunk):
            slot = (lax.broadcasted_iota(I32, (1, chunk), 1) + c * chunk).astype(F32)
            hot = (p_e <= slot) & (slot < p_end)
            w_slot = jnp.dot(wi_t, jnp.where(hot, 1.0, 0.0).astype(BF16), preferred_element_type=F32)
            p_slot = jnp.sum(jnp.where(hot, p_e, 0.0), axis=0, keepdims=True)
            b_slot = jnp.sum(jnp.where(hot, b_col, 0.0), axis=0, keepdims=True)
            lane = jnp.sum(jnp.where(w_slot <= slot - p_slot, 1.0, 0.0), axis=0, keepdims=True)
            idx_ref[e, :, c * chunk:(c + 1) * chunk] = (b_slot * LANES + lane).astype(I32)
        return carry

    lax.fori_loop(0, ne, per_expert, 0)


def _route(aff2, n, cap):
    nb = n // LANES
    return pl.pallas_call(
        functools.partial(_route_kernel, nb=nb, cap=cap),
        out_shape=[jax.ShapeDtypeStruct((N_EXPERTS * nb, LANES), I32),
                   jax.ShapeDtypeStruct((N_EXPERTS, 1, cap), I32),
                   jax.ShapeDtypeStruct((N_EXPERTS, 1, nb), I32)],
        scratch_shapes=[pltpu.VMEM((N_EXPERTS * nb, LANES), F32)] * 3,
        compiler_params=pltpu.CompilerParams(vmem_limit_bytes=VMEM_LIMIT),
        name="route",
    )(aff2)


def _ffn_kernel(idx_ref, x_hbm, wg_ref, wu_ref, wd_ref, y_ref, xbuf, sem, *, tm, tiles, n_steps):
    step = pl.program_id(0) * tiles + pl.program_id(1)
    slot = step & 1

    def fetch(src_step, dst_slot):
        for r in range(tm):
            pltpu.make_async_copy(x_hbm.at[pl.ds(idx_ref[src_step * tm + r], 1), :],
                                  xbuf.at[dst_slot, pl.ds(r, 1), :], sem.at[dst_slot]).start()

    def wait(which):
        pltpu.make_async_copy(x_hbm.at[pl.ds(0, tm), :], xbuf.at[which], sem.at[which]).wait()

    @pl.when(step == 0)
    def _():
        fetch(0, 0)

    fetch(jnp.minimum(step + 1, n_steps - 1), 1 - slot)
    wait(slot)
    x = xbuf[slot].astype(BF16)
    fc = 512
    acc = None
    for c in range(EXPERT_FF // fc):
        sl = slice(c * fc, (c + 1) * fc)
        a = jnp.dot(x, wg_ref[:, sl], preferred_element_type=F32)
        u = jnp.dot(x, wu_ref[:, sl], preferred_element_type=F32)
        mid = (a * _sigmoid(a) * u).astype(BF16)
        part = jnp.dot(mid, wd_ref[sl, :], preferred_element_type=F32)
        acc = part if acc is None else acc + part
    y_ref[...] = acc.astype(BF16)

    @pl.when(step == n_steps - 1)
    def _():
        wait(1 - slot)


def _ffn(idx_flat, x1, wg, wu, wd, cap):
    tm = min(512, cap)
    tiles = cap // tm
    return pl.pallas_call(
        functools.partial(_ffn_kernel, tm=tm, tiles=tiles, n_steps=N_EXPERTS * tiles),
        grid_spec=pltpu.PrefetchScalarGridSpec(
            num_scalar_prefetch=1,
            grid=(N_EXPERTS, tiles),
            in_specs=[pl.BlockSpec(memory_space=pl.ANY),
                      pl.BlockSpec((None, D_MODEL, EXPERT_FF), lambda e, i, idx: (e, 0, 0)),
                      pl.BlockSpec((None, D_MODEL, EXPERT_FF), lambda e, i, idx: (e, 0, 0)),
                      pl.BlockSpec((None, EXPERT_FF, D_MODEL), lambda e, i, idx: (e, 0, 0))],
            out_specs=pl.BlockSpec((tm, D_MODEL), lambda e, i, idx: (e * tiles + i, 0)),
            scratch_shapes=[pltpu.VMEM((2, tm, D_MODEL), F32), pltpu.SemaphoreType.DMA((2,))],
        ),
        out_shape=jax.ShapeDtypeStruct((N_EXPERTS * cap, D_MODEL), BF16),
        compiler_params=_params("arbitrary", "arbitrary"),
        name="ffn",
    )(idx_flat, x1, wg, wu, wd)


TOK_TILE = 128
ROW_ALIGN = 16
WIN = TOK_TILE + ROW_ALIGN


def _final_kernel(start_ref, base_ref, rank_ref, aff_ref, g_ref, b_ref, ye_hbm, o_ref, buf, sem, *, cap, n_steps):
    t = pl.program_id(0)
    slot = t & 1

    def window(step, e):
        first = start_ref[step * N_EXPERTS + e]
        ws = jnp.minimum((first // ROW_ALIGN) * ROW_ALIGN, cap - WIN)
        return pl.multiple_of(ws, ROW_ALIGN)

    def fetch(step, dst_slot):
        for e in range(N_EXPERTS):
            pltpu.make_async_copy(ye_hbm.at[pl.ds(e * cap + window(step, e), WIN), :],
                                  buf.at[dst_slot, pl.ds(e * WIN, WIN), :], sem.at[dst_slot]).start()

    @pl.when(t == 0)
    def _():
        fetch(0, 0)

    @pl.when(t + 1 < n_steps)
    def _():
        fetch(t + 1, 1 - slot)

    total = base_ref[...]
    pltpu.make_async_copy(ye_hbm.at[pl.ds(0, N_EXPERTS * WIN), :], buf.at[slot], sem.at[slot]).wait()
    lane = lax.broadcasted_iota(I32, (1, WIN), 1)
    for e in range(N_EXPERTS):
        hot = rank_ref[:, e:e + 1] == lane + window(t, e)
        y = jnp.dot(jnp.where(hot, 1.0, 0.0).astype(BF16), buf[slot, e * WIN:(e + 1) * WIN, :],
                    preferred_element_type=F32)
        total = total + aff_ref[:, e:e + 1] * y

    o_ref[...] = _layer_norm(total, g_ref[...], b_ref[...])


def _final(starts, base, rank_t, aff_t, g, b, ye, cap):
    n = base.shape[0]
    tm = TOK_TILE

    def rows(w):
        return pl.BlockSpec((tm, w), lambda i, *_: (i, 0))

    def full(shape):
        return pl.BlockSpec(shape, lambda i, *_: (0,) * len(shape))

    return pl.pallas_call(
        functools.partial(_final_kernel, cap=cap, n_steps=n // tm),
        grid_spec=pltpu.PrefetchScalarGridSpec(
            num_scalar_prefetch=1,
            grid=(n // tm,),
            in_specs=[rows(D_MODEL), rows(N_EXPERTS), rows(N_EXPERTS), full((1, D_MODEL)), full((1, D_MODEL)),
                      pl.BlockSpec(memory_space=pl.ANY)],
            out_specs=rows(D_MODEL),
            scratch_shapes=[pltpu.VMEM((2, N_EXPERTS * WIN, D_MODEL), BF16), pltpu.SemaphoreType.DMA((2,))],
        ),
        out_shape=jax.ShapeDtypeStruct((n, D_MODEL), F32),
        compiler_params=_params("arbitrary"),
        name="final",
    )(starts, base, rank_t, aff_t, g, b, ye)


def _rope_tables(seq):
    n_rows = seq // GRID_W
    rows = jnp.repeat(jnp.arange(n_rows, dtype=F32), GRID_W)
    cols = jnp.tile(jnp.arange(GRID_W, dtype=F32), n_rows)
    sec = HEAD_DIM // 2
    inv_freq = ROPE_THETA ** (-jnp.arange(0, sec, 2, dtype=F32) / sec)
    ang_r = rows[:, None] * inv_freq[None, :]
    ang_c = cols[:, None] * inv_freq[None, :]
    cos = jnp.concatenate([jnp.cos(ang_r)] * 2 + [jnp.cos(ang_c)] * 2, axis=-1)
    sin = jnp.concatenate([-jnp.sin(ang_r), jnp.sin(ang_r), -jnp.sin(ang_c), jnp.sin(ang_c)], axis=-1)
    return jnp.tile(cos, (1, 2)), jnp.tile(sin, (1, 2))


def _trunk(x, p, w):
    b, seq, _ = x.shape
    n = b * seq
    cap = max(1, CAPACITY_FACTOR * n // N_EXPERTS)
    assert seq % ROW_TILE == 0 and n % (8 * LANES) == 0 and cap % TOK_TILE == 0 and cap >= WIN
    cos_t, sin_t = _rope_tables(seq)

    h, qa, ka, va, qb, kb, vb, gates = _in_proj(
        x.reshape(n, D_MODEL), w["ln_emb_g"], w["ln_emb_b"], w["w_in"], w["q_norm_g"], w["k_norm_g"],
        cos_t, sin_t, w["seg"], seq)

    def per_batch(t):
        return t.reshape(t.shape[0], b, seq, t.shape[2])

    ya = _gqa_attention(per_batch(qa), per_batch(ka), per_batch(va))
    lam_init = 0.8 - 0.6 * math.exp(-0.3 * 0)
    yb = _diff_attention(per_batch(qb), per_batch(kb), per_batch(vb),
                         w["lambda_q1"], w["lambda_k1"], w["lambda_q2"], w["lambda_k2"], w["subln_g"], lam_init)

    x1, base, aff = _merge(ya.reshape(n, -1), yb.reshape(n, -1), gates, h, p.reshape(n, PLE_DIM),
                           w["w_branch_a"], w["w_branch_b"], w["w_out"], w["ln1_g"], w["ln1_b"],
                           w["wr_hi"], w["wr_lo"], w["w_ple_gate"], w["w_ple_proj"])

    nb = n // LANES
    rank2, idx, blkp = _route(aff.reshape(N_EXPERTS * nb, LANES), n, cap)
    ye = _ffn(idx.reshape(N_EXPERTS * cap), x1, w["w_gate"], w["w_up"], w["w_down"], cap)

    starts = blkp.reshape(N_EXPERTS, nb)[:, ::TOK_TILE // LANES]
    out = _final(starts.T.reshape(-1), base, rank2.reshape(N_EXPERTS, n).T, aff.T, w["ln2_g"], w["ln2_b"], ye, cap)
    return out.reshape(b, seq, D_MODEL)


def _prep_weights(ln_emb_g, ln_emb_b, w_in, q_norm_g, k_norm_g, lambda_q1, lambda_k1, lambda_q2, lambda_k2,
                  subln_g, w_branch_a, w_branch_b, w_out, ln1_g, ln1_b, w_router, w_gate, w_up, w_down,
                  w_ple_gate, w_ple_proj, ln2_g, ln2_b):
    i = 0
    seg_i = lax.broadcasted_iota(I32, (A_WIDTH, A_WIDTH), 0) // HEAD_DIM
    seg_j = lax.broadcasted_iota(I32, (A_WIDTH, A_WIDTH), 1) // HEAD_DIM
    wr_t = w_router[i].T
    wr_hi = wr_t.astype(BF16)
    return {
        "ln_emb_g": ln_emb_g.reshape(1, D_MODEL), "ln_emb_b": ln_emb_b.reshape(1, D_MODEL),
        "w_in": w_in[i].astype(BF16),
        "q_norm_g": jnp.tile(q_norm_g[i], 2).reshape(1, LANES), "k_norm_g": jnp.tile(k_norm_g[i], 2).reshape(1, LANES),
        "seg": (seg_i == seg_j).astype(BF16),
        "lambda_q1": lambda_q1[i].reshape(1, HEAD_DIM), "lambda_k1": lambda_k1[i].reshape(1, HEAD_DIM),
        "lambda_q2": lambda_q2[i].reshape(1, HEAD_DIM), "lambda_k2": lambda_k2[i].reshape(1, HEAD_DIM),
        "subln_g": subln_g[i].reshape(1, B_VDIM),
        "w_branch_a": w_branch_a[i].astype(BF16), "w_branch_b": w_branch_b[i].astype(BF16),
        "w_out": w_out[i].astype(BF16),
        "ln1_g": ln1_g[i].reshape(1, D_MODEL), "ln1_b": ln1_b[i].reshape(1, D_MODEL),
        "wr_hi": wr_hi, "wr_lo": (wr_t - wr_hi.astype(F32)).astype(BF16),
        "w_gate": w_gate[i].astype(BF16), "w_up": w_up[i].astype(BF16), "w_down": w_down[i].astype(BF16),
        "w_ple_gate": w_ple_gate[i].astype(BF16), "w_ple_proj": w_ple_proj[i].astype(BF16),
        "ln2_g": ln2_g[i].reshape(1, D_MODEL), "ln2_b": ln2_b[i].reshape(1, D_MODEL),
    }


def kernel(x_prompt, x_sample, p_prompt, p_sample, ln_emb_g, ln_emb_b, w_in, q_norm_g, k_norm_g, lambda_q1, lambda_k1, lambda_q2, lambda_k2, subln_g, w_branch_a, w_branch_b, w_out, ln1_g, ln1_b, w_router, w_gate, w_up, w_down, w_ple_gate, w_ple_proj, ln2_g, ln2_b):
    w = _prep_weights(ln_emb_g, ln_emb_b, w_in, q_norm_g, k_norm_g, lambda_q1, lambda_k1, lambda_q2, lambda_k2,
                      subln_g, w_branch_a, w_branch_b, w_out, ln1_g, ln1_b, w_router, w_gate, w_up, w_down,
                      w_ple_gate, w_ple_proj, ln2_g, ln2_b)
    return (_trunk(x_prompt, p_prompt[0], w), _trunk(x_sample, p_sample[0], w))
```

```python
import functools
import math

import jax
import jax.numpy as jnp
from jax import lax
from jax.experimental import pallas as pl
from jax.experimental.pallas import tpu as pltpu

F32 = jnp.float32
BF16 = jnp.bfloat16
I32 = jnp.int32

D_MODEL = 1024
DEPTH = 1
HEAD_DIM = 64
A_HEADS = 8
A_KV_HEADS = 2
A_GROUP = A_HEADS // A_KV_HEADS
A_WIDTH = A_HEADS * HEAD_DIM
B_HEADS = 4
B_VDIM = 2 * HEAD_DIM
B_WIDTH = B_HEADS * B_VDIM
GRID_W = 64
ROPE_THETA = 10000.0
N_EXPERTS = 16
EXPERT_FF = 2048
CAPACITY_FACTOR = 2
PLE_DIM = 256
LN_EPS = 1e-5
RMS_EPS = 1e-6
ALPHA = (2 * DEPTH) ** 0.25
LOG2E = math.log2(math.e)
QSCALE = HEAD_DIM ** -0.5 * LOG2E
LANES = 128
ROPE_HALF = HEAD_DIM // 4

C_QA = 0
C_KA = C_QA + A_WIDTH
C_VA = C_KA + A_KV_HEADS * HEAD_DIM
C_QB = C_VA + A_KV_HEADS * HEAD_DIM
C_KB = C_QB + 2 * B_HEADS * HEAD_DIM
C_VB = C_KB + 2 * B_HEADS * HEAD_DIM
C_G = C_VB + B_WIDTH
IN_COLS = C_G + 2 * D_MODEL

VMEM_LIMIT = 56 * 1024 * 1024
ROW_TILE = 512
NT_DIMS = (((1,), (1,)), ((), ()))


def _params(*sem):
    return pltpu.CompilerParams(dimension_semantics=sem, vmem_limit_bytes=VMEM_LIMIT)


def _full(shape):
    n = len(shape)
    return pl.BlockSpec(shape, lambda *_: (0,) * n, pipeline_mode=pl.Buffered(1))


def _layer_norm(x, g, b):
    mu = jnp.mean(x, axis=-1, keepdims=True)
    xc = x - mu
    var = jnp.mean(xc * xc, axis=-1, keepdims=True)
    return xc * lax.rsqrt(var + LN_EPS) * g + b


def _sigmoid(x):
    return 1.0 / (1.0 + jnp.exp(-x))


def _split_bf16(x):
    hi = x.astype(BF16)
    lo = (x - hi.astype(F32)).astype(BF16)
    return hi, lo


def _head_mean_sq(x, seg):
    hi, lo = _split_bf16(x * x)
    s = jnp.dot(hi, seg, preferred_element_type=F32) + jnp.dot(lo, seg, preferred_element_type=F32)
    return s * (1.0 / HEAD_DIM)


def _rope(x, cos, sin, first_half):
    fwd = pltpu.roll(x, LANES - ROPE_HALF, 1)
    bwd = pltpu.roll(x, ROPE_HALF, 1)
    return x * cos + jnp.where(first_half, fwd, bwd) * sin


def _in_proj_kernel(x_ref, lng_ref, lnb_ref, w_ref, qg_ref, kg_ref, cos_ref, sin_ref, seg_ref, qc_ref, kc_ref,
                    h_ref, qa_ref, ka_ref, va_ref, qb_ref, kb_ref, vb_ref, gate_ref, *, tiles_per_seq):
    xn = _layer_norm(x_ref[...], lng_ref[...], lnb_ref[...])
    h_ref[...] = xn
    xb = xn.astype(BF16)

    def proj(lo, hi):
        return jnp.dot(xb, w_ref[:, lo:hi], preferred_element_type=F32)

    def store_heads(ref, x):
        for i in range(x.shape[1] // HEAD_DIM):
            ref[i] = x[:, i * HEAD_DIM:(i + 1) * HEAD_DIM].astype(BF16)

    cos = cos_ref[...]
    sin = sin_ref[...]
    lane = lax.broadcasted_iota(I32, cos.shape, 1)
    first_half = (lane & ROPE_HALF) == 0

    qa = proj(C_QA, C_KA)
    qn = qa * lax.rsqrt(_head_mean_sq(qa, seg_ref[...]) + RMS_EPS)
    for c in range(A_WIDTH // LANES):
        pair = _rope(qn[:, c * LANES:(c + 1) * LANES] * qg_ref[...], cos, sin, first_half) * QSCALE
        qa_ref[2 * c] = pair[:, 0:HEAD_DIM].astype(BF16)
        qa_ref[2 * c + 1] = pair[:, HEAD_DIM:LANES].astype(BF16)

    ka = proj(C_KA, C_VA)
    kn = ka * lax.rsqrt(_head_mean_sq(ka, seg_ref[0:LANES, 0:LANES]) + RMS_EPS)
    store_heads(ka_ref, _rope(kn * kg_ref[...], cos, sin, first_half))
    one_at_0 = jnp.where(lane == 0, 1.0, 0.0)
    va = proj(C_VA, C_QB)
    for g in range(A_KV_HEADS):
        v_first = va if g == 0 else pltpu.roll(va, HEAD_DIM, 1)
        va_ref[g] = jnp.where(lane < HEAD_DIM, v_first, jnp.where(lane == HEAD_DIM, 1.0, 0.0)).astype(BF16)
    pos = (pl.program_id(0) % tiles_per_seq) * cos.shape[0] + lax.broadcasted_iota(I32, cos.shape, 0)
    hi = (pos >> 7).astype(F32)
    lo = (pos & (LANES - 1)).astype(F32)
    pair_of = lane >> 1
    q_pos = jnp.where(pair_of == 32, hi, jnp.where(pair_of == 33, lo, 0.0))
    k_pos = jnp.where(pair_of == 34, hi, jnp.where(pair_of == 35, lo, 0.0))

    def store_with_features(ref, x, pos_feat, const_ref):
        for i in range(2 * B_HEADS):
            chunk = x[:, (i // 2) * LANES:(i // 2 + 1) * LANES]
            first = chunk if i % 2 == 0 else pltpu.roll(chunk, HEAD_DIM, 1)
            feat = pos_feat + const_ref[i % B_HEADS:i % B_HEADS + 1, :]
            ref[i] = jnp.where(lane < HEAD_DIM, first, feat).astype(BF16)

    store_with_features(qb_ref, proj(C_QB, C_KB) * QSCALE, q_pos, qc_ref)
    store_with_features(kb_ref, proj(C_KB, C_VB), k_pos, kc_ref)
    vb = proj(C_VB, C_G)
    for i in range(B_HEADS):
        vb_ref[i, :, 0:B_VDIM] = vb[:, i * B_VDIM:(i + 1) * B_VDIM].astype(BF16)
        vb_ref[i, :, B_VDIM:2 * B_VDIM] = one_at_0.astype(BF16)
    gate_ref[...] = proj(C_G, IN_COLS)


def _alibi_feature_constants():
    c = jnp.asarray([2.0 ** (-8.0 * (h + 1) / B_HEADS) * LOG2E for h in range(B_HEADS)], F32)
    c1 = c.astype(BF16).astype(F32)
    c2 = (c - c1).astype(BF16).astype(F32)
    parts = jnp.stack([LANES * c1, LANES * c2, c1, c2], axis=1)
    zeros = jnp.zeros((B_HEADS, LANES), F32)
    return (zeros.at[:, HEAD_DIM + 4:HEAD_DIM + 8].set(-parts), zeros.at[:, HEAD_DIM:HEAD_DIM + 4].set(parts))


def _in_proj(x2, lng, lnb, w_in, qg, kg, cos_t, sin_t, seg, seq):
    n = x2.shape[0]
    tm = ROW_TILE
    tiles_per_seq = seq // tm

    def rows(w):
        return pl.BlockSpec((tm, w), lambda i: (i, 0))

    def heads(nh, w):
        return pl.BlockSpec((nh, tm, w), lambda i: (0, i, 0))

    tab = pl.BlockSpec((tm, LANES), lambda i: (i % tiles_per_seq, 0))
    head_outs = [(A_HEADS, HEAD_DIM), (A_KV_HEADS, HEAD_DIM), (A_KV_HEADS, LANES),
                 (2 * B_HEADS, LANES), (2 * B_HEADS, LANES), (B_HEADS, 2 * B_VDIM)]
    q_const, k_const = _alibi_feature_constants()
    return pl.pallas_call(
        functools.partial(_in_proj_kernel, tiles_per_seq=tiles_per_seq),
        grid=(n // tm,),
        in_specs=[rows(D_MODEL), _full((1, D_MODEL)), _full((1, D_MODEL)), _full((D_MODEL, IN_COLS)),
                  _full((1, LANES)), _full((1, LANES)), tab, tab, _full((A_WIDTH, A_WIDTH)),
                  _full((B_HEADS, LANES)), _full((B_HEADS, LANES))],
        out_specs=[rows(D_MODEL)] + [heads(nh, w) for nh, w in head_outs] + [rows(2 * D_MODEL)],
        out_shape=([jax.ShapeDtypeStruct((n, D_MODEL), F32)]
                   + [jax.ShapeDtypeStruct((nh, n, w), BF16) for nh, w in head_outs]
                   + [jax.ShapeDtypeStruct((n, 2 * D_MODEL), F32)]),
        compiler_params=_params("parallel"),
        name="in_proj",
    )(x2, lng, lnb, w_in, qg, kg, cos_t, sin_t, seg, q_const, k_const)


PIPE_SLOTS = 2


def _run_units(n_units, scores, softmax, values):
    scores(0, 0)
    scores(1, 1)
    softmax(0, 0)

    def two_units(i, carry):
        t = PIPE_SLOTS * (i + 1)
        scores(t, 0)
        softmax(t - 1, 1)
        values(t - 2, 0)
        scores(t + 1, 1)
        softmax(t, 0)
        values(t - 1, 1)
        return carry

    lax.fori_loop(0, (n_units >> 1) - 1, two_units, 0)
    softmax(n_units - 1, 1)
    values(n_units - 2, 0)
    values(n_units - 1, 1)


def _softmax_update(load_s, m_ref, i):
    m_prev = m_ref[i]
    m_new = jnp.maximum(m_prev, jnp.max(load_s(), axis=1, keepdims=True))
    alpha = jnp.exp2(m_prev - m_new)
    s = load_s()
    p = jnp.exp2(s - jnp.tile(m_new, (1, s.shape[1] // LANES)))
    m_ref[i] = m_new
    return p.astype(BF16), alpha


def _init_softmax_state(m_ref, acc_ref):
    m_ref[...] = jnp.full(m_ref.shape, -jnp.inf, F32)
    acc_ref[...] = jnp.zeros(acc_ref.shape, F32)


def _gqa_kernel(q_ref, k_ref, v_ref, o_ref, s_buf, p_buf, a_buf, m_ref, acc_ref, *, tq, tk, n_sub, n_steps):
    step = pl.program_id(2)

    @pl.when(step == 0)
    def _():
        _init_softmax_state(m_ref, acc_ref)

    def keys(u):
        return u & (A_KV_HEADS - 1), pl.ds(pl.multiple_of((u >> 1) * tk, tk), tk)

    def scores(u, slot):
        g, rows = keys(u)
        q = q_ref[pl.ds(g * A_GROUP, A_GROUP)].reshape(A_GROUP * tq, HEAD_DIM)
        s_buf[slot] = lax.dot_general(q, k_ref[g, rows, :], NT_DIMS, preferred_element_type=F32)

    def softmax(u, slot):
        p, alpha = _softmax_update(lambda: s_buf[slot], m_ref, u & (A_KV_HEADS - 1))
        p_buf[slot] = p
        a_buf[slot] = alpha

    def values(u, slot):
        g, rows = keys(u)
        pv = jnp.dot(p_buf[slot], v_ref[g, rows, :], preferred_element_type=F32)
        acc_ref[g] = a_buf[slot] * acc_ref[g] + pv

    _run_units(n_sub * A_KV_HEADS, scores, softmax, values)

    @pl.when(step == n_steps - 1)
    def _():
        for h in range(A_HEADS):
            g, r = divmod(h, A_GROUP)
            acc = acc_ref[g, r * tq:(r + 1) * tq, :]
            y = acc[:, 0:HEAD_DIM] / acc[:, HEAD_DIM:HEAD_DIM + 1]
            o_ref[:, h * HEAD_DIM:(h + 1) * HEAD_DIM] = y.astype(BF16)


def _gqa_attention(qa, ka, va):
    _, b, seq, _ = qa.shape
    tq = min(256, seq)
    tk = min(1024, seq)
    span = seq
    n_steps = seq // span
    rows = A_GROUP * tq
    resident = pl.Buffered(1)
    return pl.pallas_call(
        functools.partial(_gqa_kernel, tq=tq, tk=tk, n_sub=span // tk, n_steps=n_steps),
        grid=(b, seq // tq, n_steps),
        in_specs=[pl.BlockSpec((A_HEADS, None, tq, HEAD_DIM), lambda bi, qi, si: (0, bi, qi, 0)),
                  pl.BlockSpec((A_KV_HEADS, None, span, HEAD_DIM), lambda bi, qi, si: (0, bi, si, 0),
                               pipeline_mode=resident),
                  pl.BlockSpec((A_KV_HEADS, None, span, LANES), lambda bi, qi, si: (0, bi, si, 0),
                               pipeline_mode=resident)],
        out_specs=pl.BlockSpec((None, tq, A_WIDTH), lambda bi, qi, si: (bi, qi, 0)),
        out_shape=jax.ShapeDtypeStruct((b, seq, A_WIDTH), BF16),
        scratch_shapes=[pltpu.VMEM((PIPE_SLOTS, rows, tk), F32), pltpu.VMEM((PIPE_SLOTS, rows, tk), BF16),
                        pltpu.VMEM((PIPE_SLOTS, rows, LANES), F32),
                        pltpu.VMEM((A_KV_HEADS, rows, LANES), F32), pltpu.VMEM((A_KV_HEADS, rows, LANES), F32)],
        compiler_params=_params("parallel", "parallel", "arbitrary"),
        name="gqa_attn",
    )(qa, ka, va)


def _diff_kernel(slope_ref, q_ref, k_ref, v_ref, lq1_ref, lk1_ref, lq2_ref, lk2_ref, g_ref, o_ref,
                 s_buf, p_buf, a_buf, m_ref, acc_ref, *, tq, tk, span, n_steps, lam_init):
    qi = pl.program_id(1)
    step = pl.program_id(2)
    n_sub = span // tk
    q_pos = qi * tq
    diag_step = q_pos // span
    diag_sub = (q_pos // tk) % n_sub
    lane = lax.broadcasted_iota(I32, (1, LANES), 1)

    @pl.when(step == 0)
    def _():
        _init_softmax_state(m_ref, acc_ref)

    @pl.when(step == diag_step)
    def _():
        rows = pl.ds(pl.multiple_of(diag_sub * tk, tk), tk)
        offset = q_pos - (q_pos // tk) * tk
        dist = jnp.abs(lax.broadcasted_iota(I32, (tq, tk), 0) - lax.broadcasted_iota(I32, (tq, tk), 1) + offset)
        dist = dist.astype(F32)
        for h in range(B_HEADS):
            bias = dist * slope_ref[h]
            v = v_ref[h, rows, :]
            for m in range(2):
                i = m * B_HEADS + h
                s = lax.dot_general(q_ref[i][:, 0:HEAD_DIM], k_ref[i, rows, 0:HEAD_DIM], NT_DIMS,
                                    preferred_element_type=F32) - bias
                p, alpha = _softmax_update(lambda: s, m_ref, i)
                acc_ref[i] = jnp.tile(alpha, (1, 2)) * acc_ref[i] + jnp.dot(p, v, preferred_element_type=F32)

    def keys(u):
        j = u >> 2
        j = j + jnp.where((step == diag_step) & (j >= diag_sub), 1, 0)
        return u & (B_HEADS - 1), j, pl.ds(pl.multiple_of(j * tk, tk), tk)

    def scores(u, slot):
        h, j, rows = keys(u)
        keys_before = q_pos >= step * span + (j + 1) * tk
        sign = jnp.where(keys_before, -1.0, 1.0)
        lane_scale = jnp.where(lane < HEAD_DIM, 1.0, sign).astype(BF16)
        for m in range(2):
            i = m * B_HEADS + h
            s_buf[slot, m] = lax.dot_general(q_ref[i] * lane_scale, k_ref[i, rows, :], NT_DIMS,
                                             preferred_element_type=F32)

    def softmax(u, slot):
        h = u & (B_HEADS - 1)
        for m in range(2):
            p, alpha = _softmax_update(lambda: s_buf[slot, m], m_ref, m * B_HEADS + h)
            p_buf[slot, m] = p
            a_buf[slot, m] = alpha

    def values(u, slot):
        h, _, rows = keys(u)
        v = v_ref[h, rows, :]
        for m in range(2):
            i = m * B_HEADS + h
            pv = jnp.dot(p_buf[slot, m], v, preferred_element_type=F32)
            acc_ref[i] = jnp.tile(a_buf[slot, m], (1, 2)) * acc_ref[i] + pv

    _run_units(jnp.where(step == diag_step, n_sub - 1, n_sub) * B_HEADS, scores, softmax, values)

    @pl.when(step == n_steps - 1)
    def _():
        lam = (jnp.exp(jnp.sum(lq1_ref[...] * lk1_ref[...], axis=1, keepdims=True))
               - jnp.exp(jnp.sum(lq2_ref[...] * lk2_ref[...], axis=1, keepdims=True)) + lam_init)

        def normalized(i):
            acc = acc_ref[i]
            return acc[:, 0:B_VDIM] / acc[:, B_VDIM:B_VDIM + 1]

        for h in range(B_HEADS):
            y = normalized(h) - lam * normalized(B_HEADS + h)
            ms = jnp.mean(y * y, axis=1, keepdims=True)
            y = y * lax.rsqrt(ms + RMS_EPS) * g_ref[...] * (1.0 - lam_init)
            o_ref[:, h * B_VDIM:(h + 1) * B_VDIM] = y.astype(BF16)


def _diff_attention(qb, kb, vb, lq1, lk1, lq2, lk2, subln_g, lam_init):
    _, b, seq, _ = qb.shape
    tq = min(512, seq)
    tk = min(1024, seq)
    span = min(2048, seq)
    n_steps = seq // span
    assert tk % tq == 0 and span // tk >= 2
    nh = 2 * B_HEADS
    slopes = jnp.asarray([2.0 ** (-8.0 * (h + 1) / B_HEADS) * LOG2E for h in range(B_HEADS)], F32)
    vec = pl.BlockSpec((1, HEAD_DIM), lambda bi, qi, si, *_: (0, 0))
    return pl.pallas_call(
        functools.partial(_diff_kernel, tq=tq, tk=tk, span=span, n_steps=n_steps, lam_init=lam_init),
        grid_spec=pltpu.PrefetchScalarGridSpec(
            num_scalar_prefetch=1,
            grid=(b, seq // tq, n_steps),
            in_specs=[pl.BlockSpec((nh, None, tq, LANES), lambda bi, qi, si, *_: (0, bi, qi, 0)),
                      pl.BlockSpec((nh, None, span, LANES), lambda bi, qi, si, *_: (0, bi, si, 0)),
                      pl.BlockSpec((B_HEADS, None, span, 2 * B_VDIM), lambda bi, qi, si, *_: (0, bi, si, 0)),
                      vec, vec, vec, vec, pl.BlockSpec((1, B_VDIM), lambda bi, qi, si, *_: (0, 0))],
            out_specs=pl.BlockSpec((None, tq, B_WIDTH), lambda bi, qi, si, *_: (bi, qi, 0)),
            scratch_shapes=[pltpu.VMEM((PIPE_SLOTS, 2, tq, tk), F32), pltpu.VMEM((PIPE_SLOTS, 2, tq, tk), BF16),
                            pltpu.VMEM((PIPE_SLOTS, 2, tq, LANES), F32),
                            pltpu.VMEM((nh, tq, LANES), F32), pltpu.VMEM((nh, tq, 2 * B_VDIM), F32)],
        ),
        out_shape=jax.ShapeDtypeStruct((b, seq, B_WIDTH), BF16),
        compiler_params=_params("parallel", "parallel", "arbitrary"),
        name="diff_attn",
    )(slopes, qb, kb, vb, lq1, lk1, lq2, lk2, subln_g)


def _merge_kernel(ya_ref, yb_ref, gate_ref, h_ref, p_ref, wa_ref, wb_ref, wo_ref, g_ref, b_ref,
                  wrh_ref, wrl_ref, wpg_ref, wpp_ref, x1_ref, base_ref, aff_ref):
    ma = jnp.dot(ya_ref[...], wa_ref[...], preferred_element_type=F32)
    mb = jnp.dot(yb_ref[...], wb_ref[...], preferred_element_type=F32)
    merged = _sigmoid(gate_ref[:, 0:D_MODEL]) * ma + _sigmoid(gate_ref[:, D_MODEL:2 * D_MODEL]) * mb
    out = jnp.dot(merged.astype(BF16), wo_ref[...], preferred_element_type=F32)
    x1 = _layer_norm(ALPHA * h_ref[...] + out, g_ref[...], b_ref[...])
    x1_ref[...] = x1
    xh, xl = _split_bf16(x1)
    ple_gate = _sigmoid(jnp.dot(xh, wpg_ref[...], preferred_element_type=F32))
    base_ref[...] = ALPHA * x1 + ple_gate * jnp.dot(p_ref[...].astype(BF16), wpp_ref[...], preferred_element_type=F32)
    wh = wrh_ref[...]
    logits = (lax.dot_general(wh, xh, NT_DIMS, preferred_element_type=F32)
              + lax.dot_general(wh, xl, NT_DIMS, preferred_element_type=F32)
              + lax.dot_general(wrl_ref[...], xh, NT_DIMS, preferred_element_type=F32))
    e = jnp.exp(logits - jnp.max(logits, axis=0, keepdims=True))
    aff_ref[...] = e / jnp.sum(e, axis=0, keepdims=True)


def _merge(ya, yb, gates, h, p2, wa, wb, wo, g, b, wrh, wrl, wpg, wpp):
    n = h.shape[0]
    tm = ROW_TILE

    def rows(w):
        return pl.BlockSpec((tm, w), lambda i: (i, 0))

    return pl.pallas_call(
        _merge_kernel,
        grid=(n // tm,),
        in_specs=[rows(A_WIDTH), rows(B_WIDTH), rows(2 * D_MODEL), rows(D_MODEL), rows(PLE_DIM),
                  _full((A_WIDTH, D_MODEL)), _full((B_WIDTH, D_MODEL)), _full((D_MODEL, D_MODEL)),
                  _full((1, D_MODEL)), _full((1, D_MODEL)),
                  _full((N_EXPERTS, D_MODEL)), _full((N_EXPERTS, D_MODEL)),
                  _full((D_MODEL, D_MODEL)), _full((PLE_DIM, D_MODEL))],
        out_specs=[rows(D_MODEL), rows(D_MODEL), pl.BlockSpec((N_EXPERTS, tm), lambda i: (0, i))],
        out_shape=[jax.ShapeDtypeStruct((n, D_MODEL), F32), jax.ShapeDtypeStruct((n, D_MODEL), F32),
                   jax.ShapeDtypeStruct((N_EXPERTS, n), F32)],
        compiler_params=_params("parallel"),
        name="merge",
    )(ya, yb, gates, h, p2, wa, wb, wo, g, b, wrh, wrl, wpg, wpp)


def _route_kernel(aff_ref, rank_ref, idx_ref, blkp_ref, sel_scr, wi_scr, p_scr, *, nb, cap):
    ne = N_EXPERTS
    shape3 = (ne, nb, LANES)
    bits = pltpu.bitcast(aff_ref[...], I32).reshape(shape3)
    capf = float(cap)

    def count(mask3):
        part = jnp.sum(jnp.where(mask3, 1.0, 0.0), axis=1, keepdims=True)
        return jnp.sum(part, axis=2, keepdims=True)

    def search(i, thr):
        cand = thr | jnp.left_shift(jnp.int32(1), 30 - i)
        return jnp.where(count(bits >= cand) >= capf, cand, thr)

    thr = lax.fori_loop(0, 31, search, jnp.zeros((ne, 1, 1), I32))
    gt = bits > thr
    eq = bits == thr
    need = capf - count(gt)

    li = lax.broadcasted_iota(I32, (LANES, LANES), 0)
    lj = lax.broadcasted_iota(I32, (LANES, LANES), 1)
    upper_incl = jnp.where(li <= lj, 1.0, 0.0).astype(BF16)
    lower_incl = jnp.where(lj <= li, 1.0, 0.0).astype(BF16)
    bi = lax.broadcasted_iota(I32, (nb, nb), 0)
    bj = lax.broadcasted_iota(I32, (nb, nb), 1)
    lower_strict = jnp.where(bj < bi, 1.0, 0.0).astype(BF16)
    upper_strict = jnp.where(bi < bj, 1.0, 0.0).astype(BF16)

    def prefix(mask2):
        wi = jnp.dot(mask2.astype(BF16), upper_incl, preferred_element_type=F32)
        tot = jnp.broadcast_to(wi[:, LANES - 1:LANES], wi.shape).astype(BF16)
        blk = [jnp.dot(lower_strict, tot[e * nb:(e + 1) * nb], preferred_element_type=F32)
               for e in range(ne)]
        return wi, jnp.concatenate(blk, axis=0)

    eqf = jnp.where(eq, 1.0, 0.0).reshape(ne * nb, LANES)
    wi, bp = prefix(eqf)
    eq_rank = (wi + bp - eqf).reshape(shape3)
    sel = gt | (eq & (eq_rank < need))
    self_ = jnp.where(sel, 1.0, 0.0).reshape(ne * nb, LANES)
    wi, bp = prefix(self_)
    rank = wi + bp - self_
    rank_ref[...] = jnp.where(self_ > 0.0, rank, -1.0).astype(I32)
    sel_scr[...] = self_
    wi_scr[...] = wi
    p_scr[...] = bp

    chunk = min(cap, 1024)
    ones8 = jnp.ones((8, LANES), BF16)
    b_col = lax.broadcasted_iota(I32, (nb, 1), 0).astype(F32)

    def per_expert(e, carry):
        r0 = pl.multiple_of(e * nb, 8)
        sel_e = sel_scr[pl.ds(r0, nb), :].astype(BF16)
        wi_e = wi_scr[pl.ds(r0, nb), :]
        p_e = p_scr[pl.ds(r0, nb), :][:, 0:1]
        p_end = p_e + wi_e[:, LANES - 1:LANES]
        wi_t = lax.dot_general(lower_incl, sel_e, NT_DIMS, preferred_element_type=F32).astype(BF16)
        tot_row = lax.dot_general(ones8, sel_e, NT_DIMS, preferred_element_type=F32)
        p_row = jnp.dot(tot_row.astype(BF16), upper_strict, preferred_element_type=F32)
        blkp_ref[e] = p_row[0:1, :].astype(I32)
        for c in range(cap // chunk):
            slot = (lax.broadcasted_iota(I32, (1, chunk), 1) + c * chunk).astype(F32)
            hot = (p_e <= slot) & (slot < p_end)
            w_slot = jnp.dot(wi_t, jnp.where(hot, 1.0, 0.0).astype(BF16), preferred_element_type=F32)
            p_slot = jnp.sum(jnp.where(hot, p_e, 0.0), axis=0, keepdims=True)
            b_slot = jnp.sum(jnp.where(hot, b_col, 0.0), axis=0, keepdims=True)
            lane = jnp.sum(jnp.where(w_slot <= slot - p_slot, 1.0, 0.0), axis=0, keepdims=True)
            idx_ref[e, :, c * chunk:(c + 1) * chunk] = (b_slot * LANES + lane).astype(I32)
        return carry

    lax.fori_loop(0, ne, per_expert, 0)


def _route(aff2, n, cap):
    nb = n // LANES
    return pl.pallas_call(
        functools.partial(_route_kernel, nb=nb, cap=cap),
        out_shape=[jax.ShapeDtypeStruct((N_EXPERTS * nb, LANES), I32),
                   jax.ShapeDtypeStruct((N_EXPERTS, 1, cap), I32),
                   jax.ShapeDtypeStruct((N_EXPERTS, 1, nb), I32)],
        scratch_shapes=[pltpu.VMEM((N_EXPERTS * nb, LANES), F32)] * 3,
        compiler_params=pltpu.CompilerParams(vmem_limit_bytes=VMEM_LIMIT),
        name="route",
    )(aff2)


def _ffn_kernel(idx_ref, x_hbm, wg_ref, wu_ref, wd_ref, y_ref, xbuf, sem, *, tm, tiles, n_steps):
    step = pl.program_id(0) * tiles + pl.program_id(1)
    slot = step & 1

    def fetch(src_step, dst_slot):
        for r in range(tm):
            pltpu.make_async_copy(x_hbm.at[pl.ds(idx_ref[src_step * tm + r], 1), :],
                                  xbuf.at[dst_slot, pl.ds(r, 1), :], sem.at[dst_slot]).start()

    def wait(which):
        pltpu.make_async_copy(x_hbm.at[pl.ds(0, tm), :], xbuf.at[which], sem.at[which]).wait()

    @pl.when(step == 0)
    def _():
        fetch(0, 0)

    fetch(jnp.minimum(step + 1, n_steps - 1), 1 - slot)
    wait(slot)
    x = xbuf[slot].astype(BF16)
    fc = 512
    acc = None
    for c in range(EXPERT_FF // fc):
        sl = slice(c * fc, (c + 1) * fc)
        a = jnp.dot(x, wg_ref[:, sl], preferred_element_type=F32)
        u = jnp.dot(x, wu_ref[:, sl], preferred_element_type=F32)
        mid = (a * _sigmoid(a) * u).astype(BF16)
        part = jnp.dot(mid, wd_ref[sl, :], preferred_element_type=F32)
        acc = part if acc is None else acc + part
    y_ref[...] = acc.astype(BF16)

    @pl.when(step == n_steps - 1)
    def _():
        wait(1 - slot)


def _ffn(idx_flat, x1, wg, wu, wd, cap):
    tm = min(512, cap)
    tiles = cap // tm
    return pl.pallas_call(
        functools.partial(_ffn_kernel, tm=tm, tiles=tiles, n_steps=N_EXPERTS * tiles),
        grid_spec=pltpu.PrefetchScalarGridSpec(
            num_scalar_prefetch=1,
            grid=(N_EXPERTS, tiles),
            in_specs=[pl.BlockSpec(memory_space=pl.ANY),
                      pl.BlockSpec((None, D_MODEL, EXPERT_FF), lambda e, i, idx: (e, 0, 0)),
                      pl.BlockSpec((None, D_MODEL, EXPERT_FF), lambda e, i, idx: (e, 0, 0)),
                      pl.BlockSpec((None, EXPERT_FF, D_MODEL), lambda e, i, idx: (e, 0, 0))],
            out_specs=pl.BlockSpec((tm, D_MODEL), lambda e, i, idx: (e * tiles + i, 0)),
            scratch_shapes=[pltpu.VMEM((2, tm, D_MODEL), F32), pltpu.SemaphoreType.DMA((2,))],
        ),
        out_shape=jax.ShapeDtypeStruct((N_EXPERTS * cap, D_MODEL), BF16),
        compiler_params=_params("arbitrary", "arbitrary"),
        name="ffn",
    )(idx_flat, x1, wg, wu, wd)


TOK_TILE = 128
ROW_ALIGN = 16
WIN = TOK_TILE + ROW_ALIGN


def _final_kernel(start_ref, base_ref, rank_ref, aff_ref, g_ref, b_ref, ye_hbm, o_ref, buf, sem, *, cap, n_steps):
    t = pl.program_id(0)
    slot = t & 1

    def window(step, e):
        first = start_ref[step * N_EXPERTS + e]
        ws = jnp.minimum((first // ROW_ALIGN) * ROW_ALIGN, cap - WIN)
        return pl.multiple_of(ws, ROW_ALIGN)

    def fetch(step, dst_slot):
        for e in range(N_EXPERTS):
            pltpu.make_async_copy(ye_hbm.at[pl.ds(e * cap + window(step, e), WIN), :],
                                  buf.at[dst_slot, pl.ds(e * WIN, WIN), :], sem.at[dst_slot]).start()

    @pl.when(t == 0)
    def _():
        fetch(0, 0)

    @pl.when(t + 1 < n_steps)
    def _():
        fetch(t + 1, 1 - slot)

    total = base_ref[...]
    pltpu.make_async_copy(ye_hbm.at[pl.ds(0, N_EXPERTS * WIN), :], buf.at[slot], sem.at[slot]).wait()
    lane = lax.broadcasted_iota(I32, (1, WIN), 1)
    for e in range(N_EXPERTS):
        hot = rank_ref[:, e:e + 1] == lane + window(t, e)
        y = jnp.dot(jnp.where(hot, 1.0, 0.0).astype(BF16), buf[slot, e * WIN:(e + 1) * WIN, :],
                    preferred_element_type=F32)
        total = total + aff_ref[:, e:e + 1] * y

    o_ref[...] = _layer_norm(total, g_ref[...], b_ref[...])


def _final(starts, base, rank_t, aff_t, g, b, ye, cap):
    n = base.shape[0]
    tm = TOK_TILE

    def rows(w):
        return pl.BlockSpec((tm, w), lambda i, *_: (i, 0))

    def full(shape):
        return pl.BlockSpec(shape, lambda i, *_: (0,) * len(shape))

    return pl.pallas_call(
        functools.partial(_final_kernel, cap=cap, n_steps=n // tm),
        grid_spec=pltpu.PrefetchScalarGridSpec(
            num_scalar_prefetch=1,
            grid=(n // tm,),
            in_specs=[rows(D_MODEL), rows(N_EXPERTS), rows(N_EXPERTS), full((1, D_MODEL)), full((1, D_MODEL)),
                      pl.BlockSpec(memory_space=pl.ANY)],
            out_specs=rows(D_MODEL),
            scratch_shapes=[pltpu.VMEM((2, N_EXPERTS * WIN, D_MODEL), BF16), pltpu.SemaphoreType.DMA((2,))],
        ),
        out_shape=jax.ShapeDtypeStruct((n, D_MODEL), F32),
        compiler_params=_params("arbitrary"),
        name="final",
    )(starts, base, rank_t, aff_t, g, b, ye)


def _rope_tables(seq):
    n_rows = seq // GRID_W
    rows = jnp.repeat(jnp.arange(n_rows, dtype=F32), GRID_W)
    cols = jnp.tile(jnp.arange(GRID_W, dtype=F32), n_rows)
    sec = HEAD_DIM // 2
    inv_freq = ROPE_THETA ** (-jnp.arange(0, sec, 2, dtype=F32) / sec)
    ang_r = rows[:, None] * inv_freq[None, :]
    ang_c = cols[:, None] * inv_freq[None, :]
    cos = jnp.concatenate([jnp.cos(ang_r)] * 2 + [jnp.cos(ang_c)] * 2, axis=-1)
    sin = jnp.concatenate([-jnp.sin(ang_r), jnp.sin(ang_r), -jnp.sin(ang_c), jnp.sin(ang_c)], axis=-1)
    return jnp.tile(cos, (1, 2)), jnp.tile(sin, (1, 2))


def _trunk(x, p, w):
    b, seq, _ = x.shape
    n = b * seq
    cap = max(1, CAPACITY_FACTOR * n // N_EXPERTS)
    assert seq % ROW_TILE == 0 and n % (8 * LANES) == 0 and cap % TOK_TILE == 0 and cap >= WIN
    cos_t, sin_t = _rope_tables(seq)

    h, qa, ka, va, qb, kb, vb, gates = _in_proj(
        x.reshape(n, D_MODEL), w["ln_emb_g"], w["ln_emb_b"], w["w_in"], w["q_norm_g"], w["k_norm_g"],
        cos_t, sin_t, w["seg"], seq)

    def per_batch(t):
        return t.reshape(t.shape[0], b, seq, t.shape[2])

    ya = _gqa_attention(per_batch(qa), per_batch(ka), per_batch(va))
    lam_init = 0.8 - 0.6 * math.exp(-0.3 * 0)
    yb = _diff_attention(per_batch(qb), per_batch(kb), per_batch(vb),
                         w["lambda_q1"], w["lambda_k1"], w["lambda_q2"], w["lambda_k2"], w["subln_g"], lam_init)

    x1, base, aff = _merge(ya.reshape(n, -1), yb.reshape(n, -1), gates, h, p.reshape(n, PLE_DIM),
                           w["w_branch_a"], w["w_branch_b"], w["w_out"], w["ln1_g"], w["ln1_b"],
                           w["wr_hi"], w["wr_lo"], w["w_ple_gate"], w["w_ple_proj"])

    nb = n // LANES
    rank2, idx, blkp = _route(aff.reshape(N_EXPERTS * nb, LANES), n, cap)
    ye = _ffn(idx.reshape(N_EXPERTS * cap), x1, w["w_gate"], w["w_up"], w["w_down"], cap)

    starts = blkp.reshape(N_EXPERTS, nb)[:, ::TOK_TILE // LANES]
    out = _final(starts.T.reshape(-1), base, rank2.reshape(N_EXPERTS, n).T, aff.T, w["ln2_g"], w["ln2_b"], ye, cap)
    return out.reshape(b, seq, D_MODEL)


def _prep_weights(ln_emb_g, ln_emb_b, w_in, q_norm_g, k_norm_g, lambda_q1, lambda_k1, lambda_q2, lambda_k2,
                  subln_g, w_branch_a, w_branch_b, w_out, ln1_g, ln1_b, w_router, w_gate, w_up, w_down,
                  w_ple_gate, w_ple_proj, ln2_g, ln2_b):
    i = 0
    seg_i = lax.broadcasted_iota(I32, (A_WIDTH, A_WIDTH), 0) // HEAD_DIM
    seg_j = lax.broadcasted_iota(I32, (A_WIDTH, A_WIDTH), 1) // HEAD_DIM
    wr_t = w_router[i].T
    wr_hi = wr_t.astype(BF16)
    return {
        "ln_emb_g": ln_emb_g.reshape(1, D_MODEL), "ln_emb_b": ln_emb_b.reshape(1, D_MODEL),
        "w_in": w_in[i].astype(BF16),
        "q_norm_g": jnp.tile(q_norm_g[i], 2).reshape(1, LANES), "k_norm_g": jnp.tile(k_norm_g[i], 2).reshape(1, LANES),
        "seg": (seg_i == seg_j).astype(BF16),
        "lambda_q1": lambda_q1[i].reshape(1, HEAD_DIM), "lambda_k1": lambda_k1[i].reshape(1, HEAD_DIM),
        "lambda_q2": lambda_q2[i].reshape(1, HEAD_DIM), "lambda_k2": lambda_k2[i].reshape(1, HEAD_DIM),
        "subln_g": subln_g[i].reshape(1, B_VDIM),
        "w_branch_a": w_branch_a[i].astype(BF16), "w_branch_b": w_branch_b[i].astype(BF16),
        "w_out": w_out[i].astype(BF16),
        "ln1_g": ln1_g[i].reshape(1, D_MODEL), "ln1_b": ln1_b[i].reshape(1, D_MODEL),
        "wr_hi": wr_hi, "wr_lo": (wr_t - wr_hi.astype(F32)).astype(BF16),
        "w_gate": w_gate[i].astype(BF16), "w_up": w_up[i].astype(BF16), "w_down": w_down[i].astype(BF16),
        "w_ple_gate": w_ple_gate[i].astype(BF16), "w_ple_proj": w_ple_proj[i].astype(BF16),
        "ln2_g": ln2_g[i].reshape(1, D_MODEL), "ln2_b": ln2_b[i].reshape(1, D_MODEL),
    }


def kernel(x_prompt, x_sample, p_prompt, p_sample, ln_emb_g, ln_emb_b, w_in, q_norm_g, k_norm_g, lambda_q1, lambda_k1, lambda_q2, lambda_k2, subln_g, w_branch_a, w_branch_b, w_out, ln1_g, ln1_b, w_router, w_gate, w_up, w_down, w_ple_gate, w_ple_proj, ln2_g, ln2_b):
    w = _prep_weights(ln_emb_g, ln_emb_b, w_in, q_norm_g, k_norm_g, lambda_q1, lambda_k1, lambda_q2, lambda_k2,
                      subln_g, w_branch_a, w_branch_b, w_out, ln1_g, ln1_b, w_router, w_gate, w_up, w_down,
                      w_ple_gate, w_ple_proj, ln2_g, ln2_b)
    return (_trunk(x_prompt, p_prompt[0], w), _trunk(x_sample, p_sample[0], w))
```

```python
import functools
import math

import jax
import jax.numpy as jnp
from jax import lax
from jax.experimental import pallas as pl
from jax.experimental.pallas import tpu as pltpu

F32 = jnp.float32
BF16 = jnp.bfloat16
I32 = jnp.int32

D_MODEL = 1024
DEPTH = 1
HEAD_DIM = 64
A_HEADS = 8
A_KV_HEADS = 2
A_GROUP = A_HEADS // A_KV_HEADS
A_WIDTH = A_HEADS * HEAD_DIM
B_HEADS = 4
B_VDIM = 2 * HEAD_DIM
B_WIDTH = B_HEADS * B_VDIM
GRID_W = 64
ROPE_THETA = 10000.0
N_EXPERTS = 16
EXPERT_FF = 2048
CAPACITY_FACTOR = 2
PLE_DIM = 256
LN_EPS = 1e-5
RMS_EPS = 1e-6
ALPHA = (2 * DEPTH) ** 0.25
LOG2E = math.log2(math.e)
QSCALE = HEAD_DIM ** -0.5 * LOG2E
LANES = 128
ROPE_HALF = HEAD_DIM // 4

C_QA = 0
C_KA = C_QA + A_WIDTH
C_VA = C_KA + A_KV_HEADS * HEAD_DIM
C_QB = C_VA + A_KV_HEADS * HEAD_DIM
C_KB = C_QB + 2 * B_HEADS * HEAD_DIM
C_VB = C_KB + 2 * B_HEADS * HEAD_DIM
C_G = C_VB + B_WIDTH
IN_COLS = C_G + 2 * D_MODEL

VMEM_LIMIT = 56 * 1024 * 1024
MXU_DEPTH = 256

ROW_TILE = 512
GQA_TQ, GQA_TK = 256, 1024
DIFF_TQ, DIFF_TK, DIFF_SPAN = 512, 1024, 2048
NT_DIMS = (((1,), (1,)), ((), ()))


def _params(*sem):
    return pltpu.CompilerParams(dimension_semantics=sem, vmem_limit_bytes=VMEM_LIMIT)


def _full(shape):
    n = len(shape)
    return pl.BlockSpec(shape, lambda *_: (0,) * n, pipeline_mode=pl.Buffered(1))


def _layer_norm(x, g, b):
    mu = jnp.mean(x, axis=-1, keepdims=True)
    xc = x - mu
    var = jnp.mean(xc * xc, axis=-1, keepdims=True)
    return xc * lax.rsqrt(var + LN_EPS) * g + b


def _sigmoid(x):
    return 1.0 / (1.0 + jnp.exp(-x))


def _split_bf16(x):
    hi = x.astype(BF16)
    lo = (x - hi.astype(F32)).astype(BF16)
    return hi, lo


def _head_mean_sq(x, seg):
    hi, lo = _split_bf16(x * x)
    s = jnp.dot(hi, seg, preferred_element_type=F32) + jnp.dot(lo, seg, preferred_element_type=F32)
    return s * (1.0 / HEAD_DIM)


def _rope(x, cos, sin, first_half):
    fwd = pltpu.roll(x, LANES - ROPE_HALF, 1)
    bwd = pltpu.roll(x, ROPE_HALF, 1)
    return x * cos + jnp.where(first_half, fwd, bwd) * sin


def _in_proj_kernel(x_ref, lng_ref, lnb_ref, w_ref, qg_ref, kg_ref, cos_ref, sin_ref, seg_ref, qc_ref, kc_ref,
                    h_ref, qa_ref, ka_ref, va_ref, qb_ref, kb_ref, vb_ref, gate_ref, *, tiles_per_seq):
    xn = _layer_norm(x_ref[...], lng_ref[...], lnb_ref[...])
    h_ref[...] = xn
    xb = xn.astype(BF16)

    def proj(lo, hi):
        return jnp.dot(xb, w_ref[:, lo:hi], preferred_element_type=F32)

    def store_heads(ref, x):
        for i in range(x.shape[1] // HEAD_DIM):
            ref[i] = x[:, i * HEAD_DIM:(i + 1) * HEAD_DIM].astype(BF16)

    cos = cos_ref[...]
    sin = sin_ref[...]
    lane = lax.broadcasted_iota(I32, cos.shape, 1)
    first_half = (lane & ROPE_HALF) == 0

    qa = proj(C_QA, C_KA)
    qn = qa * lax.rsqrt(_head_mean_sq(qa, seg_ref[...]) + RMS_EPS)
    for c in range(A_WIDTH // LANES):
        pair = _rope(qn[:, c * LANES:(c + 1) * LANES] * qg_ref[...], cos, sin, first_half) * QSCALE
        qa_ref[2 * c] = pair[:, 0:HEAD_DIM].astype(BF16)
        qa_ref[2 * c + 1] = pair[:, HEAD_DIM:LANES].astype(BF16)

    ka = proj(C_KA, C_VA)
    kn = ka * lax.rsqrt(_head_mean_sq(ka, seg_ref[0:LANES, 0:LANES]) + RMS_EPS)
    store_heads(ka_ref, _rope(kn * kg_ref[...], cos, sin, first_half))
    one_at_0 = jnp.where(lane == 0, 1.0, 0.0)
    va = proj(C_VA, C_QB)
    for g in range(A_KV_HEADS):
        v_first = va if g == 0 else pltpu.roll(va, HEAD_DIM, 1)
        va_ref[g] = jnp.where(lane < HEAD_DIM, v_first, jnp.where(lane == HEAD_DIM, 1.0, 0.0)).astype(BF16)
    pos = (pl.program_id(0) % tiles_per_seq) * cos.shape[0] + lax.broadcasted_iota(I32, cos.shape, 0)
    hi = (pos >> 7).astype(F32)
    lo = (pos & (LANES - 1)).astype(F32)
    pair_of = lane >> 1
    q_pos = jnp.where(pair_of == 32, hi, jnp.where(pair_of == 33, lo, 0.0))
    k_pos = jnp.where(pair_of == 34, hi, jnp.where(pair_of == 35, lo, 0.0))

    def store_with_features(ref, x, pos_feat, const_ref):
        for i in range(2 * B_HEADS):
            chunk = x[:, (i // 2) * LANES:(i // 2 + 1) * LANES]
            first = chunk if i % 2 == 0 else pltpu.roll(chunk, HEAD_DIM, 1)
            feat = pos_feat + const_ref[i % B_HEADS:i % B_HEADS + 1, :]
            ref[i] = jnp.where(lane < HEAD_DIM, first, feat).astype(BF16)

    store_with_features(qb_ref, proj(C_QB, C_KB) * QSCALE, q_pos, qc_ref)
    store_with_features(kb_ref, proj(C_KB, C_VB), k_pos, kc_ref)
    vb = proj(C_VB, C_G)
    for i in range(B_HEADS):
        vb_ref[i, :, 0:B_VDIM] = vb[:, i * B_VDIM:(i + 1) * B_VDIM].astype(BF16)
        vb_ref[i, :, B_VDIM:2 * B_VDIM] = one_at_0.astype(BF16)
    gate_ref[...] = proj(C_G, IN_COLS)


def _alibi_feature_constants():
    c = jnp.asarray([2.0 ** (-8.0 * (h + 1) / B_HEADS) * LOG2E for h in range(B_HEADS)], F32)
    c1 = c.astype(BF16).astype(F32)
    c2 = (c - c1).astype(BF16).astype(F32)
    parts = jnp.stack([LANES * c1, LANES * c2, c1, c2], axis=1)
    zeros = jnp.zeros((B_HEADS, LANES), F32)
    return (zeros.at[:, HEAD_DIM + 4:HEAD_DIM + 8].set(-parts), zeros.at[:, HEAD_DIM:HEAD_DIM + 4].set(parts))


def _in_proj(x2, lng, lnb, w_in, qg, kg, cos_t, sin_t, seg, seq):
    n = x2.shape[0]
    tm = ROW_TILE
    tiles_per_seq = seq // tm

    def rows(w):
        return pl.BlockSpec((tm, w), lambda i: (i, 0))

    def heads(nh, w):
        return pl.BlockSpec((nh, tm, w), lambda i: (0, i, 0))

    tab = pl.BlockSpec((tm, LANES), lambda i: (i % tiles_per_seq, 0))
    head_outs = [(A_HEADS, HEAD_DIM), (A_KV_HEADS, HEAD_DIM), (A_KV_HEADS, LANES),
                 (2 * B_HEADS, LANES), (2 * B_HEADS, LANES), (B_HEADS, 2 * B_VDIM)]
    q_const, k_const = _alibi_feature_constants()
    return pl.pallas_call(
        functools.partial(_in_proj_kernel, tiles_per_seq=tiles_per_seq),
        grid=(n // tm,),
        in_specs=[rows(D_MODEL), _full((1, D_MODEL)), _full((1, D_MODEL)), _full((D_MODEL, IN_COLS)),
                  _full((1, LANES)), _full((1, LANES)), tab, tab, _full((A_WIDTH, A_WIDTH)),
                  _full((B_HEADS, LANES)), _full((B_HEADS, LANES))],
        out_specs=[rows(D_MODEL)] + [heads(nh, w) for nh, w in head_outs] + [rows(2 * D_MODEL)],
        out_shape=([jax.ShapeDtypeStruct((n, D_MODEL), F32)]
                   + [jax.ShapeDtypeStruct((nh, n, w), BF16) for nh, w in head_outs]
                   + [jax.ShapeDtypeStruct((n, 2 * D_MODEL), F32)]),
        compiler_params=_params("parallel"),
        name="in_proj",
    )(x2, lng, lnb, w_in, qg, kg, cos_t, sin_t, seg, q_const, k_const)


PIPE_SLOTS = 2


def _run_units(n_units, scores, softmax, values):
    scores(0, 0)
    scores(1, 1)
    softmax(0, 0)

    def two_units(i, carry):
        t = PIPE_SLOTS * (i + 1)
        scores(t, 0)
        softmax(t - 1, 1)
        values(t - 2, 0)
        scores(t + 1, 1)
        softmax(t, 0)
        values(t - 1, 1)
        return carry

    lax.fori_loop(0, (n_units >> 1) - 1, two_units, 0)
    softmax(n_units - 1, 1)
    values(n_units - 2, 0)
    values(n_units - 1, 1)


def _softmax_update(load_s, m_ref, i):
    m_prev = m_ref[i]
    m_new = jnp.maximum(m_prev, jnp.max(load_s(), axis=1, keepdims=True))
    alpha = jnp.exp2(m_prev - m_new)
    s = load_s()
    p = jnp.exp2(s - jnp.tile(m_new, (1, s.shape[1] // LANES)))
    m_ref[i] = m_new
    return p.astype(BF16), alpha


def _init_softmax_state(m_ref, acc_ref):
    m_ref[...] = jnp.full(m_ref.shape, -jnp.inf, F32)
    acc_ref[...] = jnp.zeros(acc_ref.shape, F32)


def _gqa_kernel(q_ref, k_ref, v_ref, o_ref, s_buf, p_buf, a_buf, m_ref, acc_ref, *, tq, tk, n_sub, n_steps):
    step = pl.program_id(2)

    @pl.when(step == 0)
    def _():
        _init_softmax_state(m_ref, acc_ref)

    def keys(u):
        return u & (A_KV_HEADS - 1), pl.ds(pl.multiple_of((u >> 1) * tk, tk), tk)

    def scores(u, slot):
        g, rows = keys(u)
        q = q_ref[pl.ds(g * A_GROUP, A_GROUP)].reshape(A_GROUP * tq, HEAD_DIM)
        s_buf[slot] = lax.dot_general(q, k_ref[g, rows, :], NT_DIMS, preferred_element_type=F32)

    def softmax(u, slot):
        p, alpha = _softmax_update(lambda: s_buf[slot], m_ref, u & (A_KV_HEADS - 1))
        p_buf[slot] = p
        a_buf[slot] = alpha

    def values(u, slot):
        g, rows = keys(u)
        pv = jnp.dot(p_buf[slot], v_ref[g, rows, :], preferred_element_type=F32)
        acc_ref[g] = a_buf[slot] * acc_ref[g] + pv

    _run_units(n_sub * A_KV_HEADS, scores, softmax, values)

    @pl.when(step == n_steps - 1)
    def _():
        for h in range(A_HEADS):
            g, r = divmod(h, A_GROUP)
            acc = acc_ref[g, r * tq:(r + 1) * tq, :]
            y = acc[:, 0:HEAD_DIM] / acc[:, HEAD_DIM:HEAD_DIM + 1]
            o_ref[:, h * HEAD_DIM:(h + 1) * HEAD_DIM] = y.astype(BF16)


def _gqa_attention(qa, ka, va):
    _, b, seq, _ = qa.shape
    tq = min(GQA_TQ, seq)
    tk = min(GQA_TK, seq)
    span = seq
    n_steps = seq // span
    rows = A_GROUP * tq
    resident = pl.Buffered(1)
    return pl.pallas_call(
        functools.partial(_gqa_kernel, tq=tq, tk=tk, n_sub=span // tk, n_steps=n_steps),
        grid=(b, seq // tq, n_steps),
        in_specs=[pl.BlockSpec((A_HEADS, None, tq, HEAD_DIM), lambda bi, qi, si: (0, bi, qi, 0)),
                  pl.BlockSpec((A_KV_HEADS, None, span, HEAD_DIM), lambda bi, qi, si: (0, bi, si, 0),
                               pipeline_mode=resident),
                  pl.BlockSpec((A_KV_HEADS, None, span, LANES), lambda bi, qi, si: (0, bi, si, 0),
                               pipeline_mode=resident)],
        out_specs=pl.BlockSpec((None, tq, A_WIDTH), lambda bi, qi, si: (bi, qi, 0)),
        out_shape=jax.ShapeDtypeStruct((b, seq, A_WIDTH), BF16),
        scratch_shapes=[pltpu.VMEM((PIPE_SLOTS, rows, tk), F32), pltpu.VMEM((PIPE_SLOTS, rows, tk), BF16),
                        pltpu.VMEM((PIPE_SLOTS, rows, LANES), F32),
                        pltpu.VMEM((A_KV_HEADS, rows, LANES), F32), pltpu.VMEM((A_KV_HEADS, rows, LANES), F32)],
        compiler_params=_params("parallel", "parallel", "arbitrary"),
        name="gqa_attn",
    )(qa, ka, va)


def _diff_kernel(slope_ref, q_ref, k_ref, v_ref, lq1_ref, lk1_ref, lq2_ref, lk2_ref, g_ref, o_ref,
                 s_buf, p_buf, a_buf, m_ref, acc_ref, *, tq, tk, span, n_steps, lam_init):
    qi = pl.program_id(1)
    step = pl.program_id(2)
    n_sub = span // tk
    q_pos = qi * tq
    diag_step = q_pos // span
    diag_sub = (q_pos // tk) % n_sub
    lane = lax.broadcasted_iota(I32, (1, LANES), 1)

    @pl.when(step == 0)
    def _():
        _init_softmax_state(m_ref, acc_ref)

    @pl.when(step == diag_step)
    def _():
        rows = pl.ds(pl.multiple_of(diag_sub * tk, tk), tk)
        offset = q_pos - (q_pos // tk) * tk
        dist = jnp.abs(lax.broadcasted_iota(I32, (tq, tk), 0) - lax.broadcasted_iota(I32, (tq, tk), 1) + offset)
        dist = dist.astype(F32)
        for h in range(B_HEADS):
            bias = dist * slope_ref[h]
            v = v_ref[h, rows, :]
            for m in range(2):
                i = m * B_HEADS + h
                s = lax.dot_general(q_ref[i][:, 0:HEAD_DIM], k_ref[i, rows, 0:HEAD_DIM], NT_DIMS,
                                    preferred_element_type=F32) - bias
                p, alpha = _softmax_update(lambda: s, m_ref, i)
                acc_ref[i] = jnp.tile(alpha, (1, 2)) * acc_ref[i] + jnp.dot(p, v, preferred_element_type=F32)

    def keys(u):
        j = u >> 2
        j = j + jnp.where((step == diag_step) & (j >= diag_sub), 1, 0)
        return u & (B_HEADS - 1), j, pl.ds(pl.multiple_of(j * tk, tk), tk)

    def scores(u, slot):
        h, j, rows = keys(u)
        keys_before = q_pos >= step * span + (j + 1) * tk
        sign = jnp.where(keys_before, -1.0, 1.0)
        lane_scale = jnp.where(lane < HEAD_DIM, 1.0, sign).astype(BF16)
        for m in range(2):
            i = m * B_HEADS + h
            s_buf[slot, m] = lax.dot_general(q_ref[i] * lane_scale, k_ref[i, rows, :], NT_DIMS,
                                             preferred_element_type=F32)

    def softmax(u, slot):
        h = u & (B_HEADS - 1)
        for m in range(2):
            p, alpha = _softmax_update(lambda: s_buf[slot, m], m_ref, m * B_HEADS + h)
            p_buf[slot, m] = p
            a_buf[slot, m] = alpha

    def values(u, slot):
        h, _, rows = keys(u)
        v = v_ref[h, rows, :]
        for m in range(2):
            i = m * B_HEADS + h
            pv = jnp.dot(p_buf[slot, m], v, preferred_element_type=F32)
            acc_ref[i] = jnp.tile(a_buf[slot, m], (1, 2)) * acc_ref[i] + pv

    _run_units(jnp.where(step == diag_step, n_sub - 1, n_sub) * B_HEADS, scores, softmax, values)

    @pl.when(step == n_steps - 1)
    def _():
        lam = (jnp.exp(jnp.sum(lq1_ref[...] * lk1_ref[...], axis=1, keepdims=True))
               - jnp.exp(jnp.sum(lq2_ref[...] * lk2_ref[...], axis=1, keepdims=True)) + lam_init)

        def normalized(i):
            acc = acc_ref[i]
            return acc[:, 0:B_VDIM] / acc[:, B_VDIM:B_VDIM + 1]

        for h in range(B_HEADS):
            y = normalized(h) - lam * normalized(B_HEADS + h)
            ms = jnp.mean(y * y, axis=1, keepdims=True)
            y = y * lax.rsqrt(ms + RMS_EPS) * g_ref[...] * (1.0 - lam_init)
            o_ref[:, h * B_VDIM:(h + 1) * B_VDIM] = y.astype(BF16)


def _diff_attention(qb, kb, vb, lq1, lk1, lq2, lk2, subln_g, lam_init):
    _, b, seq, _ = qb.shape
    tq = min(DIFF_TQ, seq)
    tk = min(DIFF_TK, seq)
    span = min(DIFF_SPAN, seq)
    n_steps = seq // span
    assert tk % tq == 0 and span // tk >= 2
    nh = 2 * B_HEADS
    slopes = jnp.asarray([2.0 ** (-8.0 * (h + 1) / B_HEADS) * LOG2E for h in range(B_HEADS)], F32)
    vec = pl.BlockSpec((1, HEAD_DIM), lambda bi, qi, si, *_: (0, 0))
    return pl.pallas_call(
        functools.partial(_diff_kernel, tq=tq, tk=tk, span=span, n_steps=n_steps, lam_init=lam_init),
        grid_spec=pltpu.PrefetchScalarGridSpec(
            num_scalar_prefetch=1,
            grid=(b, seq // tq, n_steps),
            in_specs=[pl.BlockSpec((nh, None, tq, LANES), lambda bi, qi, si, *_: (0, bi, qi, 0)),
                      pl.BlockSpec((nh, None, span, LANES), lambda bi, qi, si, *_: (0, bi, si, 0)),
                      pl.BlockSpec((B_HEADS, None, span, 2 * B_VDIM), lambda bi, qi, si, *_: (0, bi, si, 0)),
                      vec, vec, vec, vec, pl.BlockSpec((1, B_VDIM), lambda bi, qi, si, *_: (0, 0))],
            out_specs=pl.BlockSpec((None, tq, B_WIDTH), lambda bi, qi, si, *_: (bi, qi, 0)),
            scratch_shapes=[pltpu.VMEM((PIPE_SLOTS, 2, tq, tk), F32), pltpu.VMEM((PIPE_SLOTS, 2, tq, tk), BF16),
                            pltpu.VMEM((PIPE_SLOTS, 2, tq, LANES), F32),
                            pltpu.VMEM((nh, tq, LANES), F32), pltpu.VMEM((nh, tq, 2 * B_VDIM), F32)],
        ),
        out_shape=jax.ShapeDtypeStruct((b, seq, B_WIDTH), BF16),
        compiler_params=_params("parallel", "parallel", "arbitrary"),
        name="diff_attn",
    )(slopes, qb, kb, vb, lq1, lk1, lq2, lk2, subln_g)


def _merge_kernel(ya_ref, yb_ref, gate_ref, h_ref, p_ref, wa_ref, wb_ref, wo_ref, g_ref, b_ref,
                  wrh_ref, wrl_ref, wpg_ref, wpp_ref, x1_ref, base_ref, aff_ref):
    ma = jnp.dot(ya_ref[...], wa_ref[...], preferred_element_type=F32)
    mb = jnp.dot(yb_ref[...], wb_ref[...], preferred_element_type=F32)
    merged = _sigmoid(gate_ref[:, 0:D_MODEL]) * ma + _sigmoid(gate_ref[:, D_MODEL:2 * D_MODEL]) * mb
    out = jnp.dot(merged.astype(BF16), wo_ref[...], preferred_element_type=F32)
    x1 = _layer_norm(ALPHA * h_ref[...] + out, g_ref[...], b_ref[...])
    x1_ref[...] = x1
    xh, xl = _split_bf16(x1)
    ple_gate = _sigmoid(jnp.dot(xh, wpg_ref[...], preferred_element_type=F32))
    base_ref[...] = ALPHA * x1 + ple_gate * jnp.dot(p_ref[...].astype(BF16), wpp_ref[...], preferred_element_type=F32)
    wh = wrh_ref[...]
    logits = (lax.dot_general(wh, xh, NT_DIMS, preferred_element_type=F32)
              + lax.dot_general(wh, xl, NT_DIMS, preferred_element_type=F32)
              + lax.dot_general(wrl_ref[...], xh, NT_DIMS, preferred_element_type=F32))
    e = jnp.exp(logits - jnp.max(logits, axis=0, keepdims=True))
    aff_ref[...] = e / jnp.sum(e, axis=0, keepdims=True)


def _merge(ya, yb, gates, h, p2, wa, wb, wo, g, b, wrh, wrl, wpg, wpp):
    n = h.shape[0]
    tm = ROW_TILE

    def rows(w):
        return pl.BlockSpec((tm, w), lambda i: (i, 0))

    return pl.pallas_call(
        _merge_kernel,
        grid=(n // tm,),
        in_specs=[rows(A_WIDTH), rows(B_WIDTH), rows(2 * D_MODEL), rows(D_MODEL), rows(PLE_DIM),
                  _full((A_WIDTH, D_MODEL)), _full((B_WIDTH, D_MODEL)), _full((D_MODEL, D_MODEL)),
                  _full((1, D_MODEL)), _full((1, D_MODEL)),
                  _full((N_EXPERTS, D_MODEL)), _full((N_EXPERTS, D_MODEL)),
                  _full((D_MODEL, D_MODEL)), _full((PLE_DIM, D_MODEL))],
        out_specs=[rows(D_MODEL), rows(D_MODEL), pl.BlockSpec((N_EXPERTS, tm), lambda i: (0, i))],
        out_shape=[jax.ShapeDtypeStruct((n, D_MODEL), F32), jax.ShapeDtypeStruct((n, D_MODEL), F32),
                   jax.ShapeDtypeStruct((N_EXPERTS, n), F32)],
        compiler_params=_params("parallel"),
        name="merge",
    )(ya, yb, gates, h, p2, wa, wb, wo, g, b, wrh, wrl, wpg, wpp)


def _route_kernel(aff_ref, rank_ref, idx_ref, blkp_ref, sel_scr, wi_scr, p_scr, *, nb, cap):
    ne = N_EXPERTS
    shape3 = (ne, nb, LANES)
    bits = pltpu.bitcast(aff_ref[...], I32).reshape(shape3)
    capf = float(cap)

    def count(mask3):
        part = jnp.sum(jnp.where(mask3, 1.0, 0.0), axis=1, keepdims=True)
        return jnp.sum(part, axis=2, keepdims=True)

    def search(i, thr):
        cand = thr | jnp.left_shift(jnp.int32(1), 30 - i)
        return jnp.where(count(bits >= cand) >= capf, cand, thr)

    thr = lax.fori_loop(0, 31, search, jnp.zeros((ne, 1, 1), I32))
    gt = bits > thr
    eq = bits == thr
    need = capf - count(gt)

    li = lax.broadcasted_iota(I32, (LANES, LANES), 0)
    lj = lax.broadcasted_iota(I32, (LANES, LANES), 1)
    upper_incl = jnp.where(li <= lj, 1.0, 0.0).astype(BF16)
    lower_incl = jnp.where(lj <= li, 1.0, 0.0).astype(BF16)
    bi = lax.broadcasted_iota(I32, (nb, nb), 0)
    bj = lax.broadcasted_iota(I32, (nb, nb), 1)
    lower_strict = jnp.where(bj < bi, 1.0, 0.0).astype(BF16)
    upper_strict = jnp.where(bi < bj, 1.0, 0.0).astype(BF16)

    def prefix(mask2):
        wi = jnp.dot(mask2.astype(BF16), upper_incl, preferred_element_type=F32)
        tot = jnp.broadcast_to(wi[:, LANES - 1:LANES], wi.shape).astype(BF16)
        blk = [jnp.dot(lower_strict, tot[e * nb:(e + 1) * nb], preferred_element_type=F32)
               for e in range(ne)]
        return wi, jnp.concatenate(blk, axis=0)

    eqf = jnp.where(eq, 1.0, 0.0).reshape(ne * nb, LANES)
    wi, bp = prefix(eqf)
    eq_rank = (wi + bp - eqf).reshape(shape3)
    sel = gt | (eq & (eq_rank < need))
    self_ = jnp.where(sel, 1.0, 0.0).reshape(ne * nb, LANES)
    wi, bp = prefix(self_)
    rank = wi + bp - self_
    rank_ref[...] = jnp.where(self_ > 0.0, rank, -1.0).astype(I32)
    sel_scr[...] = self_
    wi_scr[...] = wi
    p_scr[...] = bp

    chunk = min(cap, 1024)
    ones8 = jnp.ones((8, LANES), BF16)
    b_col = lax.broadcasted_iota(I32, (nb, 1), 0).astype(F32)

    def per_expert(e, carry):
        r0 = pl.multiple_of(e * nb, 8)
        sel_e = sel_scr[pl.ds(r0, nb), :].astype(BF16)
        wi_e = wi_scr[pl.ds(r0, nb), :]
        p_e = p_scr[pl.ds(r0, nb), :][:, 0:1]
        p_end = p_e + wi_e[:, LANES - 1:LANES]
        wi_t = lax.dot_general(lower_incl, sel_e, NT_DIMS, preferred_element_type=F32).astype(BF16)
        tot_row = lax.dot_general(ones8, sel_e, NT_DIMS, preferred_element_type=F32)
        p_row = jnp.dot(tot_row.astype(BF16), upper_strict, preferred_element_type=F32)
        blkp_ref[e] = p_row[0:1, :].astype(I32)
        for c in range(cap // chunk):
            slot = (lax.broadcasted_iota(I32, (1, chunk), 1) + c * chunk).astype(F32)
            hot = (p_e <= slot) & (slot < p_end)
            w_slot = jnp.dot(wi_t, jnp.where(hot, 1.0, 0.0).astype(BF16), preferred_element_type=F32)
            p_slot = jnp.sum(jnp.where(hot, p_e, 0.0), axis=0, keepdims=True)
            b_slot = jnp.sum(jnp.where(hot, b_col, 0.0), axis=0, keepdims=True)
            lane = jnp.sum(jnp.where(w_slot <= slot - p_slot, 1.0, 0.0), axis=0, keepdims=True)
            idx_ref[e, :, c * chunk:(c + 1) * chunk] = (b_slot * LANES + lane).astype(I32)
        return carry

    lax.fori_loop(0, ne, per_expert, 0)


def _route(aff2, n, cap):
    nb = n // LANES
    return pl.pallas_call(
        functools.partial(_route_kernel, nb=nb, cap=cap),
        out_shape=[jax.ShapeDtypeStruct((N_EXPERTS * nb, LANES), I32),
                   jax.ShapeDtypeStruct((N_EXPERTS, 1, cap), I32),
                   jax.ShapeDtypeStruct((N_EXPERTS, 1, nb), I32)],
        scratch_shapes=[pltpu.VMEM((N_EXPERTS * nb, LANES), F32)] * 3,
        compiler_params=pltpu.CompilerParams(vmem_limit_bytes=VMEM_LIMIT),
        name="route",
    )(aff2)


def _ffn_kernel(idx_ref, x_hbm, wg_ref, wu_ref, wd_ref, y_ref, xbuf, sem, *, tm, tiles, n_steps):
    step = pl.program_id(0) * tiles + pl.program_id(1)
    slot = step & 1

    def fetch(src_step, dst_slot):
        for r in range(tm):
            pltpu.make_async_copy(x_hbm.at[pl.ds(idx_ref[src_step * tm + r], 1), :],
                                  xbuf.at[dst_slot, pl.ds(r, 1), :], sem.at[dst_slot]).start()

    def wait(which):
        pltpu.make_async_copy(x_hbm.at[pl.ds(0, tm), :], xbuf.at[which], sem.at[which]).wait()

    @pl.when(step == 0)
    def _():
        fetch(0, 0)

    fetch(jnp.minimum(step + 1, n_steps - 1), 1 - slot)
    wait(slot)
    x = xbuf[slot].astype(BF16)
    fc = 512
    acc = None
    for c in range(EXPERT_FF // fc):
        sl = slice(c * fc, (c + 1) * fc)
        a = jnp.dot(x, wg_ref[:, sl], preferred_element_type=F32)
        u = jnp.dot(x, wu_ref[:, sl], preferred_element_type=F32)
        mid = (a * _sigmoid(a) * u).astype(BF16)
        part = jnp.dot(mid, wd_ref[sl, :], preferred_element_type=F32)
        acc = part if acc is None else acc + part
    y_ref[...] = acc.astype(BF16)

    @pl.when(step == n_steps - 1)
    def _():
        wait(1 - slot)


def _ffn(idx_flat, x1, wg, wu, wd, cap):
    tm = min(ROW_TILE, cap)
    tiles = cap // tm
    return pl.pallas_call(
        functools.partial(_ffn_kernel, tm=tm, tiles=tiles, n_steps=N_EXPERTS * tiles),
        grid_spec=pltpu.PrefetchScalarGridSpec(
            num_scalar_prefetch=1,
            grid=(N_EXPERTS, tiles),
            in_specs=[pl.BlockSpec(memory_space=pl.ANY),
                      pl.BlockSpec((None, D_MODEL, EXPERT_FF), lambda e, i, idx: (e, 0, 0)),
                      pl.BlockSpec((None, D_MODEL, EXPERT_FF), lambda e, i, idx: (e, 0, 0)),
                      pl.BlockSpec((None, EXPERT_FF, D_MODEL), lambda e, i, idx: (e, 0, 0))],
            out_specs=pl.BlockSpec((tm, D_MODEL), lambda e, i, idx: (e * tiles + i, 0)),
            scratch_shapes=[pltpu.VMEM((2, tm, D_MODEL), F32), pltpu.SemaphoreType.DMA((2,))],
        ),
        out_shape=jax.ShapeDtypeStruct((N_EXPERTS * cap, D_MODEL), BF16),
        compiler_params=_params("arbitrary", "arbitrary"),
        name="ffn",
    )(idx_flat, x1, wg, wu, wd)


TOK_TILE = 128
ROW_ALIGN = 16
WIN = TOK_TILE + ROW_ALIGN
SPARSE_CNT = 32
SPARSE_WIN = SPARSE_CNT + ROW_ALIGN
GROUP = 4
assert GROUP * SPARSE_WIN <= MXU_DEPTH and N_EXPERTS % GROUP == 0


def _final_kernel(start_ref, sparse_ref, base_ref, rank_ref, aff_ref, g_ref, b_ref, ye_hbm, o_ref, buf, sem,
                  *, cap, n_steps):
    t = pl.program_id(0)
    slot = t & 1

    def window(step, e, rows):
        first = start_ref[step * N_EXPERTS + e]
        ws = jnp.minimum((first // ROW_ALIGN) * ROW_ALIGN, cap - rows)
        return pl.multiple_of(ws, ROW_ALIGN)

    def fetch(step, dst_slot):
        def start_copies(rows):
            for e in range(N_EXPERTS):
                pltpu.make_async_copy(ye_hbm.at[pl.ds(e * cap + window(step, e, rows), rows), :],
                                      buf.at[dst_slot, pl.ds(e * rows, rows), :], sem.at[dst_slot]).start()

        @pl.when(sparse_ref[step] == 1)
        def _():
            start_copies(SPARSE_WIN)

        @pl.when(sparse_ref[step] == 0)
        def _():
            start_copies(WIN)

    def wait(rows):
        pltpu.make_async_copy(ye_hbm.at[pl.ds(0, N_EXPERTS * rows), :],
                              buf.at[slot, pl.ds(0, N_EXPERTS * rows), :], sem.at[slot]).wait()

    @pl.when(t == 0)
    def _():
        fetch(0, 0)

    @pl.when(t + 1 < n_steps)
    def _():
        fetch(t + 1, 1 - slot)

    @pl.when(sparse_ref[t] == 1)
    def _():
        wait(SPARSE_WIN)
        gate = aff_ref[...]
        gate_hi = gate.astype(BF16).astype(F32)
        gate_lo = gate - gate_hi
        lane = lax.broadcasted_iota(I32, (1, GROUP * SPARSE_WIN), 1)
        total = base_ref[...]
        for grp in range(N_EXPERTS // GROUP):
            hi = jnp.zeros((TOK_TILE, GROUP * SPARSE_WIN), F32)
            lo = hi
            for k in range(GROUP):
                e = grp * GROUP + k
                rank = rank_ref[:, e:e + 1]
                target = jnp.where(rank >= 0, rank - window(t, e, SPARSE_WIN) + k * SPARSE_WIN, -1)
                hot = lane == target
                hi = jnp.where(hot, gate_hi[:, e:e + 1], hi)
                lo = jnp.where(hot, gate_lo[:, e:e + 1], lo)
            rows = buf[slot, grp * GROUP * SPARSE_WIN:(grp + 1) * GROUP * SPARSE_WIN, :]
            y = jnp.dot(jnp.concatenate([hi, lo], axis=0).astype(BF16), rows, preferred_element_type=F32)
            total = total + y[0:TOK_TILE] + y[TOK_TILE:2 * TOK_TILE]
        o_ref[...] = _layer_norm(total, g_ref[...], b_ref[...])

    @pl.when(sparse_ref[t] == 0)
    def _():
        wait(WIN)
        lane = lax.broadcasted_iota(I32, (1, WIN), 1)
        total = base_ref[...]
        for e in range(N_EXPERTS):
            hot = rank_ref[:, e:e + 1] == lane + window(t, e, WIN)
            y = jnp.dot(jnp.where(hot, 1.0, 0.0).astype(BF16), buf[slot, e * WIN:(e + 1) * WIN, :],
                        preferred_element_type=F32)
            total = total + aff_ref[:, e:e + 1] * y
        o_ref[...] = _layer_norm(total, g_ref[...], b_ref[...])


def _final(starts, sparse, base, rank_t, aff_t, g, b, ye, cap):
    n = base.shape[0]
    tm = TOK_TILE

    def rows(w):
        return pl.BlockSpec((tm, w), lambda i, *_: (i, 0))

    def full(shape):
        return pl.BlockSpec(shape, lambda i, *_: (0,) * len(shape))

    return pl.pallas_call(
        functools.partial(_final_kernel, cap=cap, n_steps=n // tm),
        grid_spec=pltpu.PrefetchScalarGridSpec(
            num_scalar_prefetch=2,
            grid=(n // tm,),
            in_specs=[rows(D_MODEL), rows(N_EXPERTS), rows(N_EXPERTS), full((1, D_MODEL)), full((1, D_MODEL)),
                      pl.BlockSpec(memory_space=pl.ANY)],
            out_specs=rows(D_MODEL),
            scratch_shapes=[pltpu.VMEM((2, N_EXPERTS * WIN, D_MODEL), BF16), pltpu.SemaphoreType.DMA((2,))],
        ),
        out_shape=jax.ShapeDtypeStruct((n, D_MODEL), F32),
        compiler_params=_params("arbitrary"),
        name="final",
    )(starts, sparse, base, rank_t, aff_t, g, b, ye)


def _rope_tables(seq):
    n_rows = seq // GRID_W
    rows = jnp.repeat(jnp.arange(n_rows, dtype=F32), GRID_W)
    cols = jnp.tile(jnp.arange(GRID_W, dtype=F32), n_rows)
    sec = HEAD_DIM // 2
    inv_freq = ROPE_THETA ** (-jnp.arange(0, sec, 2, dtype=F32) / sec)
    ang_r = rows[:, None] * inv_freq[None, :]
    ang_c = cols[:, None] * inv_freq[None, :]
    cos = jnp.concatenate([jnp.cos(ang_r)] * 2 + [jnp.cos(ang_c)] * 2, axis=-1)
    sin = jnp.concatenate([-jnp.sin(ang_r), jnp.sin(ang_r), -jnp.sin(ang_c), jnp.sin(ang_c)], axis=-1)
    return jnp.tile(cos, (1, 2)), jnp.tile(sin, (1, 2))


def _trunk(x, p, w):
    b, seq, _ = x.shape
    n = b * seq
    cap = max(1, CAPACITY_FACTOR * n // N_EXPERTS)
    assert seq % ROW_TILE == 0 and n % (8 * LANES) == 0 and cap % TOK_TILE == 0 and cap >= WIN
    cos_t, sin_t = _rope_tables(seq)

    h, qa, ka, va, qb, kb, vb, gates = _in_proj(
        x.reshape(n, D_MODEL), w["ln_emb_g"], w["ln_emb_b"], w["w_in"], w["q_norm_g"], w["k_norm_g"],
        cos_t, sin_t, w["seg"], seq)

    def per_batch(t):
        return t.reshape(t.shape[0], b, seq, t.shape[2])

    ya = _gqa_attention(per_batch(qa), per_batch(ka), per_batch(va))
    lam_init = 0.8 - 0.6 * math.exp(-0.3 * 0)
    yb = _diff_attention(per_batch(qb), per_batch(kb), per_batch(vb),
                         w["lambda_q1"], w["lambda_k1"], w["lambda_q2"], w["lambda_k2"], w["subln_g"], lam_init)

    x1, base, aff = _merge(ya.reshape(n, -1), yb.reshape(n, -1), gates, h, p.reshape(n, PLE_DIM),
                           w["w_branch_a"], w["w_branch_b"], w["w_out"], w["ln1_g"], w["ln1_b"],
                           w["wr_hi"], w["wr_lo"], w["w_ple_gate"], w["w_ple_proj"])

    nb = n // LANES
    rank2, idx, blkp = _route(aff.reshape(N_EXPERTS * nb, LANES), n, cap)
    ye = _ffn(idx.reshape(N_EXPERTS * cap), x1, w["w_gate"], w["w_up"], w["w_down"], cap)

    starts = blkp.reshape(N_EXPERTS, nb)[:, ::TOK_TILE // LANES]
    ends = jnp.concatenate([starts[:, 1:], jnp.full((N_EXPERTS, 1), cap, I32)], axis=1)
    sparse = (jnp.max(ends - starts, axis=0) <= SPARSE_CNT).astype(I32)
    out = _final(starts.T.reshape(-1), sparse, base, rank2.reshape(N_EXPERTS, n).T, aff.T,
                 w["ln2_g"], w["ln2_b"], ye, cap)
    return out.reshape(b, seq, D_MODEL)


def _prep_weights(ln_emb_g, ln_emb_b, w_in, q_norm_g, k_norm_g, lambda_q1, lambda_k1, lambda_q2, lambda_k2,
                  subln_g, w_branch_a, w_branch_b, w_out, ln1_g, ln1_b, w_router, w_gate, w_up, w_down,
                  w_ple_gate, w_ple_proj, ln2_g, ln2_b):
    i = 0
    seg_i = lax.broadcasted_iota(I32, (A_WIDTH, A_WIDTH), 0) // HEAD_DIM
    seg_j = lax.broadcasted_iota(I32, (A_WIDTH, A_WIDTH), 1) // HEAD_DIM
    wr_t = w_router[i].T
    wr_hi = wr_t.astype(BF16)
    return {
        "ln_emb_g": ln_emb_g.reshape(1, D_MODEL), "ln_emb_b": ln_emb_b.reshape(1, D_MODEL),
        "w_in": w_in[i].astype(BF16),
        "q_norm_g": jnp.tile(q_norm_g[i], 2).reshape(1, LANES), "k_norm_g": jnp.tile(k_norm_g[i], 2).reshape(1, LANES),
        "seg": (seg_i == seg_j).astype(BF16),
        "lambda_q1": lambda_q1[i].reshape(1, HEAD_DIM), "lambda_k1": lambda_k1[i].reshape(1, HEAD_DIM),
        "lambda_q2": lambda_q2[i].reshape(1, HEAD_DIM), "lambda_k2": lambda_k2[i].reshape(1, HEAD_DIM),
        "subln_g": subln_g[i].reshape(1, B_VDIM),
        "w_branch_a": w_branch_a[i].astype(BF16), "w_branch_b": w_branch_b[i].astype(BF16),
        "w_out": w_out[i].astype(BF16),
        "ln1_g": ln1_g[i].reshape(1, D_MODEL), "ln1_b": ln1_b[i].reshape(1, D_MODEL),
        "wr_hi": wr_hi, "wr_lo": (wr_t - wr_hi.astype(F32)).astype(BF16),
        "w_gate": w_gate[i].astype(BF16), "w_up": w_up[i].astype(BF16), "w_down": w_down[i].astype(BF16),
        "w_ple_gate": w_ple_gate[i].astype(BF16), "w_ple_proj": w_ple_proj[i].astype(BF16),
        "ln2_g": ln2_g[i].reshape(1, D_MODEL), "ln2_b": ln2_b[i].reshape(1, D_MODEL),
    }


def kernel(x_prompt, x_sample, p_prompt, p_sample, ln_emb_g, ln_emb_b, w_in, q_norm_g, k_norm_g, lambda_q1, lambda_k1, lambda_q2, lambda_k2, subln_g, w_branch_a, w_branch_b, w_out, ln1_g, ln1_b, w_router, w_gate, w_up, w_down, w_ple_gate, w_ple_proj, ln2_g, ln2_b):
    w = _prep_weights(ln_emb_g, ln_emb_b, w_in, q_norm_g, k_norm_g, lambda_q1, lambda_k1, lambda_q2, lambda_k2,
                      subln_g, w_branch_a, w_branch_b, w_out, ln1_g, ln1_b, w_router, w_gate, w_up, w_down,
                      w_ple_gate, w_ple_proj, ln2_g, ln2_b)
    return (_trunk(x_prompt, p_prompt[0], w), _trunk(x_sample, p_sample[0], w))
```

```python
import functools
import math

import jax
import jax.numpy as jnp
from jax import lax
from jax.experimental import pallas as pl
from jax.experimental.pallas import tpu as pltpu

F32 = jnp.float32
BF16 = jnp.bfloat16
I32 = jnp.int32

D_MODEL = 1024
DEPTH = 1
HEAD_DIM = 64
A_HEADS = 8
A_KV_HEADS = 2
A_GROUP = A_HEADS // A_KV_HEADS
A_WIDTH = A_HEADS * HEAD_DIM
B_HEADS = 4
B_VDIM = 2 * HEAD_DIM
B_WIDTH = B_HEADS * B_VDIM
GRID_W = 64
ROPE_THETA = 10000.0
N_EXPERTS = 16
EXPERT_FF = 2048
CAPACITY_FACTOR = 2
PLE_DIM = 256
LN_EPS = 1e-5
RMS_EPS = 1e-6
ALPHA = (2 * DEPTH) ** 0.25
LOG2E = math.log2(math.e)
QSCALE = HEAD_DIM ** -0.5 * LOG2E
LANES = 128
ROPE_HALF = HEAD_DIM // 4

C_QA = 0
C_KA = C_QA + A_WIDTH
C_VA = C_KA + A_KV_HEADS * HEAD_DIM
C_QB = C_VA + A_KV_HEADS * HEAD_DIM
C_KB = C_QB + 2 * B_HEADS * HEAD_DIM
C_VB = C_KB + 2 * B_HEADS * HEAD_DIM
C_G = C_VB + B_WIDTH
IN_COLS = C_G + 2 * D_MODEL

VMEM_LIMIT = 56 * 1024 * 1024
MXU_DEPTH = 256

ROW_TILE = 512
GQA_TQ, GQA_TK = 256, 1024
DIFF_TQ, DIFF_TK, DIFF_SPAN = 512, 1024, 2048
NT_DIMS = (((1,), (1,)), ((), ()))


def _params(*sem):
    return pltpu.CompilerParams(dimension_semantics=sem, vmem_limit_bytes=VMEM_LIMIT)


def _full(shape):
    n = len(shape)
    return pl.BlockSpec(shape, lambda *_: (0,) * n, pipeline_mode=pl.Buffered(1))


def _layer_norm(x, g, b):
    mu = jnp.mean(x, axis=-1, keepdims=True)
    xc = x - mu
    var = jnp.mean(xc * xc, axis=-1, keepdims=True)
    return xc * lax.rsqrt(var + LN_EPS) * g + b


def _sigmoid(x):
    return 1.0 / (1.0 + jnp.exp(-x))


def _split_bf16(x):
    hi = x.astype(BF16)
    lo = (x - hi.astype(F32)).astype(BF16)
    return hi, lo


def _head_mean_sq(x, seg):
    hi, lo = _split_bf16(x * x)
    s = jnp.dot(hi, seg, preferred_element_type=F32) + jnp.dot(lo, seg, preferred_element_type=F32)
    return s * (1.0 / HEAD_DIM)


def _rope(x, cos, sin, first_half):
    fwd = pltpu.roll(x, LANES - ROPE_HALF, 1)
    bwd = pltpu.roll(x, ROPE_HALF, 1)
    return x * cos + jnp.where(first_half, fwd, bwd) * sin


def _in_proj_kernel(x_ref, lng_ref, lnb_ref, w_ref, qg_ref, kg_ref, cos_ref, sin_ref, seg_ref, qc_ref, kc_ref,
                    h_ref, qa_ref, ka_ref, va_ref, qb_ref, kb_ref, vb_ref, gate_ref, *, tiles_per_seq):
    xn = _layer_norm(x_ref[...], lng_ref[...], lnb_ref[...])
    h_ref[...] = xn
    xb = xn.astype(BF16)

    def proj(lo, hi):
        return jnp.dot(xb, w_ref[:, lo:hi], preferred_element_type=F32)

    def store_heads(ref, x):
        for i in range(x.shape[1] // HEAD_DIM):
            ref[i] = x[:, i * HEAD_DIM:(i + 1) * HEAD_DIM].astype(BF16)

    cos = cos_ref[...]
    sin = sin_ref[...]
    lane = lax.broadcasted_iota(I32, cos.shape, 1)
    first_half = (lane & ROPE_HALF) == 0

    qa = proj(C_QA, C_KA)
    qn = qa * lax.rsqrt(_head_mean_sq(qa, seg_ref[...]) + RMS_EPS)
    for c in range(A_WIDTH // LANES):
        pair = _rope(qn[:, c * LANES:(c + 1) * LANES] * qg_ref[...], cos, sin, first_half) * QSCALE
        qa_ref[2 * c] = pair[:, 0:HEAD_DIM].astype(BF16)
        qa_ref[2 * c + 1] = pair[:, HEAD_DIM:LANES].astype(BF16)

    ka = proj(C_KA, C_VA)
    kn = ka * lax.rsqrt(_head_mean_sq(ka, seg_ref[0:LANES, 0:LANES]) + RMS_EPS)
    store_heads(ka_ref, _rope(kn * kg_ref[...], cos, sin, first_half))
    one_at_0 = jnp.where(lane == 0, 1.0, 0.0)
    va = proj(C_VA, C_QB)
    for g in range(A_KV_HEADS):
        v_first = va if g == 0 else pltpu.roll(va, HEAD_DIM, 1)
        va_ref[g] = jnp.where(lane < HEAD_DIM, v_first, jnp.where(lane == HEAD_DIM, 1.0, 0.0)).astype(BF16)
    pos = (pl.program_id(0) % tiles_per_seq) * cos.shape[0] + lax.broadcasted_iota(I32, cos.shape, 0)
    hi = (pos >> 7).astype(F32)
    lo = (pos & (LANES - 1)).astype(F32)
    pair_of = lane >> 1
    q_pos = jnp.where(pair_of == 32, hi, jnp.where(pair_of == 33, lo, 0.0))
    k_pos = jnp.where(pair_of == 34, hi, jnp.where(pair_of == 35, lo, 0.0))

    def store_with_features(ref, x, pos_feat, const_ref):
        for i in range(2 * B_HEADS):
            chunk = x[:, (i // 2) * LANES:(i // 2 + 1) * LANES]
            first = chunk if i % 2 == 0 else pltpu.roll(chunk, HEAD_DIM, 1)
            feat = pos_feat + const_ref[i % B_HEADS:i % B_HEADS + 1, :]
            ref[i] = jnp.where(lane < HEAD_DIM, first, feat).astype(BF16)

    store_with_features(qb_ref, proj(C_QB, C_KB) * QSCALE, q_pos, qc_ref)
    store_with_features(kb_ref, proj(C_KB, C_VB), k_pos, kc_ref)
    vb = proj(C_VB, C_G)
    for i in range(B_HEADS):
        vb_ref[i, :, 0:B_VDIM] = vb[:, i * B_VDIM:(i + 1) * B_VDIM].astype(BF16)
        vb_ref[i, :, B_VDIM:2 * B_VDIM] = one_at_0.astype(BF16)
    gate_ref[...] = proj(C_G, IN_COLS)


def _alibi_feature_constants():
    c = jnp.asarray([2.0 ** (-8.0 * (h + 1) / B_HEADS) * LOG2E for h in range(B_HEADS)], F32)
    c1 = c.astype(BF16).astype(F32)
    c2 = (c - c1).astype(BF16).astype(F32)
    parts = jnp.stack([LANES * c1, LANES * c2, c1, c2], axis=1)
    zeros = jnp.zeros((B_HEADS, LANES), F32)
    return (zeros.at[:, HEAD_DIM + 4:HEAD_DIM + 8].set(-parts), zeros.at[:, HEAD_DIM:HEAD_DIM + 4].set(parts))


def _in_proj(x2, lng, lnb, w_in, qg, kg, cos_t, sin_t, seg, seq):
    n = x2.shape[0]
    tm = ROW_TILE
    tiles_per_seq = seq // tm

    def rows(w):
        return pl.BlockSpec((tm, w), lambda i: (i, 0))

    def heads(nh, w):
        return pl.BlockSpec((nh, tm, w), lambda i: (0, i, 0))

    tab = pl.BlockSpec((tm, LANES), lambda i: (i % tiles_per_seq, 0))
    head_outs = [(A_HEADS, HEAD_DIM), (A_KV_HEADS, HEAD_DIM), (A_KV_HEADS, LANES),
                 (2 * B_HEADS, LANES), (2 * B_HEADS, LANES), (B_HEADS, 2 * B_VDIM)]
    q_const, k_const = _alibi_feature_constants()
    return pl.pallas_call(
        functools.partial(_in_proj_kernel, tiles_per_seq=tiles_per_seq),
        grid=(n // tm,),
        in_specs=[rows(D_MODEL), _full((1, D_MODEL)), _full((1, D_MODEL)), _full((D_MODEL, IN_COLS)),
                  _full((1, LANES)), _full((1, LANES)), tab, tab, _full((A_WIDTH, A_WIDTH)),
                  _full((B_HEADS, LANES)), _full((B_HEADS, LANES))],
        out_specs=[rows(D_MODEL)] + [heads(nh, w) for nh, w in head_outs] + [rows(2 * D_MODEL)],
        out_shape=([jax.ShapeDtypeStruct((n, D_MODEL), F32)]
                   + [jax.ShapeDtypeStruct((nh, n, w), BF16) for nh, w in head_outs]
                   + [jax.ShapeDtypeStruct((n, 2 * D_MODEL), F32)]),
        compiler_params=_params("parallel"),
        name="in_proj",
    )(x2, lng, lnb, w_in, qg, kg, cos_t, sin_t, seg, q_const, k_const)


PIPE_SLOTS = 2


def _run_units(n_units, scores, softmax, values):
    scores(0, 0)
    scores(1, 1)
    softmax(0, 0)

    def two_units(i, carry):
        t = PIPE_SLOTS * (i + 1)
        scores(t, 0)
        softmax(t - 1, 1)
        values(t - 2, 0)
        scores(t + 1, 1)
        softmax(t, 0)
        values(t - 1, 1)
        return carry

    lax.fori_loop(0, (n_units >> 1) - 1, two_units, 0)
    softmax(n_units - 1, 1)
    values(n_units - 2, 0)
    values(n_units - 1, 1)


def _chunk_max(s):
    cm = s[:, 0:LANES]
    for c in range(1, s.shape[1] // LANES):
        cm = jnp.maximum(cm, s[:, c * LANES:(c + 1) * LANES])
    return cm


def _softmax_update(load_s, m_ref, i, chunk_max=None):
    m_prev = m_ref[i]
    row_max = jnp.max(load_s() if chunk_max is None else chunk_max, axis=1, keepdims=True)
    m_new = jnp.maximum(m_prev, row_max)
    alpha = jnp.exp2(m_prev - m_new)
    s = load_s()
    p = jnp.exp2(s - jnp.tile(m_new, (1, s.shape[1] // LANES)))
    m_ref[i] = m_new
    return p.astype(BF16), alpha


def _init_softmax_state(m_ref, acc_ref):
    m_ref[...] = jnp.full(m_ref.shape, -jnp.inf, F32)
    acc_ref[...] = jnp.zeros(acc_ref.shape, F32)


def _gqa_kernel(q_ref, k_ref, v_ref, o_ref, s_buf, p_buf, a_buf, x_buf, m_ref, acc_ref, *, tq, tk, n_sub, n_steps):
    step = pl.program_id(2)

    @pl.when(step == 0)
    def _():
        _init_softmax_state(m_ref, acc_ref)

    def keys(u):
        return u & (A_KV_HEADS - 1), pl.ds(pl.multiple_of((u >> 1) * tk, tk), tk)

    def scores(u, slot):
        g, rows = keys(u)
        q = q_ref[pl.ds(g * A_GROUP, A_GROUP)].reshape(A_GROUP * tq, HEAD_DIM)
        s = lax.dot_general(q, k_ref[g, rows, :], NT_DIMS, preferred_element_type=F32)
        s_buf[slot] = s
        x_buf[slot] = _chunk_max(s)

    def softmax(u, slot):
        p, alpha = _softmax_update(lambda: s_buf[slot], m_ref, u & (A_KV_HEADS - 1), chunk_max=x_buf[slot])
        p_buf[slot] = p
        a_buf[slot] = alpha

    def values(u, slot):
        g, rows = keys(u)
        pv = jnp.dot(p_buf[slot], v_ref[g, rows, :], preferred_element_type=F32)
        acc_ref[g] = a_buf[slot] * acc_ref[g] + pv

    _run_units(n_sub * A_KV_HEADS, scores, softmax, values)

    @pl.when(step == n_steps - 1)
    def _():
        for h in range(A_HEADS):
            g, r = divmod(h, A_GROUP)
            acc = acc_ref[g, r * tq:(r + 1) * tq, :]
            y = acc[:, 0:HEAD_DIM] / acc[:, HEAD_DIM:HEAD_DIM + 1]
            o_ref[:, h * HEAD_DIM:(h + 1) * HEAD_DIM] = y.astype(BF16)


def _gqa_attention(qa, ka, va):
    _, b, seq, _ = qa.shape
    tq = min(GQA_TQ, seq)
    tk = min(GQA_TK, seq)
    span = seq
    n_steps = seq // span
    rows = A_GROUP * tq
    resident = pl.Buffered(1)
    return pl.pallas_call(
        functools.partial(_gqa_kernel, tq=tq, tk=tk, n_sub=span // tk, n_steps=n_steps),
        grid=(b, seq // tq, n_steps),
        in_specs=[pl.BlockSpec((A_HEADS, None, tq, HEAD_DIM), lambda bi, qi, si: (0, bi, qi, 0)),
                  pl.BlockSpec((A_KV_HEADS, None, span, HEAD_DIM), lambda bi, qi, si: (0, bi, si, 0),
                               pipeline_mode=resident),
                  pl.BlockSpec((A_KV_HEADS, None, span, LANES), lambda bi, qi, si: (0, bi, si, 0),
                               pipeline_mode=resident)],
        out_specs=pl.BlockSpec((None, tq, A_WIDTH), lambda bi, qi, si: (bi, qi, 0)),
        out_shape=jax.ShapeDtypeStruct((b, seq, A_WIDTH), BF16),
        scratch_shapes=[pltpu.VMEM((PIPE_SLOTS, rows, tk), F32), pltpu.VMEM((PIPE_SLOTS, rows, tk), BF16),
                        pltpu.VMEM((PIPE_SLOTS, rows, LANES), F32), pltpu.VMEM((PIPE_SLOTS, rows, LANES), F32),
                        pltpu.VMEM((A_KV_HEADS, rows, LANES), F32), pltpu.VMEM((A_KV_HEADS, rows, LANES), F32)],
        compiler_params=_params("parallel", "parallel", "arbitrary"),
        name="gqa_attn",
    )(qa, ka, va)


def _diff_kernel(slope_ref, q_ref, k_ref, v_ref, lq1_ref, lk1_ref, lq2_ref, lk2_ref, g_ref, o_ref,
                 s_buf, p_buf, a_buf, x_buf, m_ref, acc_ref, *, tq, tk, span, n_steps, lam_init):
    qi = pl.program_id(1)
    step = pl.program_id(2)
    n_sub = span // tk
    q_pos = qi * tq
    diag_step = q_pos // span
    diag_sub = (q_pos // tk) % n_sub
    lane = lax.broadcasted_iota(I32, (1, LANES), 1)

    @pl.when(step == 0)
    def _():
        _init_softmax_state(m_ref, acc_ref)

    @pl.when(step == diag_step)
    def _():
        rows = pl.ds(pl.multiple_of(diag_sub * tk, tk), tk)
        offset = q_pos - (q_pos // tk) * tk
        dist = jnp.abs(lax.broadcasted_iota(I32, (tq, tk), 0) - lax.broadcasted_iota(I32, (tq, tk), 1) + offset)
        dist = dist.astype(F32)
        for h in range(B_HEADS):
            bias = dist * slope_ref[h]
            v = v_ref[h, rows, :]
            for m in range(2):
                i = m * B_HEADS + h
                s = lax.dot_general(q_ref[i][:, 0:HEAD_DIM], k_ref[i, rows, 0:HEAD_DIM], NT_DIMS,
                                    preferred_element_type=F32) - bias
                p, alpha = _softmax_update(lambda: s, m_ref, i)
                acc_ref[i] = jnp.tile(alpha, (1, 2)) * acc_ref[i] + jnp.dot(p, v, preferred_element_type=F32)

    def keys(u):
        j = u >> 2
        j = j + jnp.where((step == diag_step) & (j >= diag_sub), 1, 0)
        return u & (B_HEADS - 1), j, pl.ds(pl.multiple_of(j * tk, tk), tk)

    def scores(u, slot):
        h, j, rows = keys(u)
        keys_before = q_pos >= step * span + (j + 1) * tk
        sign = jnp.where(keys_before, -1.0, 1.0)
        lane_scale = jnp.where(lane < HEAD_DIM, 1.0, sign).astype(BF16)
        for m in range(2):
            i = m * B_HEADS + h
            s = lax.dot_general(q_ref[i] * lane_scale, k_ref[i, rows, :], NT_DIMS, preferred_element_type=F32)
            s_buf[slot, m] = s
            x_buf[slot, m] = _chunk_max(s)

    def softmax(u, slot):
        h = u & (B_HEADS - 1)
        for m in range(2):
            p, alpha = _softmax_update(lambda: s_buf[slot, m], m_ref, m * B_HEADS + h, chunk_max=x_buf[slot, m])
            p_buf[slot, m] = p
            a_buf[slot, m] = alpha

    def values(u, slot):
        h, _, rows = keys(u)
        v = v_ref[h, rows, :]
        for m in range(2):
            i = m * B_HEADS + h
            pv = jnp.dot(p_buf[slot, m], v, preferred_element_type=F32)
            acc_ref[i] = jnp.tile(a_buf[slot, m], (1, 2)) * acc_ref[i] + pv

    _run_units(jnp.where(step == diag_step, n_sub - 1, n_sub) * B_HEADS, scores, softmax, values)

    @pl.when(step == n_steps - 1)
    def _():
        lam = (jnp.exp(jnp.sum(lq1_ref[...] * lk1_ref[...], axis=1, keepdims=True))
               - jnp.exp(jnp.sum(lq2_ref[...] * lk2_ref[...], axis=1, keepdims=True)) + lam_init)

        def normalized(i):
            acc = acc_ref[i]
            return acc[:, 0:B_VDIM] / acc[:, B_VDIM:B_VDIM + 1]

        for h in range(B_HEADS):
            y = normalized(h) - lam * normalized(B_HEADS + h)
            ms = jnp.mean(y * y, axis=1, keepdims=True)
            y = y * lax.rsqrt(ms + RMS_EPS) * g_ref[...] * (1.0 - lam_init)
            o_ref[:, h * B_VDIM:(h + 1) * B_VDIM] = y.astype(BF16)


def _diff_attention(qb, kb, vb, lq1, lk1, lq2, lk2, subln_g, lam_init):
    _, b, seq, _ = qb.shape
    tq = min(DIFF_TQ, seq)
    tk = min(DIFF_TK, seq)
    span = min(DIFF_SPAN, seq)
    n_steps = seq // span
    assert tk % tq == 0 and span // tk >= 2
    nh = 2 * B_HEADS
    slopes = jnp.asarray([2.0 ** (-8.0 * (h + 1) / B_HEADS) * LOG2E for h in range(B_HEADS)], F32)
    vec = pl.BlockSpec((1, HEAD_DIM), lambda bi, qi, si, *_: (0, 0))
    return pl.pallas_call(
        functools.partial(_diff_kernel, tq=tq, tk=tk, span=span, n_steps=n_steps, lam_init=lam_init),
        grid_spec=pltpu.PrefetchScalarGridSpec(
            num_scalar_prefetch=1,
            grid=(b, seq // tq, n_steps),
            in_specs=[pl.BlockSpec((nh, None, tq, LANES), lambda bi, qi, si, *_: (0, bi, qi, 0)),
                      pl.BlockSpec((nh, None, span, LANES), lambda bi, qi, si, *_: (0, bi, si, 0)),
                      pl.BlockSpec((B_HEADS, None, span, 2 * B_VDIM), lambda bi, qi, si, *_: (0, bi, si, 0)),
                      vec, vec, vec, vec, pl.BlockSpec((1, B_VDIM), lambda bi, qi, si, *_: (0, 0))],
            out_specs=pl.BlockSpec((None, tq, B_WIDTH), lambda bi, qi, si, *_: (bi, qi, 0)),
            scratch_shapes=[pltpu.VMEM((PIPE_SLOTS, 2, tq, tk), F32), pltpu.VMEM((PIPE_SLOTS, 2, tq, tk), BF16),
                            pltpu.VMEM((PIPE_SLOTS, 2, tq, LANES), F32), pltpu.VMEM((PIPE_SLOTS, 2, tq, LANES), F32),
                            pltpu.VMEM((nh, tq, LANES), F32), pltpu.VMEM((nh, tq, 2 * B_VDIM), F32)],
        ),
        out_shape=jax.ShapeDtypeStruct((b, seq, B_WIDTH), BF16),
        compiler_params=_params("parallel", "parallel", "arbitrary"),
        name="diff_attn",
    )(slopes, qb, kb, vb, lq1, lk1, lq2, lk2, subln_g)


def _merge_kernel(ya_ref, yb_ref, gate_ref, h_ref, p_ref, wa_ref, wb_ref, wo_ref, g_ref, b_ref,
                  wrh_ref, wrl_ref, wpg_ref, wpp_ref, x1_ref, base_ref, aff_ref):
    ma = jnp.dot(ya_ref[...], wa_ref[...], preferred_element_type=F32)
    mb = jnp.dot(yb_ref[...], wb_ref[...], preferred_element_type=F32)
    merged = _sigmoid(gate_ref[:, 0:D_MODEL]) * ma + _sigmoid(gate_ref[:, D_MODEL:2 * D_MODEL]) * mb
    out = jnp.dot(merged.astype(BF16), wo_ref[...], preferred_element_type=F32)
    x1 = _layer_norm(ALPHA * h_ref[...] + out, g_ref[...], b_ref[...])
    x1_ref[...] = x1
    xh, xl = _split_bf16(x1)
    ple_gate = _sigmoid(jnp.dot(xh, wpg_ref[...], preferred_element_type=F32))
    base_ref[...] = ALPHA * x1 + ple_gate * jnp.dot(p_ref[...].astype(BF16), wpp_ref[...], preferred_element_type=F32)
    wh = wrh_ref[...]
    logits = (lax.dot_general(wh, xh, NT_DIMS, preferred_element_type=F32)
              + lax.dot_general(wh, xl, NT_DIMS, preferred_element_type=F32)
              + lax.dot_general(wrl_ref[...], xh, NT_DIMS, preferred_element_type=F32))
    e = jnp.exp(logits - jnp.max(logits, axis=0, keepdims=True))
    aff_ref[...] = e / jnp.sum(e, axis=0, keepdims=True)


def _merge(ya, yb, gates, h, p2, wa, wb, wo, g, b, wrh, wrl, wpg, wpp):
    n = h.shape[0]
    tm = ROW_TILE

    def rows(w):
        return pl.BlockSpec((tm, w), lambda i: (i, 0))

    return pl.pallas_call(
        _merge_kernel,
        grid=(n // tm,),
        in_specs=[rows(A_WIDTH), rows(B_WIDTH), rows(2 * D_MODEL), rows(D_MODEL), rows(PLE_DIM),
                  _full((A_WIDTH, D_MODEL)), _full((B_WIDTH, D_MODEL)), _full((D_MODEL, D_MODEL)),
                  _full((1, D_MODEL)), _full((1, D_MODEL)),
                  _full((N_EXPERTS, D_MODEL)), _full((N_EXPERTS, D_MODEL)),
                  _full((D_MODEL, D_MODEL)), _full((PLE_DIM, D_MODEL))],
        out_specs=[rows(D_MODEL), rows(D_MODEL), pl.BlockSpec((N_EXPERTS, tm), lambda i: (0, i))],
        out_shape=[jax.ShapeDtypeStruct((n, D_MODEL), F32), jax.ShapeDtypeStruct((n, D_MODEL), F32),
                   jax.ShapeDtypeStruct((N_EXPERTS, n), F32)],
        compiler_params=_params("parallel"),
        name="merge",
    )(ya, yb, gates, h, p2, wa, wb, wo, g, b, wrh, wrl, wpg, wpp)


def _route_kernel(aff_ref, rank_ref, idx_ref, blkp_ref, sel_scr, wi_scr, p_scr, *, nb, cap):
    ne = N_EXPERTS
    shape3 = (ne, nb, LANES)
    bits = pltpu.bitcast(aff_ref[...], I32).reshape(shape3)
    capf = float(cap)

    def count(mask3):
        part = jnp.sum(jnp.where(mask3, 1.0, 0.0), axis=1, keepdims=True)
        return jnp.sum(part, axis=2, keepdims=True)

    def search(i, thr):
        cand = thr | jnp.left_shift(jnp.int32(1), 30 - i)
        return jnp.where(count(bits >= cand) >= capf, cand, thr)

    thr = lax.fori_loop(0, 31, search, jnp.zeros((ne, 1, 1), I32))
    gt = bits > thr
    eq = bits == thr
    need = capf - count(gt)

    li = lax.broadcasted_iota(I32, (LANES, LANES), 0)
    lj = lax.broadcasted_iota(I32, (LANES, LANES), 1)
    upper_incl = jnp.where(li <= lj, 1.0, 0.0).astype(BF16)
    lower_incl = jnp.where(lj <= li, 1.0, 0.0).astype(BF16)
    bi = lax.broadcasted_iota(I32, (nb, nb), 0)
    bj = lax.broadcasted_iota(I32, (nb, nb), 1)
    lower_strict = jnp.where(bj < bi, 1.0, 0.0).astype(BF16)
    upper_strict = jnp.where(bi < bj, 1.0, 0.0).astype(BF16)

    def prefix(mask2):
        wi = jnp.dot(mask2.astype(BF16), upper_incl, preferred_element_type=F32)
        tot = jnp.broadcast_to(wi[:, LANES - 1:LANES], wi.shape).astype(BF16)
        blk = [jnp.dot(lower_strict, tot[e * nb:(e + 1) * nb], preferred_element_type=F32)
               for e in range(ne)]
        return wi, jnp.concatenate(blk, axis=0)

    eqf = jnp.where(eq, 1.0, 0.0).reshape(ne * nb, LANES)
    wi, bp = prefix(eqf)
    eq_rank = (wi + bp - eqf).reshape(shape3)
    sel = gt | (eq & (eq_rank < need))
    self_ = jnp.where(sel, 1.0, 0.0).reshape(ne * nb, LANES)
    wi, bp = prefix(self_)
    rank = wi + bp - self_
    rank_ref[...] = jnp.where(self_ > 0.0, rank, -1.0).astype(I32)
    sel_scr[...] = self_
    wi_scr[...] = wi
    p_scr[...] = bp

    chunk = min(cap, 1024)
    ones8 = jnp.ones((8, LANES), BF16)
    b_col = lax.broadcasted_iota(I32, (nb, 1), 0).astype(F32)

    def per_expert(e, carry):
        r0 = pl.multiple_of(e * nb, 8)
        sel_e = sel_scr[pl.ds(r0, nb), :].astype(BF16)
        wi_e = wi_scr[pl.ds(r0, nb), :]
        p_e = p_scr[pl.ds(r0, nb), :][:, 0:1]
        p_end = p_e + wi_e[:, LANES - 1:LANES]
        wi_t = lax.dot_general(lower_incl, sel_e, NT_DIMS, preferred_element_type=F32).astype(BF16)
        tot_row = lax.dot_general(ones8, sel_e, NT_DIMS, preferred_element_type=F32)
        p_row = jnp.dot(tot_row.astype(BF16), upper_strict, preferred_element_type=F32)
        blkp_ref[e] = p_row[0:1, :].astype(I32)
        for c in range(cap // chunk):
            slot = (lax.broadcasted_iota(I32, (1, chunk), 1) + c * chunk).astype(F32)
            hot = (p_e <= slot) & (slot < p_end)
            w_slot = jnp.dot(wi_t, jnp.where(hot, 1.0, 0.0).astype(BF16), preferred_element_type=F32)
            p_slot = jnp.sum(jnp.where(hot, p_e, 0.0), axis=0, keepdims=True)
            b_slot = jnp.sum(jnp.where(hot, b_col, 0.0), axis=0, keepdims=True)
            lane = jnp.sum(jnp.where(w_slot <= slot - p_slot, 1.0, 0.0), axis=0, keepdims=True)
            idx_ref[e, :, c * chunk:(c + 1) * chunk] = (b_slot * LANES + lane).astype(I32)
        return carry

    lax.fori_loop(0, ne, per_expert, 0)


def _route(aff2, n, cap):
    nb = n // LANES
    return pl.pallas_call(
        functools.partial(_route_kernel, nb=nb, cap=cap),
        out_shape=[jax.ShapeDtypeStruct((N_EXPERTS * nb, LANES), I32),
                   jax.ShapeDtypeStruct((N_EXPERTS, 1, cap), I32),
                   jax.ShapeDtypeStruct((N_EXPERTS, 1, nb), I32)],
        scratch_shapes=[pltpu.VMEM((N_EXPERTS * nb, LANES), F32)] * 3,
        compiler_params=pltpu.CompilerParams(vmem_limit_bytes=VMEM_LIMIT),
        name="route",
    )(aff2)


def _ffn_kernel(idx_ref, x_hbm, wg_ref, wu_ref, wd_ref, y_ref, xbuf, sem, *, tm, tiles, n_steps):
    step = pl.program_id(0) * tiles + pl.program_id(1)
    slot = step & 1

    def fetch(src_step, dst_slot):
        for r in range(tm):
            pltpu.make_async_copy(x_hbm.at[pl.ds(idx_ref[src_step * tm + r], 1), :],
                                  xbuf.at[dst_slot, pl.ds(r, 1), :], sem.at[dst_slot]).start()

    def wait(which):
        pltpu.make_async_copy(x_hbm.at[pl.ds(0, tm), :], xbuf.at[which], sem.at[which]).wait()

    @pl.when(step == 0)
    def _():
        fetch(0, 0)

    fetch(jnp.minimum(step + 1, n_steps - 1), 1 - slot)
    wait(slot)
    x = xbuf[slot].astype(BF16)
    fc = 512
    acc = None
    for c in range(EXPERT_FF // fc):
        sl = slice(c * fc, (c + 1) * fc)
        a = jnp.dot(x, wg_ref[:, sl], preferred_element_type=F32)
        u = jnp.dot(x, wu_ref[:, sl], preferred_element_type=F32)
        mid = (a * _sigmoid(a) * u).astype(BF16)
        part = jnp.dot(mid, wd_ref[sl, :], preferred_element_type=F32)
        acc = part if acc is None else acc + part
    y_ref[...] = acc.astype(BF16)

    @pl.when(step == n_steps - 1)
    def _():
        wait(1 - slot)


def _ffn(idx_flat, x1, wg, wu, wd, cap):
    tm = min(ROW_TILE, cap)
    tiles = cap // tm
    return pl.pallas_call(
        functools.partial(_ffn_kernel, tm=tm, tiles=tiles, n_steps=N_EXPERTS * tiles),
        grid_spec=pltpu.PrefetchScalarGridSpec(
            num_scalar_prefetch=1,
            grid=(N_EXPERTS, tiles),
            in_specs=[pl.BlockSpec(memory_space=pl.ANY),
                      pl.BlockSpec((None, D_MODEL, EXPERT_FF), lambda e, i, idx: (e, 0, 0)),
                      pl.BlockSpec((None, D_MODEL, EXPERT_FF), lambda e, i, idx: (e, 0, 0)),
                      pl.BlockSpec((None, EXPERT_FF, D_MODEL), lambda e, i, idx: (e, 0, 0))],
            out_specs=pl.BlockSpec((tm, D_MODEL), lambda e, i, idx: (e * tiles + i, 0)),
            scratch_shapes=[pltpu.VMEM((2, tm, D_MODEL), F32), pltpu.SemaphoreType.DMA((2,))],
        ),
        out_shape=jax.ShapeDtypeStruct((N_EXPERTS * cap, D_MODEL), BF16),
        compiler_params=_params("arbitrary", "arbitrary"),
        name="ffn",
    )(idx_flat, x1, wg, wu, wd)


TOK_TILE = 128
ROW_ALIGN = 16
WIN = TOK_TILE + ROW_ALIGN
SPARSE_CNT = 32
SPARSE_WIN = SPARSE_CNT + ROW_ALIGN
GROUP = 4
assert GROUP * SPARSE_WIN <= MXU_DEPTH and N_EXPERTS % GROUP == 0


def _final_kernel(start_ref, sparse_ref, base_ref, rank_ref, aff_ref, g_ref, b_ref, ye_hbm, o_ref, buf, sem,
                  *, cap, n_steps):
    t = pl.program_id(0)
    slot = t & 1

    def window(step, e, rows):
        first = start_ref[step * N_EXPERTS + e]
        ws = jnp.minimum((first // ROW_ALIGN) * ROW_ALIGN, cap - rows)
        return pl.multiple_of(ws, ROW_ALIGN)

    def fetch(step, dst_slot):
        def start_copies(rows):
            for e in range(N_EXPERTS):
                pltpu.make_async_copy(ye_hbm.at[pl.ds(e * cap + window(step, e, rows), rows), :],
                                      buf.at[dst_slot, pl.ds(e * rows, rows), :], sem.at[dst_slot]).start()

        @pl.when(sparse_ref[step] == 1)
        def _():
            start_copies(SPARSE_WIN)

        @pl.when(sparse_ref[step] == 0)
        def _():
            start_copies(WIN)

    def wait(rows):
        pltpu.make_async_copy(ye_hbm.at[pl.ds(0, N_EXPERTS * rows), :],
                              buf.at[slot, pl.ds(0, N_EXPERTS * rows), :], sem.at[slot]).wait()

    @pl.when(t == 0)
    def _():
        fetch(0, 0)

    @pl.when(t + 1 < n_steps)
    def _():
        fetch(t + 1, 1 - slot)

    @pl.when(sparse_ref[t] == 1)
    def _():
        wait(SPARSE_WIN)
        gate = aff_ref[...]
        gate_hi = gate.astype(BF16).astype(F32)
        gate_lo = gate - gate_hi
        lane = lax.broadcasted_iota(I32, (1, GROUP * SPARSE_WIN), 1)
        total = base_ref[...]
        for grp in range(N_EXPERTS // GROUP):
            hi = jnp.zeros((TOK_TILE, GROUP * SPARSE_WIN), F32)
            lo = hi
            for k in range(GROUP):
                e = grp * GROUP + k
                rank = rank_ref[:, e:e + 1]
                target = jnp.where(rank >= 0, rank - window(t, e, SPARSE_WIN) + k * SPARSE_WIN, -1)
                hot = lane == target
                hi = jnp.where(hot, gate_hi[:, e:e + 1], hi)
                lo = jnp.where(hot, gate_lo[:, e:e + 1], lo)
            rows = buf[slot, grp * GROUP * SPARSE_WIN:(grp + 1) * GROUP * SPARSE_WIN, :]
            y = jnp.dot(jnp.concatenate([hi, lo], axis=0).astype(BF16), rows, preferred_element_type=F32)
            total = total + y[0:TOK_TILE] + y[TOK_TILE:2 * TOK_TILE]
        o_ref[...] = _layer_norm(total, g_ref[...], b_ref[...])

    @pl.when(sparse_ref[t] == 0)
    def _():
        wait(WIN)
        lane = lax.broadcasted_iota(I32, (1, WIN), 1)
        total = base_ref[...]
        for e in range(N_EXPERTS):
            hot = rank_ref[:, e:e + 1] == lane + window(t, e, WIN)
            y = jnp.dot(jnp.where(hot, 1.0, 0.0).astype(BF16), buf[slot, e * WIN:(e + 1) * WIN, :],
                        preferred_element_type=F32)
            total = total + aff_ref[:, e:e + 1] * y
        o_ref[...] = _layer_norm(total, g_ref[...], b_ref[...])


def _final(starts, sparse, base, rank_t, aff_t, g, b, ye, cap):
    n = base.shape[0]
    tm = TOK_TILE

    def rows(w):
        return pl.BlockSpec((tm, w), lambda i, *_: (i, 0))

    def full(shape):
        return pl.BlockSpec(shape, lambda i, *_: (0,) * len(shape))

    return pl.pallas_call(
        functools.partial(_final_kernel, cap=cap, n_steps=n // tm),
        grid_spec=pltpu.PrefetchScalarGridSpec(
            num_scalar_prefetch=2,
            grid=(n // tm,),
            in_specs=[rows(D_MODEL), rows(N_EXPERTS), rows(N_EXPERTS), full((1, D_MODEL)), full((1, D_MODEL)),
                      pl.BlockSpec(memory_space=pl.ANY)],
            out_specs=rows(D_MODEL),
            scratch_shapes=[pltpu.VMEM((2, N_EXPERTS * WIN, D_MODEL), BF16), pltpu.SemaphoreType.DMA((2,))],
        ),
        out_shape=jax.ShapeDtypeStruct((n, D_MODEL), F32),
        compiler_params=_params("arbitrary"),
        name="final",
    )(starts, sparse, base, rank_t, aff_t, g, b, ye)


def _rope_tables(seq):
    n_rows = seq // GRID_W
    rows = jnp.repeat(jnp.arange(n_rows, dtype=F32), GRID_W)
    cols = jnp.tile(jnp.arange(GRID_W, dtype=F32), n_rows)
    sec = HEAD_DIM // 2
    inv_freq = ROPE_THETA ** (-jnp.arange(0, sec, 2, dtype=F32) / sec)
    ang_r = rows[:, None] * inv_freq[None, :]
    ang_c = cols[:, None] * inv_freq[None, :]
    cos = jnp.concatenate([jnp.cos(ang_r)] * 2 + [jnp.cos(ang_c)] * 2, axis=-1)
    sin = jnp.concatenate([-jnp.sin(ang_r), jnp.sin(ang_r), -jnp.sin(ang_c), jnp.sin(ang_c)], axis=-1)
    return jnp.tile(cos, (1, 2)), jnp.tile(sin, (1, 2))


def _trunk(x, p, w):
    b, seq, _ = x.shape
    n = b * seq
    cap = max(1, CAPACITY_FACTOR * n // N_EXPERTS)
    assert seq % ROW_TILE == 0 and n % (8 * LANES) == 0 and cap % TOK_TILE == 0 and cap >= WIN
    cos_t, sin_t = _rope_tables(seq)

    h, qa, ka, va, qb, kb, vb, gates = _in_proj(
        x.reshape(n, D_MODEL), w["ln_emb_g"], w["ln_emb_b"], w["w_in"], w["q_norm_g"], w["k_norm_g"],
        cos_t, sin_t, w["seg"], seq)

    def per_batch(t):
        return t.reshape(t.shape[0], b, seq, t.shape[2])

    ya = _gqa_attention(per_batch(qa), per_batch(ka), per_batch(va))
    lam_init = 0.8 - 0.6 * math.exp(-0.3 * 0)
    yb = _diff_attention(per_batch(qb), per_batch(kb), per_batch(vb),
                         w["lambda_q1"], w["lambda_k1"], w["lambda_q2"], w["lambda_k2"], w["subln_g"], lam_init)

    x1, base, aff = _merge(ya.reshape(n, -1), yb.reshape(n, -1), gates, h, p.reshape(n, PLE_DIM),
                           w["w_branch_a"], w["w_branch_b"], w["w_out"], w["ln1_g"], w["ln1_b"],
                           w["wr_hi"], w["wr_lo"], w["w_ple_gate"], w["w_ple_proj"])

    nb = n // LANES
    rank2, idx, blkp = _route(aff.reshape(N_EXPERTS * nb, LANES), n, cap)
    ye = _ffn(idx.reshape(N_EXPERTS * cap), x1, w["w_gate"], w["w_up"], w["w_down"], cap)

    starts = blkp.reshape(N_EXPERTS, nb)[:, ::TOK_TILE // LANES]
    ends = jnp.concatenate([starts[:, 1:], jnp.full((N_EXPERTS, 1), cap, I32)], axis=1)
    sparse = (jnp.max(ends - starts, axis=0) <= SPARSE_CNT).astype(I32)
    out = _final(starts.T.reshape(-1), sparse, base, rank2.reshape(N_EXPERTS, n).T, aff.T,
                 w["ln2_g"], w["ln2_b"], ye, cap)
    return out.reshape(b, seq, D_MODEL)


def _prep_weights(ln_emb_g, ln_emb_b, w_in, q_norm_g, k_norm_g, lambda_q1, lambda_k1, lambda_q2, lambda_k2,
                  subln_g, w_branch_a, w_branch_b, w_out, ln1_g, ln1_b, w_router, w_gate, w_up, w_down,
                  w_ple_gate, w_ple_proj, ln2_g, ln2_b):
    i = 0
    seg_i = lax.broadcasted_iota(I32, (A_WIDTH, A_WIDTH), 0) // HEAD_DIM
    seg_j = lax.broadcasted_iota(I32, (A_WIDTH, A_WIDTH), 1) // HEAD_DIM
    wr_t = w_router[i].T
    wr_hi = wr_t.astype(BF16)
    return {
        "ln_emb_g": ln_emb_g.reshape(1, D_MODEL), "ln_emb_b": ln_emb_b.reshape(1, D_MODEL),
        "w_in": w_in[i].astype(BF16),
        "q_norm_g": jnp.tile(q_norm_g[i], 2).reshape(1, LANES), "k_norm_g": jnp.tile(k_norm_g[i], 2).reshape(1, LANES),
        "seg": (seg_i == seg_j).astype(BF16),
        "lambda_q1": lambda_q1[i].reshape(1, HEAD_DIM), "lambda_k1": lambda_k1[i].reshape(1, HEAD_DIM),
        "lambda_q2": lambda_q2[i].reshape(1, HEAD_DIM), "lambda_k2": lambda_k2[i].reshape(1, HEAD_DIM),
        "subln_g": subln_g[i].reshape(1, B_VDIM),
        "w_branch_a": w_branch_a[i].astype(BF16), "w_branch_b": w_branch_b[i].astype(BF16),
        "w_out": w_out[i].astype(BF16),
        "ln1_g": ln1_g[i].reshape(1, D_MODEL), "ln1_b": ln1_b[i].reshape(1, D_MODEL),
        "wr_hi": wr_hi, "wr_lo": (wr_t - wr_hi.astype(F32)).astype(BF16),
        "w_gate": w_gate[i].astype(BF16), "w_up": w_up[i].astype(BF16), "w_down": w_down[i].astype(BF16),
        "w_ple_gate": w_ple_gate[i].astype(BF16), "w_ple_proj": w_ple_proj[i].astype(BF16),
        "ln2_g": ln2_g[i].reshape(1, D_MODEL), "ln2_b": ln2_b[i].reshape(1, D_MODEL),
    }


def kernel(x_prompt, x_sample, p_prompt, p_sample, ln_emb_g, ln_emb_b, w_in, q_norm_g, k_norm_g, lambda_q1, lambda_k1, lambda_q2, lambda_k2, subln_g, w_branch_a, w_branch_b, w_out, ln1_g, ln1_b, w_router, w_gate, w_up, w_down, w_ple_gate, w_ple_proj, ln2_g, ln2_b):
    w = _prep_weights(ln_emb_g, ln_emb_b, w_in, q_norm_g, k_norm_g, lambda_q1, lambda_k1, lambda_q2, lambda_k2,
                      subln_g, w_branch_a, w_branch_b, w_out, ln1_g, ln1_b, w_router, w_gate, w_up, w_down,
                      w_ple_gate, w_ple_proj, ln2_g, ln2_b)
    return (_trunk(x_prompt, p_prompt[0], w), _trunk(x_sample, p_sample[0], w))
```

```python
import functools
import math

import jax
import jax.numpy as jnp
from jax import lax
from jax.experimental import pallas as pl
from jax.experimental.pallas import tpu as pltpu

F32 = jnp.float32
BF16 = jnp.bfloat16
I32 = jnp.int32

D_MODEL = 1024
DEPTH = 1
HEAD_DIM = 64
A_HEADS = 8
A_KV_HEADS = 2
A_GROUP = A_HEADS // A_KV_HEADS
A_WIDTH = A_HEADS * HEAD_DIM
B_HEADS = 4
B_VDIM = 2 * HEAD_DIM
B_WIDTH = B_HEADS * B_VDIM
GRID_W = 64
ROPE_THETA = 10000.0
N_EXPERTS = 16
EXPERT_FF = 2048
CAPACITY_FACTOR = 2
PLE_DIM = 256
LN_EPS = 1e-5
RMS_EPS = 1e-6
ALPHA = (2 * DEPTH) ** 0.25
LOG2E = math.log2(math.e)
QSCALE = HEAD_DIM ** -0.5 * LOG2E
LANES = 128
ROPE_HALF = HEAD_DIM // 4

C_QA = 0
C_KA = C_QA + A_WIDTH
C_VA = C_KA + A_KV_HEADS * HEAD_DIM
C_QB = C_VA + A_KV_HEADS * HEAD_DIM
C_KB = C_QB + 2 * B_HEADS * HEAD_DIM
C_VB = C_KB + 2 * B_HEADS * HEAD_DIM
C_G = C_VB + B_WIDTH
IN_COLS = C_G + 2 * D_MODEL

VMEM_LIMIT = 56 * 1024 * 1024
MXU_DEPTH = 256

ROW_TILE = 512
IN_PROJ_PARTS = 2
GQA_TQ, GQA_TK = 256, 1024
DIFF_TQ, DIFF_TK, DIFF_SPAN = 512, 1024, 2048
NT_DIMS = (((1,), (1,)), ((), ()))


def _params(*sem):
    return pltpu.CompilerParams(dimension_semantics=sem, vmem_limit_bytes=VMEM_LIMIT)


def _full(shape):
    n = len(shape)
    return pl.BlockSpec(shape, lambda *_: (0,) * n, pipeline_mode=pl.Buffered(1))


def _layer_norm(x, g, b):
    mu = jnp.mean(x, axis=-1, keepdims=True)
    xc = x - mu
    var = jnp.mean(xc * xc, axis=-1, keepdims=True)
    return xc * lax.rsqrt(var + LN_EPS) * g + b


def _sigmoid(x):
    return 1.0 / (1.0 + jnp.exp(-x))


def _split_bf16(x):
    hi = x.astype(BF16)
    lo = (x - hi.astype(F32)).astype(BF16)
    return hi, lo


def _head_mean_sq(x, seg):
    hi, lo = _split_bf16(x * x)
    s = jnp.dot(hi, seg, preferred_element_type=F32) + jnp.dot(lo, seg, preferred_element_type=F32)
    return s * (1.0 / HEAD_DIM)


def _rope(x, cos, sin, first_half):
    fwd = pltpu.roll(x, LANES - ROPE_HALF, 1)
    bwd = pltpu.roll(x, ROPE_HALF, 1)
    return x * cos + jnp.where(first_half, fwd, bwd) * sin


def _in_proj_kernel(x_ref, lng_ref, lnb_ref, w_ref, qg_ref, kg_ref, cos_ref, sin_ref, seg_ref, qc_ref, kc_ref,
                    h_ref, qa_ref, ka_ref, va_ref, qb_ref, kb_ref, vb_ref, gate_ref, *, tiles_per_seq):
    n_rows = x_ref.shape[0]
    part = n_rows // IN_PROJ_PARTS
    lane = lax.broadcasted_iota(I32, (part, LANES), 1)
    first_half = (lane & ROPE_HALF) == 0
    one_at_0 = jnp.where(lane == 0, 1.0, 0.0)
    pair_of = lane >> 1

    for r0 in range(0, n_rows, part):
        rs = slice(r0, r0 + part)
        xn = _layer_norm(x_ref[rs, :], lng_ref[...], lnb_ref[...])
        h_ref[rs, :] = xn
        xb = xn.astype(BF16)

        def proj(lo, hi):
            return jnp.dot(xb, w_ref[:, lo:hi], preferred_element_type=F32)

        def store_heads(ref, x):
            for i in range(x.shape[1] // HEAD_DIM):
                ref[i, rs, :] = x[:, i * HEAD_DIM:(i + 1) * HEAD_DIM].astype(BF16)

        cos = cos_ref[rs, :]
        sin = sin_ref[rs, :]

        qa = proj(C_QA, C_KA)
        qn = qa * lax.rsqrt(_head_mean_sq(qa, seg_ref[...]) + RMS_EPS)
        for c in range(A_WIDTH // LANES):
            pair = _rope(qn[:, c * LANES:(c + 1) * LANES] * qg_ref[...], cos, sin, first_half) * QSCALE
            qa_ref[2 * c, rs, :] = pair[:, 0:HEAD_DIM].astype(BF16)
            qa_ref[2 * c + 1, rs, :] = pair[:, HEAD_DIM:LANES].astype(BF16)

        ka = proj(C_KA, C_VA)
        kn = ka * lax.rsqrt(_head_mean_sq(ka, seg_ref[0:LANES, 0:LANES]) + RMS_EPS)
        store_heads(ka_ref, _rope(kn * kg_ref[...], cos, sin, first_half))
        va = proj(C_VA, C_QB)
        for g in range(A_KV_HEADS):
            v_first = va if g == 0 else pltpu.roll(va, HEAD_DIM, 1)
            va_ref[g, rs, :] = jnp.where(lane < HEAD_DIM, v_first, jnp.where(lane == HEAD_DIM, 1.0, 0.0)).astype(BF16)
        pos = ((pl.program_id(0) % tiles_per_seq) * n_rows + r0) + lax.broadcasted_iota(I32, (part, LANES), 0)
        hi = (pos >> 7).astype(F32)
        lo = (pos & (LANES - 1)).astype(F32)
        q_pos = jnp.where(pair_of == 32, hi, jnp.where(pair_of == 33, lo, 0.0))
        k_pos = jnp.where(pair_of == 34, hi, jnp.where(pair_of == 35, lo, 0.0))

        def store_with_features(ref, x, pos_feat, const_ref):
            for i in range(2 * B_HEADS):
                chunk = x[:, (i // 2) * LANES:(i // 2 + 1) * LANES]
                first = chunk if i % 2 == 0 else pltpu.roll(chunk, HEAD_DIM, 1)
                feat = pos_feat + const_ref[i % B_HEADS:i % B_HEADS + 1, :]
                ref[i, rs, :] = jnp.where(lane < HEAD_DIM, first, feat).astype(BF16)

        store_with_features(qb_ref, proj(C_QB, C_KB) * QSCALE, q_pos, qc_ref)
        store_with_features(kb_ref, proj(C_KB, C_VB), k_pos, kc_ref)
        vb = proj(C_VB, C_G)
        for i in range(B_HEADS):
            vb_ref[i, rs, 0:B_VDIM] = vb[:, i * B_VDIM:(i + 1) * B_VDIM].astype(BF16)
            vb_ref[i, rs, B_VDIM:2 * B_VDIM] = one_at_0.astype(BF16)
        gate_ref[rs, :] = proj(C_G, IN_COLS)


def _alibi_feature_constants():
    c = jnp.asarray([2.0 ** (-8.0 * (h + 1) / B_HEADS) * LOG2E for h in range(B_HEADS)], F32)
    c1 = c.astype(BF16).astype(F32)
    c2 = (c - c1).astype(BF16).astype(F32)
    parts = jnp.stack([LANES * c1, LANES * c2, c1, c2], axis=1)
    zeros = jnp.zeros((B_HEADS, LANES), F32)
    return (zeros.at[:, HEAD_DIM + 4:HEAD_DIM + 8].set(-parts), zeros.at[:, HEAD_DIM:HEAD_DIM + 4].set(parts))


def _in_proj(x2, lng, lnb, w_in, qg, kg, cos_t, sin_t, seg, seq):
    n = x2.shape[0]
    tm = ROW_TILE
    tiles_per_seq = seq // tm

    def rows(w):
        return pl.BlockSpec((tm, w), lambda i: (i, 0))

    def heads(nh, w):
        return pl.BlockSpec((nh, tm, w), lambda i: (0, i, 0))

    tab = pl.BlockSpec((tm, LANES), lambda i: (i % tiles_per_seq, 0))
    head_outs = [(A_HEADS, HEAD_DIM), (A_KV_HEADS, HEAD_DIM), (A_KV_HEADS, LANES),
                 (2 * B_HEADS, LANES), (2 * B_HEADS, LANES), (B_HEADS, 2 * B_VDIM)]
    q_const, k_const = _alibi_feature_constants()
    return pl.pallas_call(
        functools.partial(_in_proj_kernel, tiles_per_seq=tiles_per_seq),
        grid=(n // tm,),
        in_specs=[rows(D_MODEL), _full((1, D_MODEL)), _full((1, D_MODEL)), _full((D_MODEL, IN_COLS)),
                  _full((1, LANES)), _full((1, LANES)), tab, tab, _full((A_WIDTH, A_WIDTH)),
                  _full((B_HEADS, LANES)), _full((B_HEADS, LANES))],
        out_specs=[rows(D_MODEL)] + [heads(nh, w) for nh, w in head_outs] + [rows(2 * D_MODEL)],
        out_shape=([jax.ShapeDtypeStruct((n, D_MODEL), F32)]
                   + [jax.ShapeDtypeStruct((nh, n, w), BF16) for nh, w in head_outs]
                   + [jax.ShapeDtypeStruct((n, 2 * D_MODEL), F32)]),
        compiler_params=_params("parallel"),
        name="in_proj",
    )(x2, lng, lnb, w_in, qg, kg, cos_t, sin_t, seg, q_const, k_const)


PIPE_SLOTS = 2


def _run_units(n_units, scores, softmax, values):
    scores(0, 0)
    scores(1, 1)
    softmax(0, 0)

    def two_units(i, carry):
        t = PIPE_SLOTS * (i + 1)
        scores(t, 0)
        softmax(t - 1, 1)
        values(t - 2, 0)
        scores(t + 1, 1)
        softmax(t, 0)
        values(t - 1, 1)
        return carry

    lax.fori_loop(0, (n_units >> 1) - 1, two_units, 0)
    softmax(n_units - 1, 1)
    values(n_units - 2, 0)
    values(n_units - 1, 1)


def _chunk_max(s):
    cm = s[:, 0:LANES]
    for c in range(1, s.shape[1] // LANES):
        cm = jnp.maximum(cm, s[:, c * LANES:(c + 1) * LANES])
    return cm


def _softmax_update(load_s, m_ref, i, chunk_max=None):
    m_prev = m_ref[i]
    row_max = jnp.max(load_s() if chunk_max is None else chunk_max, axis=1, keepdims=True)
    m_new = jnp.maximum(m_prev, row_max)
    alpha = jnp.exp2(m_prev - m_new)
    s = load_s()
    p = jnp.exp2(s - jnp.tile(m_new, (1, s.shape[1] // LANES)))
    m_ref[i] = m_new
    return p.astype(BF16), alpha


def _init_softmax_state(m_ref, acc_ref):
    m_ref[...] = jnp.full(m_ref.shape, -jnp.inf, F32)
    acc_ref[...] = jnp.zeros(acc_ref.shape, F32)


def _gqa_kernel(q_ref, k_ref, v_ref, o_ref, s_buf, p_buf, a_buf, x_buf, m_ref, acc_ref, *, tq, tk, n_sub, n_steps):
    step = pl.program_id(2)

    @pl.when(step == 0)
    def _():
        _init_softmax_state(m_ref, acc_ref)

    def keys(u):
        return u & (A_KV_HEADS - 1), pl.ds(pl.multiple_of((u >> 1) * tk, tk), tk)

    def scores(u, slot):
        g, rows = keys(u)
        q = q_ref[pl.ds(g * A_GROUP, A_GROUP)].reshape(A_GROUP * tq, HEAD_DIM)
        s = lax.dot_general(q, k_ref[g, rows, :], NT_DIMS, preferred_element_type=F32)
        s_buf[slot] = s
        x_buf[slot] = _chunk_max(s)

    def softmax(u, slot):
        p, alpha = _softmax_update(lambda: s_buf[slot], m_ref, u & (A_KV_HEADS - 1), chunk_max=x_buf[slot])
        p_buf[slot] = p
        a_buf[slot] = alpha

    def values(u, slot):
        g, rows = keys(u)
        pv = jnp.dot(p_buf[slot], v_ref[g, rows, :], preferred_element_type=F32)
        acc_ref[g] = a_buf[slot] * acc_ref[g] + pv

    _run_units(n_sub * A_KV_HEADS, scores, softmax, values)

    @pl.when(step == n_steps - 1)
    def _():
        for h in range(A_HEADS):
            g, r = divmod(h, A_GROUP)
            acc = acc_ref[g, r * tq:(r + 1) * tq, :]
            y = acc[:, 0:HEAD_DIM] / acc[:, HEAD_DIM:HEAD_DIM + 1]
            o_ref[:, h * HEAD_DIM:(h + 1) * HEAD_DIM] = y.astype(BF16)


def _gqa_attention(qa, ka, va):
    _, b, seq, _ = qa.shape
    tq = min(GQA_TQ, seq)
    tk = min(GQA_TK, seq)
    span = seq
    n_steps = seq // span
    rows = A_GROUP * tq
    resident = pl.Buffered(1)
    return pl.pallas_call(
        functools.partial(_gqa_kernel, tq=tq, tk=tk, n_sub=span // tk, n_steps=n_steps),
        grid=(b, seq // tq, n_steps),
        in_specs=[pl.BlockSpec((A_HEADS, None, tq, HEAD_DIM), lambda bi, qi, si: (0, bi, qi, 0)),
                  pl.BlockSpec((A_KV_HEADS, None, span, HEAD_DIM), lambda bi, qi, si: (0, bi, si, 0),
                               pipeline_mode=resident),
                  pl.BlockSpec((A_KV_HEADS, None, span, LANES), lambda bi, qi, si: (0, bi, si, 0),
                               pipeline_mode=resident)],
        out_specs=pl.BlockSpec((None, tq, A_WIDTH), lambda bi, qi, si: (bi, qi, 0)),
        out_shape=jax.ShapeDtypeStruct((b, seq, A_WIDTH), BF16),
        scratch_shapes=[pltpu.VMEM((PIPE_SLOTS, rows, tk), F32), pltpu.VMEM((PIPE_SLOTS, rows, tk), BF16),
                        pltpu.VMEM((PIPE_SLOTS, rows, LANES), F32), pltpu.VMEM((PIPE_SLOTS, rows, LANES), F32),
                        pltpu.VMEM((A_KV_HEADS, rows, LANES), F32), pltpu.VMEM((A_KV_HEADS, rows, LANES), F32)],
        compiler_params=_params("parallel", "parallel", "arbitrary"),
        name="gqa_attn",
    )(qa, ka, va)


def _diff_kernel(slope_ref, q_ref, k_ref, v_ref, lq1_ref, lk1_ref, lq2_ref, lk2_ref, g_ref, o_ref,
                 s_buf, p_buf, a_buf, x_buf, m_ref, acc_ref, *, tq, tk, span, n_steps, lam_init):
    qi = pl.program_id(1)
    step = pl.program_id(2)
    n_sub = span // tk
    q_pos = qi * tq
    diag_step = q_pos // span
    diag_sub = (q_pos // tk) % n_sub
    lane = lax.broadcasted_iota(I32, (1, LANES), 1)

    @pl.when(step == 0)
    def _():
        _init_softmax_state(m_ref, acc_ref)

    @pl.when(step == diag_step)
    def _():
        rows = pl.ds(pl.multiple_of(diag_sub * tk, tk), tk)
        offset = q_pos - (q_pos // tk) * tk
        dist = jnp.abs(lax.broadcasted_iota(I32, (tq, tk), 0) - lax.broadcasted_iota(I32, (tq, tk), 1) + offset)
        dist = dist.astype(F32)
        for h in range(B_HEADS):
            bias = dist * slope_ref[h]
            v = v_ref[h, rows, :]
            for m in range(2):
                i = m * B_HEADS + h
                s = lax.dot_general(q_ref[i][:, 0:HEAD_DIM], k_ref[i, rows, 0:HEAD_DIM], NT_DIMS,
                                    preferred_element_type=F32) - bias
                p, alpha = _softmax_update(lambda: s, m_ref, i)
                acc_ref[i] = jnp.tile(alpha, (1, 2)) * acc_ref[i] + jnp.dot(p, v, preferred_element_type=F32)

    def keys(u):
        j = u >> 2
        j = j + jnp.where((step == diag_step) & (j >= diag_sub), 1, 0)
        return u & (B_HEADS - 1), j, pl.ds(pl.multiple_of(j * tk, tk), tk)

    def scores(u, slot):
        h, j, rows = keys(u)
        keys_before = q_pos >= step * span + (j + 1) * tk
        sign = jnp.where(keys_before, -1.0, 1.0)
        lane_scale = jnp.where(lane < HEAD_DIM, 1.0, sign).astype(BF16)
        for m in range(2):
            i = m * B_HEADS + h
            s = lax.dot_general(q_ref[i] * lane_scale, k_ref[i, rows, :], NT_DIMS, preferred_element_type=F32)
            s_buf[slot, m] = s
            x_buf[slot, m] = _chunk_max(s)

    def softmax(u, slot):
        h = u & (B_HEADS - 1)
        for m in range(2):
            p, alpha = _softmax_update(lambda: s_buf[slot, m], m_ref, m * B_HEADS + h, chunk_max=x_buf[slot, m])
            p_buf[slot, m] = p
            a_buf[slot, m] = alpha

    def values(u, slot):
        h, _, rows = keys(u)
        v = v_ref[h, rows, :]
        for m in range(2):
            i = m * B_HEADS + h
            pv = jnp.dot(p_buf[slot, m], v, preferred_element_type=F32)
            acc_ref[i] = jnp.tile(a_buf[slot, m], (1, 2)) * acc_ref[i] + pv

    _run_units(jnp.where(step == diag_step, n_sub - 1, n_sub) * B_HEADS, scores, softmax, values)

    @pl.when(step == n_steps - 1)
    def _():
        lam = (jnp.exp(jnp.sum(lq1_ref[...] * lk1_ref[...], axis=1, keepdims=True))
               - jnp.exp(jnp.sum(lq2_ref[...] * lk2_ref[...], axis=1, keepdims=True)) + lam_init)

        def normalized(i):
            acc = acc_ref[i]
            return acc[:, 0:B_VDIM] / acc[:, B_VDIM:B_VDIM + 1]

        for h in range(B_HEADS):
            y = normalized(h) - lam * normalized(B_HEADS + h)
            ms = jnp.mean(y * y, axis=1, keepdims=True)
            y = y * lax.rsqrt(ms + RMS_EPS) * g_ref[...] * (1.0 - lam_init)
            o_ref[:, h * B_VDIM:(h + 1) * B_VDIM] = y.astype(BF16)


def _diff_attention(qb, kb, vb, lq1, lk1, lq2, lk2, subln_g, lam_init):
    _, b, seq, _ = qb.shape
    tq = min(DIFF_TQ, seq)
    tk = min(DIFF_TK, seq)
    span = min(DIFF_SPAN, seq)
    n_steps = seq // span
    assert tk % tq == 0 and span // tk >= 2
    nh = 2 * B_HEADS
    slopes = jnp.asarray([2.0 ** (-8.0 * (h + 1) / B_HEADS) * LOG2E for h in range(B_HEADS)], F32)
    vec = pl.BlockSpec((1, HEAD_DIM), lambda bi, qi, si, *_: (0, 0))
    return pl.pallas_call(
        functools.partial(_diff_kernel, tq=tq, tk=tk, span=span, n_steps=n_steps, lam_init=lam_init),
        grid_spec=pltpu.PrefetchScalarGridSpec(
            num_scalar_prefetch=1,
            grid=(b, seq // tq, n_steps),
            in_specs=[pl.BlockSpec((nh, None, tq, LANES), lambda bi, qi, si, *_: (0, bi, qi, 0)),
                      pl.BlockSpec((nh, None, span, LANES), lambda bi, qi, si, *_: (0, bi, si, 0)),
                      pl.BlockSpec((B_HEADS, None, span, 2 * B_VDIM), lambda bi, qi, si, *_: (0, bi, si, 0)),
                      vec, vec, vec, vec, pl.BlockSpec((1, B_VDIM), lambda bi, qi, si, *_: (0, 0))],
            out_specs=pl.BlockSpec((None, tq, B_WIDTH), lambda bi, qi, si, *_: (bi, qi, 0)),
            scratch_shapes=[pltpu.VMEM((PIPE_SLOTS, 2, tq, tk), F32), pltpu.VMEM((PIPE_SLOTS, 2, tq, tk), BF16),
                            pltpu.VMEM((PIPE_SLOTS, 2, tq, LANES), F32), pltpu.VMEM((PIPE_SLOTS, 2, tq, LANES), F32),
                            pltpu.VMEM((nh, tq, LANES), F32), pltpu.VMEM((nh, tq, 2 * B_VDIM), F32)],
        ),
        out_shape=jax.ShapeDtypeStruct((b, seq, B_WIDTH), BF16),
        compiler_params=_params("parallel", "parallel", "arbitrary"),
        name="diff_attn",
    )(slopes, qb, kb, vb, lq1, lk1, lq2, lk2, subln_g)


def _merge_kernel(ya_ref, yb_ref, gate_ref, h_ref, p_ref, wa_ref, wb_ref, wo_ref, g_ref, b_ref,
                  wrh_ref, wrl_ref, wpg_ref, wpp_ref, x1_ref, base_ref, aff_ref):
    ma = jnp.dot(ya_ref[...], wa_ref[...], preferred_element_type=F32)
    mb = jnp.dot(yb_ref[...], wb_ref[...], preferred_element_type=F32)
    merged = _sigmoid(gate_ref[:, 0:D_MODEL]) * ma + _sigmoid(gate_ref[:, D_MODEL:2 * D_MODEL]) * mb
    out = jnp.dot(merged.astype(BF16), wo_ref[...], preferred_element_type=F32)
    x1 = _layer_norm(ALPHA * h_ref[...] + out, g_ref[...], b_ref[...])
    x1_ref[...] = x1
    xh, xl = _split_bf16(x1)
    ple_gate = _sigmoid(jnp.dot(xh, wpg_ref[...], preferred_element_type=F32))
    base_ref[...] = ALPHA * x1 + ple_gate * jnp.dot(p_ref[...].astype(BF16), wpp_ref[...], preferred_element_type=F32)
    wh = wrh_ref[...]
    logits = (lax.dot_general(wh, xh, NT_DIMS, preferred_element_type=F32)
              + lax.dot_general(wh, xl, NT_DIMS, preferred_element_type=F32)
              + lax.dot_general(wrl_ref[...], xh, NT_DIMS, preferred_element_type=F32))
    e = jnp.exp(logits - jnp.max(logits, axis=0, keepdims=True))
    aff_ref[...] = e / jnp.sum(e, axis=0, keepdims=True)


def _merge(ya, yb, gates, h, p2, wa, wb, wo, g, b, wrh, wrl, wpg, wpp):
    n = h.shape[0]
    tm = ROW_TILE

    def rows(w):
        return pl.BlockSpec((tm, w), lambda i: (i, 0))

    return pl.pallas_call(
        _merge_kernel,
        grid=(n // tm,),
        in_specs=[rows(A_WIDTH), rows(B_WIDTH), rows(2 * D_MODEL), rows(D_MODEL), rows(PLE_DIM),
                  _full((A_WIDTH, D_MODEL)), _full((B_WIDTH, D_MODEL)), _full((D_MODEL, D_MODEL)),
                  _full((1, D_MODEL)), _full((1, D_MODEL)),
                  _full((N_EXPERTS, D_MODEL)), _full((N_EXPERTS, D_MODEL)),
                  _full((D_MODEL, D_MODEL)), _full((PLE_DIM, D_MODEL))],
        out_specs=[rows(D_MODEL), rows(D_MODEL), pl.BlockSpec((N_EXPERTS, tm), lambda i: (0, i))],
        out_shape=[jax.ShapeDtypeStruct((n, D_MODEL), F32), jax.ShapeDtypeStruct((n, D_MODEL), F32),
                   jax.ShapeDtypeStruct((N_EXPERTS, n), F32)],
        compiler_params=_params("parallel"),
        name="merge",
    )(ya, yb, gates, h, p2, wa, wb, wo, g, b, wrh, wrl, wpg, wpp)


def _route_kernel(aff_ref, rank_ref, idx_ref, blkp_ref, sel_scr, wi_scr, p_scr, *, nb, cap):
    ne = N_EXPERTS
    shape3 = (ne, nb, LANES)
    bits = pltpu.bitcast(aff_ref[...], I32).reshape(shape3)
    capf = float(cap)

    def count(mask3):
        part = jnp.sum(jnp.where(mask3, 1.0, 0.0), axis=1, keepdims=True)
        return jnp.sum(part, axis=2, keepdims=True)

    def search(i, thr):
        cand = thr | jnp.left_shift(jnp.int32(1), 30 - i)
        return jnp.where(count(bits >= cand) >= capf, cand, thr)

    thr = lax.fori_loop(0, 31, search, jnp.zeros((ne, 1, 1), I32))
    gt = bits > thr
    eq = bits == thr
    need = capf - count(gt)

    li = lax.broadcasted_iota(I32, (LANES, LANES), 0)
    lj = lax.broadcasted_iota(I32, (LANES, LANES), 1)
    upper_incl = jnp.where(li <= lj, 1.0, 0.0).astype(BF16)
    lower_incl = jnp.where(lj <= li, 1.0, 0.0).astype(BF16)
    bi = lax.broadcasted_iota(I32, (nb, nb), 0)
    bj = lax.broadcasted_iota(I32, (nb, nb), 1)
    lower_strict = jnp.where(bj < bi, 1.0, 0.0).astype(BF16)
    upper_strict = jnp.where(bi < bj, 1.0, 0.0).astype(BF16)

    def prefix(mask2):
        wi = jnp.dot(mask2.astype(BF16), upper_incl, preferred_element_type=F32)
        tot = jnp.broadcast_to(wi[:, LANES - 1:LANES], wi.shape).astype(BF16)
        blk = [jnp.dot(lower_strict, tot[e * nb:(e + 1) * nb], preferred_element_type=F32)
               for e in range(ne)]
        return wi, jnp.concatenate(blk, axis=0)

    eqf = jnp.where(eq, 1.0, 0.0).reshape(ne * nb, LANES)
    wi, bp = prefix(eqf)
    eq_rank = (wi + bp - eqf).reshape(shape3)
    sel = gt | (eq & (eq_rank < need))
    self_ = jnp.where(sel, 1.0, 0.0).reshape(ne * nb, LANES)
    wi, bp = prefix(self_)
    rank = wi + bp - self_
    rank_ref[...] = jnp.where(self_ > 0.0, rank, -1.0).astype(I32)
    sel_scr[...] = self_
    wi_scr[...] = wi
    p_scr[...] = bp

    chunk = min(cap, 1024)
    ones8 = jnp.ones((8, LANES), BF16)
    b_col = lax.broadcasted_iota(I32, (nb, 1), 0).astype(F32)

    def per_expert(e, carry):
        r0 = pl.multiple_of(e * nb, 8)
        sel_e = sel_scr[pl.ds(r0, nb), :].astype(BF16)
        wi_e = wi_scr[pl.ds(r0, nb), :]
        p_e = p_scr[pl.ds(r0, nb), :][:, 0:1]
        p_end = p_e + wi_e[:, LANES - 1:LANES]
        wi_t = lax.dot_general(lower_incl, sel_e, NT_DIMS, preferred_element_type=F32).astype(BF16)
        tot_row = lax.dot_general(ones8, sel_e, NT_DIMS, preferred_element_type=F32)
        p_row = jnp.dot(tot_row.astype(BF16), upper_strict, preferred_element_type=F32)
        blkp_ref[e] = p_row[0:1, :].astype(I32)
        for c in range(cap // chunk):
            slot = (lax.broadcasted_iota(I32, (1, chunk), 1) + c * chunk).astype(F32)
            hot = (p_e <= slot) & (slot < p_end)
            w_slot = jnp.dot(wi_t, jnp.where(hot, 1.0, 0.0).astype(BF16), preferred_element_type=F32)
            p_slot = jnp.sum(jnp.where(hot, p_e, 0.0), axis=0, keepdims=True)
            b_slot = jnp.sum(jnp.where(hot, b_col, 0.0), axis=0, keepdims=True)
            lane = jnp.sum(jnp.where(w_slot <= slot - p_slot, 1.0, 0.0), axis=0, keepdims=True)
            idx_ref[e, :, c * chunk:(c + 1) * chunk] = (b_slot * LANES + lane).astype(I32)
        return carry

    lax.fori_loop(0, ne, per_expert, 0)


def _route(aff2, n, cap):
    nb = n // LANES
    return pl.pallas_call(
        functools.partial(_route_kernel, nb=nb, cap=cap),
        out_shape=[jax.ShapeDtypeStruct((N_EXPERTS * nb, LANES), I32),
                   jax.ShapeDtypeStruct((N_EXPERTS, 1, cap), I32),
                   jax.ShapeDtypeStruct((N_EXPERTS, 1, nb), I32)],
        scratch_shapes=[pltpu.VMEM((N_EXPERTS * nb, LANES), F32)] * 3,
        compiler_params=pltpu.CompilerParams(vmem_limit_bytes=VMEM_LIMIT),
        name="route",
    )(aff2)


FFN_SLOTS = 3


def _ffn_kernel(idx_ref, x_hbm, wg_ref, wu_ref, wd_ref, y_ref, xbuf, sem, *, tm, tiles, n_steps):
    step = pl.program_id(0) * tiles + pl.program_id(1)
    slot = lax.rem(step, FFN_SLOTS)

    def fetch(src_step, dst_slot):
        for r in range(tm):
            pltpu.make_async_copy(x_hbm.at[pl.ds(idx_ref[src_step * tm + r], 1), :],
                                  xbuf.at[dst_slot, pl.ds(r, 1), :], sem.at[dst_slot]).start()

    def wait(which):
        pltpu.make_async_copy(x_hbm.at[pl.ds(0, tm), :], xbuf.at[which], sem.at[which]).wait()

    @pl.when(step == 0)
    def _():
        fetch(0, 0)
        fetch(jnp.minimum(1, n_steps - 1), 1)

    wait(slot)
    x = xbuf[slot].astype(BF16)
    fc = 512
    acc = None
    for c in range(EXPERT_FF // fc):
        sl = slice(c * fc, (c + 1) * fc)
        a = jnp.dot(x, wg_ref[:, sl], preferred_element_type=F32)
        u = jnp.dot(x, wu_ref[:, sl], preferred_element_type=F32)
        mid = (a * _sigmoid(a) * u).astype(BF16)
        part = jnp.dot(mid, wd_ref[sl, :], preferred_element_type=F32)
        acc = part if acc is None else acc + part
    y_ref[...] = acc.astype(BF16)
    fetch(jnp.minimum(step + 2, n_steps - 1), lax.rem(step + 2, FFN_SLOTS))

    @pl.when(step == n_steps - 1)
    def _():
        wait(lax.rem(step + 1, FFN_SLOTS))
        wait(lax.rem(step + 2, FFN_SLOTS))


def _ffn(idx_flat, x1, wg, wu, wd, cap):
    tm = min(ROW_TILE, cap)
    tiles = cap // tm
    return pl.pallas_call(
        functools.partial(_ffn_kernel, tm=tm, tiles=tiles, n_steps=N_EXPERTS * tiles),
        grid_spec=pltpu.PrefetchScalarGridSpec(
            num_scalar_prefetch=1,
            grid=(N_EXPERTS, tiles),
            in_specs=[pl.BlockSpec(memory_space=pl.ANY),
                      pl.BlockSpec((None, D_MODEL, EXPERT_FF), lambda e, i, idx: (e, 0, 0)),
                      pl.BlockSpec((None, D_MODEL, EXPERT_FF), lambda e, i, idx: (e, 0, 0)),
                      pl.BlockSpec((None, EXPERT_FF, D_MODEL), lambda e, i, idx: (e, 0, 0))],
            out_specs=pl.BlockSpec((tm, D_MODEL), lambda e, i, idx: (e * tiles + i, 0)),
            scratch_shapes=[pltpu.VMEM((FFN_SLOTS, tm, D_MODEL), F32), pltpu.SemaphoreType.DMA((FFN_SLOTS,))],
        ),
        out_shape=jax.ShapeDtypeStruct((N_EXPERTS * cap, D_MODEL), BF16),
        compiler_params=_params("arbitrary", "arbitrary"),
        name="ffn",
    )(idx_flat, x1, wg, wu, wd)


TOK_TILE = 128
ROW_ALIGN = 16
WIN = TOK_TILE + ROW_ALIGN
SPARSE_CNT = 32
SPARSE_WIN = SPARSE_CNT + ROW_ALIGN
GROUP = 4
assert GROUP * SPARSE_WIN <= MXU_DEPTH and N_EXPERTS % GROUP == 0


def _final_kernel(start_ref, sparse_ref, base_ref, rank_ref, aff_ref, g_ref, b_ref, ye_hbm, o_ref, buf, sem,
                  *, cap, n_steps):
    t = pl.program_id(0)
    slot = t & 1

    def window(step, e, rows):
        first = start_ref[step * N_EXPERTS + e]
        ws = jnp.minimum((first // ROW_ALIGN) * ROW_ALIGN, cap - rows)
        return pl.multiple_of(ws, ROW_ALIGN)

    def fetch(step, dst_slot):
        def start_copies(rows):
            for e in range(N_EXPERTS):
                pltpu.make_async_copy(ye_hbm.at[pl.ds(e * cap + window(step, e, rows), rows), :],
                                      buf.at[dst_slot, pl.ds(e * rows, rows), :], sem.at[dst_slot]).start()

        @pl.when(sparse_ref[step] == 1)
        def _():
            start_copies(SPARSE_WIN)

        @pl.when(sparse_ref[step] == 0)
        def _():
            start_copies(WIN)

    def wait(rows):
        pltpu.make_async_copy(ye_hbm.at[pl.ds(0, N_EXPERTS * rows), :],
                              buf.at[slot, pl.ds(0, N_EXPERTS * rows), :], sem.at[slot]).wait()

    @pl.when(t == 0)
    def _():
        fetch(0, 0)

    @pl.when(t + 1 < n_steps)
    def _():
        fetch(t + 1, 1 - slot)

    @pl.when(sparse_ref[t] == 1)
    def _():
        wait(SPARSE_WIN)
        gate = aff_ref[...]
        gate_hi = gate.astype(BF16).astype(F32)
        gate_lo = gate - gate_hi
        lane = lax.broadcasted_iota(I32, (1, GROUP * SPARSE_WIN), 1)
        total = base_ref[...]
        for grp in range(N_EXPERTS // GROUP):
            hi = jnp.zeros((TOK_TILE, GROUP * SPARSE_WIN), F32)
            lo = hi
            for k in range(GROUP):
                e = grp * GROUP + k
                rank = rank_ref[:, e:e + 1]
                target = jnp.where(rank >= 0, rank - window(t, e, SPARSE_WIN) + k * SPARSE_WIN, -1)
                hot = lane == target
                hi = jnp.where(hot, gate_hi[:, e:e + 1], hi)
                lo = jnp.where(hot, gate_lo[:, e:e + 1], lo)
            rows = buf[slot, grp * GROUP * SPARSE_WIN:(grp + 1) * GROUP * SPARSE_WIN, :]
            y = jnp.dot(jnp.concatenate([hi, lo], axis=0).astype(BF16), rows, preferred_element_type=F32)
            total = total + y[0:TOK_TILE] + y[TOK_TILE:2 * TOK_TILE]
        o_ref[...] = _layer_norm(total, g_ref[...], b_ref[...])

    @pl.when(sparse_ref[t] == 0)
    def _():
        wait(WIN)
        lane = lax.broadcasted_iota(I32, (1, WIN), 1)
        total = base_ref[...]
        for e in range(N_EXPERTS):
            hot = rank_ref[:, e:e + 1] == lane + window(t, e, WIN)
            y = jnp.dot(jnp.where(hot, 1.0, 0.0).astype(BF16), buf[slot, e * WIN:(e + 1) * WIN, :],
                        preferred_element_type=F32)
            total = total + aff_ref[:, e:e + 1] * y
        o_ref[...] = _layer_norm(total, g_ref[...], b_ref[...])


def _final(starts, sparse, base, rank_t, aff_t, g, b, ye, cap):
    n = base.shape[0]
    tm = TOK_TILE

    def rows(w):
        return pl.BlockSpec((tm, w), lambda i, *_: (i, 0))

    def full(shape):
        return pl.BlockSpec(shape, lambda i, *_: (0,) * len(shape))

    return pl.pallas_call(
        functools.partial(_final_kernel, cap=cap, n_steps=n // tm),
        grid_spec=pltpu.PrefetchScalarGridSpec(
            num_scalar_prefetch=2,
            grid=(n // tm,),
            in_specs=[rows(D_MODEL), rows(N_EXPERTS), rows(N_EXPERTS), full((1, D_MODEL)), full((1, D_MODEL)),
                      pl.BlockSpec(memory_space=pl.ANY)],
            out_specs=rows(D_MODEL),
            scratch_shapes=[pltpu.VMEM((2, N_EXPERTS * WIN, D_MODEL), BF16), pltpu.SemaphoreType.DMA((2,))],
        ),
        out_shape=jax.ShapeDtypeStruct((n, D_MODEL), F32),
        compiler_params=_params("arbitrary"),
        name="final",
    )(starts, sparse, base, rank_t, aff_t, g, b, ye)


def _rope_tables(seq):
    n_rows = seq // GRID_W
    rows = jnp.repeat(jnp.arange(n_rows, dtype=F32), GRID_W)
    cols = jnp.tile(jnp.arange(GRID_W, dtype=F32), n_rows)
    sec = HEAD_DIM // 2
    inv_freq = ROPE_THETA ** (-jnp.arange(0, sec, 2, dtype=F32) / sec)
    ang_r = rows[:, None] * inv_freq[None, :]
    ang_c = cols[:, None] * inv_freq[None, :]
    cos = jnp.concatenate([jnp.cos(ang_r)] * 2 + [jnp.cos(ang_c)] * 2, axis=-1)
    sin = jnp.concatenate([-jnp.sin(ang_r), jnp.sin(ang_r), -jnp.sin(ang_c), jnp.sin(ang_c)], axis=-1)
    return jnp.tile(cos, (1, 2)), jnp.tile(sin, (1, 2))


def _trunk(x, p, w):
    b, seq, _ = x.shape
    n = b * seq
    cap = max(1, CAPACITY_FACTOR * n // N_EXPERTS)
    assert seq % ROW_TILE == 0 and n % (8 * LANES) == 0 and cap % TOK_TILE == 0 and cap >= WIN
    cos_t, sin_t = _rope_tables(seq)

    h, qa, ka, va, qb, kb, vb, gates = _in_proj(
        x.reshape(n, D_MODEL), w["ln_emb_g"], w["ln_emb_b"], w["w_in"], w["q_norm_g"], w["k_norm_g"],
        cos_t, sin_t, w["seg"], seq)

    def per_batch(t):
        return t.reshape(t.shape[0], b, seq, t.shape[2])

    ya = _gqa_attention(per_batch(qa), per_batch(ka), per_batch(va))
    lam_init = 0.8 - 0.6 * math.exp(-0.3 * 0)
    yb = _diff_attention(per_batch(qb), per_batch(kb), per_batch(vb),
                         w["lambda_q1"], w["lambda_k1"], w["lambda_q2"], w["lambda_k2"], w["subln_g"], lam_init)

    x1, base, aff = _merge(ya.reshape(n, -1), yb.reshape(n, -1), gates, h, p.reshape(n, PLE_DIM),
                           w["w_branch_a"], w["w_branch_b"], w["w_out"], w["ln1_g"], w["ln1_b"],
                           w["wr_hi"], w["wr_lo"], w["w_ple_gate"], w["w_ple_proj"])

    nb = n // LANES
    rank2, idx, blkp = _route(aff.reshape(N_EXPERTS * nb, LANES), n, cap)
    ye = _ffn(idx.reshape(N_EXPERTS * cap), x1, w["w_gate"], w["w_up"], w["w_down"], cap)

    starts = blkp.reshape(N_EXPERTS, nb)[:, ::TOK_TILE // LANES]
    ends = jnp.concatenate([starts[:, 1:], jnp.full((N_EXPERTS, 1), cap, I32)], axis=1)
    sparse = (jnp.max(ends - starts, axis=0) <= SPARSE_CNT).astype(I32)
    out = _final(starts.T.reshape(-1), sparse, base, rank2.reshape(N_EXPERTS, n).T, aff.T,
                 w["ln2_g"], w["ln2_b"], ye, cap)
    return out.reshape(b, seq, D_MODEL)


def _prep_weights(ln_emb_g, ln_emb_b, w_in, q_norm_g, k_norm_g, lambda_q1, lambda_k1, lambda_q2, lambda_k2,
                  subln_g, w_branch_a, w_branch_b, w_out, ln1_g, ln1_b, w_router, w_gate, w_up, w_down,
                  w_ple_gate, w_ple_proj, ln2_g, ln2_b):
    i = 0
    seg_i = lax.broadcasted_iota(I32, (A_WIDTH, A_WIDTH), 0) // HEAD_DIM
    seg_j = lax.broadcasted_iota(I32, (A_WIDTH, A_WIDTH), 1) // HEAD_DIM
    wr_t = w_router[i].T
    wr_hi = wr_t.astype(BF16)
    return {
        "ln_emb_g": ln_emb_g.reshape(1, D_MODEL), "ln_emb_b": ln_emb_b.reshape(1, D_MODEL),
        "w_in": w_in[i].astype(BF16),
        "q_norm_g": jnp.tile(q_norm_g[i], 2).reshape(1, LANES), "k_norm_g": jnp.tile(k_norm_g[i], 2).reshape(1, LANES),
        "seg": (seg_i == seg_j).astype(BF16),
        "lambda_q1": lambda_q1[i].reshape(1, HEAD_DIM), "lambda_k1": lambda_k1[i].reshape(1, HEAD_DIM),
        "lambda_q2": lambda_q2[i].reshape(1, HEAD_DIM), "lambda_k2": lambda_k2[i].reshape(1, HEAD_DIM),
        "subln_g": subln_g[i].reshape(1, B_VDIM),
        "w_branch_a": w_branch_a[i].astype(BF16), "w_branch_b": w_branch_b[i].astype(BF16),
        "w_out": w_out[i].astype(BF16),
        "ln1_g": ln1_g[i].reshape(1, D_MODEL), "ln1_b": ln1_b[i].reshape(1, D_MODEL),
        "wr_hi": wr_hi, "wr_lo": (wr_t - wr_hi.astype(F32)).astype(BF16),
        "w_gate": w_gate[i].astype(BF16), "w_up": w_up[i].astype(BF16), "w_down": w_down[i].astype(BF16),
        "w_ple_gate": w_ple_gate[i].astype(BF16), "w_ple_proj": w_ple_proj[i].astype(BF16),
        "ln2_g": ln2_g[i].reshape(1, D_MODEL), "ln2_b": ln2_b[i].reshape(1, D_MODEL),
    }


def kernel(x_prompt, x_sample, p_prompt, p_sample, ln_emb_g, ln_emb_b, w_in, q_norm_g, k_norm_g, lambda_q1, lambda_k1, lambda_q2, lambda_k2, subln_g, w_branch_a, w_branch_b, w_out, ln1_g, ln1_b, w_router, w_gate, w_up, w_down, w_ple_gate, w_ple_proj, ln2_g, ln2_b):
    w = _prep_weights(ln_emb_g, ln_emb_b, w_in, q_norm_g, k_norm_g, lambda_q1, lambda_k1, lambda_q2, lambda_k2,
                      subln_g, w_branch_a, w_branch_b, w_out, ln1_g, ln1_b, w_router, w_gate, w_up, w_down,
                      w_ple_gate, w_ple_proj, ln2_g, ln2_b)
    return (_trunk(x_prompt, p_prompt[0], w), _trunk(x_sample, p_sample[0], w))
```

```python
import functools
import math

import jax
import jax.numpy as jnp
from jax import lax
from jax.experimental import pallas as pl
from jax.experimental.pallas import tpu as pltpu

F32 = jnp.float32
BF16 = jnp.bfloat16
I32 = jnp.int32

D_MODEL = 1024
DEPTH = 1
HEAD_DIM = 64
A_HEADS = 8
A_KV_HEADS = 2
A_GROUP = A_HEADS // A_KV_HEADS
A_WIDTH = A_HEADS * HEAD_DIM
B_HEADS = 4
B_VDIM = 2 * HEAD_DIM
B_WIDTH = B_HEADS * B_VDIM
GRID_W = 64
ROPE_THETA = 10000.0
N_EXPERTS = 16
EXPERT_FF = 2048
CAPACITY_FACTOR = 2
PLE_DIM = 256
LN_EPS = 1e-5
RMS_EPS = 1e-6
ALPHA = (2 * DEPTH) ** 0.25
LOG2E = math.log2(math.e)
QSCALE = HEAD_DIM ** -0.5 * LOG2E
LANES = 128
ROPE_HALF = HEAD_DIM // 4

C_QA = 0
C_KA = C_QA + A_WIDTH
C_VA = C_KA + A_KV_HEADS * HEAD_DIM
C_QB = C_VA + A_KV_HEADS * HEAD_DIM
C_KB = C_QB + 2 * B_HEADS * HEAD_DIM
C_VB = C_KB + 2 * B_HEADS * HEAD_DIM
C_G = C_VB + B_WIDTH
IN_COLS = C_G + 2 * D_MODEL

VMEM_LIMIT = 56 * 1024 * 1024
MXU_DEPTH = 256

ROW_TILE = 512
IN_PROJ_PARTS = 2
GQA_TQ, GQA_TK = 256, 1024
DIFF_TQ, DIFF_TK, DIFF_SPAN = 512, 1024, 2048
NT_DIMS = (((1,), (1,)), ((), ()))


def _params(*sem):
    return pltpu.CompilerParams(dimension_semantics=sem, vmem_limit_bytes=VMEM_LIMIT)


def _full(shape):
    n = len(shape)
    return pl.BlockSpec(shape, lambda *_: (0,) * n, pipeline_mode=pl.Buffered(1))


def _layer_norm(x, g, b):
    mu = jnp.mean(x, axis=-1, keepdims=True)
    xc = x - mu
    var = jnp.mean(xc * xc, axis=-1, keepdims=True)
    return xc * lax.rsqrt(var + LN_EPS) * g + b


def _sigmoid(x):
    return 1.0 / (1.0 + jnp.exp(-x))


def _split_bf16(x):
    hi = x.astype(BF16)
    lo = (x - hi.astype(F32)).astype(BF16)
    return hi, lo


def _head_mean_sq(x, seg):
    hi, lo = _split_bf16(x * x)
    s = jnp.dot(hi, seg, preferred_element_type=F32) + jnp.dot(lo, seg, preferred_element_type=F32)
    return s * (1.0 / HEAD_DIM)


def _rope(x, cos, sin, first_half):
    fwd = pltpu.roll(x, LANES - ROPE_HALF, 1)
    bwd = pltpu.roll(x, ROPE_HALF, 1)
    return x * cos + jnp.where(first_half, fwd, bwd) * sin


def _in_proj_kernel(x_ref, lng_ref, lnb_ref, w_ref, qg_ref, kg_ref, cos_ref, sin_ref, seg_ref, qc_ref, kc_ref,
                    h_ref, qa_ref, ka_ref, va_ref, qb_ref, kb_ref, vb_ref, gate_ref, *, tiles_per_seq):
    n_rows = x_ref.shape[0]
    part = n_rows // IN_PROJ_PARTS
    lane = lax.broadcasted_iota(I32, (part, LANES), 1)
    first_half = (lane & ROPE_HALF) == 0
    one_at_0 = jnp.where(lane == 0, 1.0, 0.0)
    pair_of = lane >> 1

    for r0 in range(0, n_rows, part):
        rs = slice(r0, r0 + part)
        xn = _layer_norm(x_ref[rs, :], lng_ref[...], lnb_ref[...])
        h_ref[rs, :] = xn
        xb = xn.astype(BF16)

        def proj(lo, hi):
            return jnp.dot(xb, w_ref[:, lo:hi], preferred_element_type=F32)

        def store_heads(ref, x):
            for i in range(x.shape[1] // HEAD_DIM):
                ref[i, rs, :] = x[:, i * HEAD_DIM:(i + 1) * HEAD_DIM].astype(BF16)

        cos = cos_ref[rs, :]
        sin = sin_ref[rs, :]

        qa = proj(C_QA, C_KA)
        qn = qa * lax.rsqrt(_head_mean_sq(qa, seg_ref[...]) + RMS_EPS)
        for c in range(A_WIDTH // LANES):
            pair = _rope(qn[:, c * LANES:(c + 1) * LANES] * qg_ref[...], cos, sin, first_half) * QSCALE
            qa_ref[2 * c, rs, :] = pair[:, 0:HEAD_DIM].astype(BF16)
            qa_ref[2 * c + 1, rs, :] = pair[:, HEAD_DIM:LANES].astype(BF16)

        ka = proj(C_KA, C_VA)
        kn = ka * lax.rsqrt(_head_mean_sq(ka, seg_ref[0:LANES, 0:LANES]) + RMS_EPS)
        store_heads(ka_ref, _rope(kn * kg_ref[...], cos, sin, first_half))
        va = proj(C_VA, C_QB)
        for g in range(A_KV_HEADS):
            v_first = va if g == 0 else pltpu.roll(va, HEAD_DIM, 1)
            va_ref[g, rs, :] = jnp.where(lane < HEAD_DIM, v_first, jnp.where(lane == HEAD_DIM, 1.0, 0.0)).astype(BF16)
        pos = ((pl.program_id(0) % tiles_per_seq) * n_rows + r0) + lax.broadcasted_iota(I32, (part, LANES), 0)
        hi = (pos >> 7).astype(F32)
        lo = (pos & (LANES - 1)).astype(F32)
        q_pos = jnp.where(pair_of == 32, hi, jnp.where(pair_of == 33, lo, 0.0))
        k_pos = jnp.where(pair_of == 34, hi, jnp.where(pair_of == 35, lo, 0.0))

        def store_with_features(ref, x, pos_feat, const_ref):
            for i in range(2 * B_HEADS):
                chunk = x[:, (i // 2) * LANES:(i // 2 + 1) * LANES]
                first = chunk if i % 2 == 0 else pltpu.roll(chunk, HEAD_DIM, 1)
                feat = pos_feat + const_ref[i % B_HEADS:i % B_HEADS + 1, :]
                ref[i, rs, :] = jnp.where(lane < HEAD_DIM, first, feat).astype(BF16)

        store_with_features(qb_ref, proj(C_QB, C_KB) * QSCALE, q_pos, qc_ref)
        store_with_features(kb_ref, proj(C_KB, C_VB), k_pos, kc_ref)
        vb = proj(C_VB, C_G)
        for i in range(B_HEADS):
            vb_ref[i, rs, 0:B_VDIM] = vb[:, i * B_VDIM:(i + 1) * B_VDIM].astype(BF16)
            vb_ref[i, rs, B_VDIM:2 * B_VDIM] = one_at_0.astype(BF16)
        gate_ref[rs, :] = proj(C_G, IN_COLS)


def _alibi_feature_constants():
    c = jnp.asarray([2.0 ** (-8.0 * (h + 1) / B_HEADS) * LOG2E for h in range(B_HEADS)], F32)
    c1 = c.astype(BF16).astype(F32)
    c2 = (c - c1).astype(BF16).astype(F32)
    parts = jnp.stack([LANES * c1, LANES * c2, c1, c2], axis=1)
    zeros = jnp.zeros((B_HEADS, LANES), F32)
    return (zeros.at[:, HEAD_DIM + 4:HEAD_DIM + 8].set(-parts), zeros.at[:, HEAD_DIM:HEAD_DIM + 4].set(parts))


def _in_proj(x2, lng, lnb, w_in, qg, kg, cos_t, sin_t, seg, seq):
    n = x2.shape[0]
    tm = ROW_TILE
    tiles_per_seq = seq // tm

    def rows(w):
        return pl.BlockSpec((tm, w), lambda i: (i, 0))

    def heads(nh, w):
        return pl.BlockSpec((nh, tm, w), lambda i: (0, i, 0))

    tab = pl.BlockSpec((tm, LANES), lambda i: (i % tiles_per_seq, 0))
    head_outs = [(A_HEADS, HEAD_DIM), (A_KV_HEADS, HEAD_DIM), (A_KV_HEADS, LANES),
                 (2 * B_HEADS, LANES), (2 * B_HEADS, LANES), (B_HEADS, 2 * B_VDIM)]
    q_const, k_const = _alibi_feature_constants()
    return pl.pallas_call(
        functools.partial(_in_proj_kernel, tiles_per_seq=tiles_per_seq),
        grid=(n // tm,),
        in_specs=[rows(D_MODEL), _full((1, D_MODEL)), _full((1, D_MODEL)), _full((D_MODEL, IN_COLS)),
                  _full((1, LANES)), _full((1, LANES)), tab, tab, _full((A_WIDTH, A_WIDTH)),
                  _full((B_HEADS, LANES)), _full((B_HEADS, LANES))],
        out_specs=[rows(D_MODEL)] + [heads(nh, w) for nh, w in head_outs] + [rows(2 * D_MODEL)],
        out_shape=([jax.ShapeDtypeStruct((n, D_MODEL), F32)]
                   + [jax.ShapeDtypeStruct((nh, n, w), BF16) for nh, w in head_outs]
                   + [jax.ShapeDtypeStruct((n, 2 * D_MODEL), F32)]),
        compiler_params=_params("parallel"),
        name="in_proj",
    )(x2, lng, lnb, w_in, qg, kg, cos_t, sin_t, seg, q_const, k_const)


PIPE_SLOTS = 2


def _run_units(n_units, scores, softmax, values):
    scores(0, 0)
    scores(1, 1)
    softmax(0, 0)

    def two_units(i, carry):
        t = PIPE_SLOTS * (i + 1)
        scores(t, 0)
        softmax(t - 1, 1)
        values(t - 2, 0)
        scores(t + 1, 1)
        softmax(t, 0)
        values(t - 1, 1)
        return carry

    lax.fori_loop(0, (n_units >> 1) - 1, two_units, 0)
    softmax(n_units - 1, 1)
    values(n_units - 2, 0)
    values(n_units - 1, 1)


def _chunk_max(s):
    cm = s[:, 0:LANES]
    for c in range(1, s.shape[1] // LANES):
        cm = jnp.maximum(cm, s[:, c * LANES:(c + 1) * LANES])
    return cm


def _softmax_update(load_s, m_ref, i, chunk_max=None):
    m_prev = m_ref[i]
    row_max = jnp.max(load_s() if chunk_max is None else chunk_max, axis=1, keepdims=True)
    m_new = jnp.maximum(m_prev, row_max)
    alpha = jnp.exp2(m_prev - m_new)
    s = load_s()
    p = jnp.exp2(s - jnp.tile(m_new, (1, s.shape[1] // LANES)))
    m_ref[i] = m_new
    return p.astype(BF16), alpha


def _init_softmax_state(m_ref, acc_ref):
    m_ref[...] = jnp.full(m_ref.shape, -jnp.inf, F32)
    acc_ref[...] = jnp.zeros(acc_ref.shape, F32)


def _gqa_kernel(q_ref, k_ref, v_ref, o_ref, s_buf, p_buf, a_buf, x_buf, m_ref, acc_ref, *, tq, tk, n_sub, n_steps):
    step = pl.program_id(2)

    @pl.when(step == 0)
    def _():
        _init_softmax_state(m_ref, acc_ref)

    def keys(u):
        return u & (A_KV_HEADS - 1), pl.ds(pl.multiple_of((u >> 1) * tk, tk), tk)

    def scores(u, slot):
        g, rows = keys(u)
        q = q_ref[pl.ds(g * A_GROUP, A_GROUP)].reshape(A_GROUP * tq, HEAD_DIM)
        s = lax.dot_general(q, k_ref[g, rows, :], NT_DIMS, preferred_element_type=F32)
        s_buf[slot] = s
        x_buf[slot] = _chunk_max(s)

    def softmax(u, slot):
        p, alpha = _softmax_update(lambda: s_buf[slot], m_ref, u & (A_KV_HEADS - 1), chunk_max=x_buf[slot])
        p_buf[slot] = p
        a_buf[slot] = alpha

    def values(u, slot):
        g, rows = keys(u)
        pv = jnp.dot(p_buf[slot], v_ref[g, rows, :], preferred_element_type=F32)
        acc_ref[g] = a_buf[slot] * acc_ref[g] + pv

    _run_units(n_sub * A_KV_HEADS, scores, softmax, values)

    @pl.when(step == n_steps - 1)
    def _():
        for h in range(A_HEADS):
            g, r = divmod(h, A_GROUP)
            acc = acc_ref[g, r * tq:(r + 1) * tq, :]
            y = acc[:, 0:HEAD_DIM] / acc[:, HEAD_DIM:HEAD_DIM + 1]
            o_ref[:, h * HEAD_DIM:(h + 1) * HEAD_DIM] = y.astype(BF16)


def _gqa_attention(qa, ka, va):
    _, b, seq, _ = qa.shape
    tq = min(GQA_TQ, seq)
    tk = min(GQA_TK, seq)
    span = seq
    n_steps = seq // span
    rows = A_GROUP * tq
    resident = pl.Buffered(1)
    return pl.pallas_call(
        functools.partial(_gqa_kernel, tq=tq, tk=tk, n_sub=span // tk, n_steps=n_steps),
        grid=(b, seq // tq, n_steps),
        in_specs=[pl.BlockSpec((A_HEADS, None, tq, HEAD_DIM), lambda bi, qi, si: (0, bi, qi, 0)),
                  pl.BlockSpec((A_KV_HEADS, None, span, HEAD_DIM), lambda bi, qi, si: (0, bi, si, 0),
                               pipeline_mode=resident),
                  pl.BlockSpec((A_KV_HEADS, None, span, LANES), lambda bi, qi, si: (0, bi, si, 0),
                               pipeline_mode=resident)],
        out_specs=pl.BlockSpec((None, tq, A_WIDTH), lambda bi, qi, si: (bi, qi, 0)),
        out_shape=jax.ShapeDtypeStruct((b, seq, A_WIDTH), BF16),
        scratch_shapes=[pltpu.VMEM((PIPE_SLOTS, rows, tk), F32), pltpu.VMEM((PIPE_SLOTS, rows, tk), BF16),
                        pltpu.VMEM((PIPE_SLOTS, rows, LANES), F32), pltpu.VMEM((PIPE_SLOTS, rows, LANES), F32),
                        pltpu.VMEM((A_KV_HEADS, rows, LANES), F32), pltpu.VMEM((A_KV_HEADS, rows, LANES), F32)],
        compiler_params=_params("parallel", "parallel", "arbitrary"),
        name="gqa_attn",
    )(qa, ka, va)


def _diff_kernel(slope_ref, q_ref, k_ref, v_ref, lq1_ref, lk1_ref, lq2_ref, lk2_ref, g_ref, o_ref,
                 s_buf, p_buf, a_buf, x_buf, m_ref, acc_ref, *, tq, tk, span, n_steps, lam_init):
    qi = pl.program_id(1)
    step = pl.program_id(2)
    n_sub = span // tk
    q_pos = qi * tq
    diag_step = q_pos // span
    diag_sub = (q_pos // tk) % n_sub
    lane = lax.broadcasted_iota(I32, (1, LANES), 1)

    @pl.when(step == 0)
    def _():
        _init_softmax_state(m_ref, acc_ref)

    @pl.when(step == diag_step)
    def _():
        rows = pl.ds(pl.multiple_of(diag_sub * tk, tk), tk)
        offset = q_pos - (q_pos // tk) * tk
        dist = jnp.abs(lax.broadcasted_iota(I32, (tq, tk), 0) - lax.broadcasted_iota(I32, (tq, tk), 1) + offset)
        dist = dist.astype(F32)
        for h in range(B_HEADS):
            bias = dist * slope_ref[h]
            v = v_ref[h, rows, :]
            for m in range(2):
                i = m * B_HEADS + h
                s = lax.dot_general(q_ref[i][:, 0:HEAD_DIM], k_ref[i, rows, 0:HEAD_DIM], NT_DIMS,
                                    preferred_element_type=F32) - bias
                p, alpha = _softmax_update(lambda: s, m_ref, i)
                acc_ref[i] = jnp.tile(alpha, (1, 2)) * acc_ref[i] + jnp.dot(p, v, preferred_element_type=F32)

    def keys(u):
        j = u >> 2
        j = j + jnp.where((step == diag_step) & (j >= diag_sub), 1, 0)
        return u & (B_HEADS - 1), j, pl.ds(pl.multiple_of(j * tk, tk), tk)

    def scores(u, slot):
        h, j, rows = keys(u)
        keys_before = q_pos >= step * span + (j + 1) * tk
        sign = jnp.where(keys_before, -1.0, 1.0)
        lane_scale = jnp.where(lane < HEAD_DIM, 1.0, sign).astype(BF16)
        for m in range(2):
            i = m * B_HEADS + h
            s = lax.dot_general(q_ref[i] * lane_scale, k_ref[i, rows, :], NT_DIMS, preferred_element_type=F32)
            s_buf[slot, m] = s
            x_buf[slot, m] = _chunk_max(s)

    def softmax(u, slot):
        h = u & (B_HEADS - 1)
        for m in range(2):
            p, alpha = _softmax_update(lambda: s_buf[slot, m], m_ref, m * B_HEADS + h, chunk_max=x_buf[slot, m])
            p_buf[slot, m] = p
            a_buf[slot, m] = alpha

    def values(u, slot):
        h, _, rows = keys(u)
        v = v_ref[h, rows, :]
        for m in range(2):
            i = m * B_HEADS + h
            pv = jnp.dot(p_buf[slot, m], v, preferred_element_type=F32)
            acc_ref[i] = jnp.tile(a_buf[slot, m], (1, 2)) * acc_ref[i] + pv

    _run_units(jnp.where(step == diag_step, n_sub - 1, n_sub) * B_HEADS, scores, softmax, values)

    @pl.when(step == n_steps - 1)
    def _():
        lam = (jnp.exp(jnp.sum(lq1_ref[...] * lk1_ref[...], axis=1, keepdims=True))
               - jnp.exp(jnp.sum(lq2_ref[...] * lk2_ref[...], axis=1, keepdims=True)) + lam_init)

        def normalized(i):
            acc = acc_ref[i]
            return acc[:, 0:B_VDIM] / acc[:, B_VDIM:B_VDIM + 1]

        for h in range(B_HEADS):
            y = normalized(h) - lam * normalized(B_HEADS + h)
            ms = jnp.mean(y * y, axis=1, keepdims=True)
            y = y * lax.rsqrt(ms + RMS_EPS) * g_ref[...] * (1.0 - lam_init)
            o_ref[:, h * B_VDIM:(h + 1) * B_VDIM] = y.astype(BF16)


def _diff_attention(qb, kb, vb, lq1, lk1, lq2, lk2, subln_g, lam_init):
    _, b, seq, _ = qb.shape
    tq = min(DIFF_TQ, seq)
    tk = min(DIFF_TK, seq)
    span = min(DIFF_SPAN, seq)
    n_steps = seq // span
    assert tk % tq == 0 and span // tk >= 2
    nh = 2 * B_HEADS
    slopes = jnp.asarray([2.0 ** (-8.0 * (h + 1) / B_HEADS) * LOG2E for h in range(B_HEADS)], F32)
    vec = pl.BlockSpec((1, HEAD_DIM), lambda bi, qi, si, *_: (0, 0))
    return pl.pallas_call(
        functools.partial(_diff_kernel, tq=tq, tk=tk, span=span, n_steps=n_steps, lam_init=lam_init),
        grid_spec=pltpu.PrefetchScalarGridSpec(
            num_scalar_prefetch=1,
            grid=(b, seq // tq, n_steps),
            in_specs=[pl.BlockSpec((nh, None, tq, LANES), lambda bi, qi, si, *_: (0, bi, qi, 0)),
                      pl.BlockSpec((nh, None, span, LANES), lambda bi, qi, si, *_: (0, bi, si, 0)),
                      pl.BlockSpec((B_HEADS, None, span, 2 * B_VDIM), lambda bi, qi, si, *_: (0, bi, si, 0)),
                      vec, vec, vec, vec, pl.BlockSpec((1, B_VDIM), lambda bi, qi, si, *_: (0, 0))],
            out_specs=pl.BlockSpec((None, tq, B_WIDTH), lambda bi, qi, si, *_: (bi, qi, 0)),
            scratch_shapes=[pltpu.VMEM((PIPE_SLOTS, 2, tq, tk), F32), pltpu.VMEM((PIPE_SLOTS, 2, tq, tk), BF16),
                            pltpu.VMEM((PIPE_SLOTS, 2, tq, LANES), F32), pltpu.VMEM((PIPE_SLOTS, 2, tq, LANES), F32),
                            pltpu.VMEM((nh, tq, LANES), F32), pltpu.VMEM((nh, tq, 2 * B_VDIM), F32)],
        ),
        out_shape=jax.ShapeDtypeStruct((b, seq, B_WIDTH), BF16),
        compiler_params=_params("parallel", "parallel", "arbitrary"),
        name="diff_attn",
    )(slopes, qb, kb, vb, lq1, lk1, lq2, lk2, subln_g)


def _merge_kernel(ya_ref, yb_ref, gate_ref, h_ref, p_ref, wa_ref, wb_ref, wo_ref, g_ref, b_ref,
                  wrh_ref, wrl_ref, wpg_ref, wpp_ref, x1_ref, base_ref, aff_ref):
    ma = jnp.dot(ya_ref[...], wa_ref[...], preferred_element_type=F32)
    mb = jnp.dot(yb_ref[...], wb_ref[...], preferred_element_type=F32)
    merged = _sigmoid(gate_ref[:, 0:D_MODEL]) * ma + _sigmoid(gate_ref[:, D_MODEL:2 * D_MODEL]) * mb
    out = jnp.dot(merged.astype(BF16), wo_ref[...], preferred_element_type=F32)
    x1 = _layer_norm(ALPHA * h_ref[...] + out, g_ref[...], b_ref[...])
    x1_ref[...] = x1
    xh, xl = _split_bf16(x1)
    ple_gate = _sigmoid(jnp.dot(xh, wpg_ref[...], preferred_element_type=F32))
    base_ref[...] = ALPHA * x1 + ple_gate * jnp.dot(p_ref[...].astype(BF16), wpp_ref[...], preferred_element_type=F32)
    wh = wrh_ref[...]
    logits = (lax.dot_general(wh, xh, NT_DIMS, preferred_element_type=F32)
              + lax.dot_general(wh, xl, NT_DIMS, preferred_element_type=F32)
              + lax.dot_general(wrl_ref[...], xh, NT_DIMS, preferred_element_type=F32))
    e = jnp.exp(logits - jnp.max(logits, axis=0, keepdims=True))
    aff_ref[...] = e / jnp.sum(e, axis=0, keepdims=True)


def _merge(ya, yb, gates, h, p2, wa, wb, wo, g, b, wrh, wrl, wpg, wpp):
    n = h.shape[0]
    tm = ROW_TILE

    def rows(w):
        return pl.BlockSpec((tm, w), lambda i: (i, 0))

    return pl.pallas_call(
        _merge_kernel,
        grid=(n // tm,),
        in_specs=[rows(A_WIDTH), rows(B_WIDTH), rows(2 * D_MODEL), rows(D_MODEL), rows(PLE_DIM),
                  _full((A_WIDTH, D_MODEL)), _full((B_WIDTH, D_MODEL)), _full((D_MODEL, D_MODEL)),
                  _full((1, D_MODEL)), _full((1, D_MODEL)),
                  _full((N_EXPERTS, D_MODEL)), _full((N_EXPERTS, D_MODEL)),
                  _full((D_MODEL, D_MODEL)), _full((PLE_DIM, D_MODEL))],
        out_specs=[rows(D_MODEL), rows(D_MODEL), pl.BlockSpec((N_EXPERTS, tm), lambda i: (0, i))],
        out_shape=[jax.ShapeDtypeStruct((n, D_MODEL), F32), jax.ShapeDtypeStruct((n, D_MODEL), F32),
                   jax.ShapeDtypeStruct((N_EXPERTS, n), F32)],
        compiler_params=_params("parallel"),
        name="merge",
    )(ya, yb, gates, h, p2, wa, wb, wo, g, b, wrh, wrl, wpg, wpp)


def _route_kernel(aff_ref, rank_ref, idx_ref, blkp_ref, sel_scr, wi_scr, p_scr, *, nb, cap):
    ne = N_EXPERTS
    shape3 = (ne, nb, LANES)
    bits = pltpu.bitcast(aff_ref[...], I32).reshape(shape3)
    capf = float(cap)

    def count(mask3):
        part = jnp.sum(jnp.where(mask3, 1.0, 0.0), axis=1, keepdims=True)
        return jnp.sum(part, axis=2, keepdims=True)

    def search(i, thr):
        cand = thr | jnp.left_shift(jnp.int32(1), 30 - i)
        return jnp.where(count(bits >= cand) >= capf, cand, thr)

    thr = lax.fori_loop(0, 31, search, jnp.zeros((ne, 1, 1), I32))
    gt = bits > thr
    eq = bits == thr
    need = capf - count(gt)

    li = lax.broadcasted_iota(I32, (LANES, LANES), 0)
    lj = lax.broadcasted_iota(I32, (LANES, LANES), 1)
    upper_incl = jnp.where(li <= lj, 1.0, 0.0).astype(BF16)
    lower_incl = jnp.where(lj <= li, 1.0, 0.0).astype(BF16)
    bi = lax.broadcasted_iota(I32, (nb, nb), 0)
    bj = lax.broadcasted_iota(I32, (nb, nb), 1)
    lower_strict = jnp.where(bj < bi, 1.0, 0.0).astype(BF16)
    upper_strict = jnp.where(bi < bj, 1.0, 0.0).astype(BF16)

    def prefix(mask2):
        wi = jnp.dot(mask2.astype(BF16), upper_incl, preferred_element_type=F32)
        tot = jnp.broadcast_to(wi[:, LANES - 1:LANES], wi.shape).astype(BF16)
        blk = [jnp.dot(lower_strict, tot[e * nb:(e + 1) * nb], preferred_element_type=F32)
               for e in range(ne)]
        return wi, jnp.concatenate(blk, axis=0)

    eqf = jnp.where(eq, 1.0, 0.0).reshape(ne * nb, LANES)
    wi, bp = prefix(eqf)
    eq_rank = (wi + bp - eqf).reshape(shape3)
    sel = gt | (eq & (eq_rank < need))
    self_ = jnp.where(sel, 1.0, 0.0).reshape(ne * nb, LANES)
    wi, bp = prefix(self_)
    rank = wi + bp - self_
    rank_ref[...] = jnp.where(self_ > 0.0, rank, -1.0).astype(I32)
    sel_scr[...] = self_
    wi_scr[...] = wi
    p_scr[...] = bp

    chunk = min(cap, 1024)
    ones8 = jnp.ones((8, LANES), BF16)
    b_col = lax.broadcasted_iota(I32, (nb, 1), 0).astype(F32)

    def per_expert(e, carry):
        r0 = pl.multiple_of(e * nb, 8)
        sel_e = sel_scr[pl.ds(r0, nb), :].astype(BF16)
        wi_e = wi_scr[pl.ds(r0, nb), :]
        p_e = p_scr[pl.ds(r0, nb), :][:, 0:1]
        p_end = p_e + wi_e[:, LANES - 1:LANES]
        wi_t = lax.dot_general(lower_incl, sel_e, NT_DIMS, preferred_element_type=F32).astype(BF16)
        tot_row = lax.dot_general(ones8, sel_e, NT_DIMS, preferred_element_type=F32)
        p_row = jnp.dot(tot_row.astype(BF16), upper_strict, preferred_element_type=F32)
        blkp_ref[e] = p_row[0:1, :].astype(I32)
        for c in range(cap // chunk):
            slot = (lax.broadcasted_iota(I32, (1, chunk), 1) + c * chunk).astype(F32)
            hot = (p_e <= slot) & (slot < p_end)
            w_slot = jnp.dot(wi_t, jnp.where(hot, 1.0, 0.0).astype(BF16), preferred_element_type=F32)
            p_slot = jnp.sum(jnp.where(hot, p_e, 0.0), axis=0, keepdims=True)
            b_slot = jnp.sum(jnp.where(hot, b_col, 0.0), axis=0, keepdims=True)
            lane = jnp.sum(jnp.where(w_slot <= slot - p_slot, 1.0, 0.0), axis=0, keepdims=True)
            idx_ref[e, :, c * chunk:(c + 1) * chunk] = (b_slot * LANES + lane).astype(I32)
        return carry

    lax.fori_loop(0, ne, per_expert, 0)


def _route(aff2, n, cap):
    nb = n // LANES
    return pl.pallas_call(
        functools.partial(_route_kernel, nb=nb, cap=cap),
        out_shape=[jax.ShapeDtypeStruct((N_EXPERTS * nb, LANES), I32),
                   jax.ShapeDtypeStruct((N_EXPERTS, 1, cap), I32),
                   jax.ShapeDtypeStruct((N_EXPERTS, 1, nb), I32)],
        scratch_shapes=[pltpu.VMEM((N_EXPERTS * nb, LANES), F32)] * 3,
        compiler_params=pltpu.CompilerParams(vmem_limit_bytes=VMEM_LIMIT),
        name="route",
    )(aff2)


FFN_SLOTS = 3


def _ffn_kernel(idx_ref, x_hbm, wg_ref, wu_ref, wd_ref, y_ref, xbuf, sem, *, tm, tiles, n_steps):
    step = pl.program_id(0) * tiles + pl.program_id(1)
    slot = lax.rem(step, FFN_SLOTS)

    def fetch(src_step, dst_slot):
        for r in range(tm):
            pltpu.make_async_copy(x_hbm.at[pl.ds(idx_ref[src_step * tm + r], 1), :],
                                  xbuf.at[dst_slot, pl.ds(r, 1), :], sem.at[dst_slot]).start()

    def wait(which):
        pltpu.make_async_copy(x_hbm.at[pl.ds(0, tm), :], xbuf.at[which], sem.at[which]).wait()

    @pl.when(step == 0)
    def _():
        fetch(0, 0)
        fetch(jnp.minimum(1, n_steps - 1), 1)

    wait(slot)
    x = xbuf[slot].astype(BF16)
    fc = 512
    acc = None
    for c in range(EXPERT_FF // fc):
        sl = slice(c * fc, (c + 1) * fc)
        a = jnp.dot(x, wg_ref[:, sl], preferred_element_type=F32)
        u = jnp.dot(x, wu_ref[:, sl], preferred_element_type=F32)
        mid = (a * _sigmoid(a) * u).astype(BF16)
        part = jnp.dot(mid, wd_ref[sl, :], preferred_element_type=F32)
        acc = part if acc is None else acc + part
    y_ref[...] = acc.astype(BF16)
    next_src = jnp.minimum(step + 2, n_steps - 1)
    next_slot = lax.rem(step + 2, FFN_SLOTS)
    for k in range(FFN_SLOTS):
        @pl.when(next_slot == k)
        def _():
            fetch(next_src, k)

    @pl.when(step == n_steps - 1)
    def _():
        wait(lax.rem(step + 1, FFN_SLOTS))
        wait(lax.rem(step + 2, FFN_SLOTS))


def _ffn(idx_flat, x1, wg, wu, wd, cap):
    tm = min(ROW_TILE, cap)
    tiles = cap // tm
    return pl.pallas_call(
        functools.partial(_ffn_kernel, tm=tm, tiles=tiles, n_steps=N_EXPERTS * tiles),
        grid_spec=pltpu.PrefetchScalarGridSpec(
            num_scalar_prefetch=1,
            grid=(N_EXPERTS, tiles),
            in_specs=[pl.BlockSpec(memory_space=pl.ANY),
                      pl.BlockSpec((None, D_MODEL, EXPERT_FF), lambda e, i, idx: (e, 0, 0)),
                      pl.BlockSpec((None, D_MODEL, EXPERT_FF), lambda e, i, idx: (e, 0, 0)),
                      pl.BlockSpec((None, EXPERT_FF, D_MODEL), lambda e, i, idx: (e, 0, 0))],
            out_specs=pl.BlockSpec((tm, D_MODEL), lambda e, i, idx: (e * tiles + i, 0)),
            scratch_shapes=[pltpu.VMEM((FFN_SLOTS, tm, D_MODEL), F32), pltpu.SemaphoreType.DMA((FFN_SLOTS,))],
        ),
        out_shape=jax.ShapeDtypeStruct((N_EXPERTS * cap, D_MODEL), BF16),
        compiler_params=_params("arbitrary", "arbitrary"),
        name="ffn",
    )(idx_flat, x1, wg, wu, wd)


TOK_TILE = 128
ROW_ALIGN = 16
WIN = TOK_TILE + ROW_ALIGN
SPARSE_CNT = 32
SPARSE_WIN = SPARSE_CNT + ROW_ALIGN
GROUP = 4
assert GROUP * SPARSE_WIN <= MXU_DEPTH and N_EXPERTS % GROUP == 0


def _final_kernel(start_ref, sparse_ref, base_ref, rank_ref, aff_ref, g_ref, b_ref, ye_hbm, o_ref, buf, sem,
                  *, cap, n_steps):
    t = pl.program_id(0)
    slot = t & 1

    def window(step, e, rows):
        first = start_ref[step * N_EXPERTS + e]
        ws = jnp.minimum((first // ROW_ALIGN) * ROW_ALIGN, cap - rows)
        return pl.multiple_of(ws, ROW_ALIGN)

    def fetch(step, dst_slot):
        def start_copies(rows):
            for e in range(N_EXPERTS):
                pltpu.make_async_copy(ye_hbm.at[pl.ds(e * cap + window(step, e, rows), rows), :],
                                      buf.at[dst_slot, pl.ds(e * rows, rows), :], sem.at[dst_slot]).start()

        @pl.when(sparse_ref[step] == 1)
        def _():
            start_copies(SPARSE_WIN)

        @pl.when(sparse_ref[step] == 0)
        def _():
            start_copies(WIN)

    def wait(rows):
        pltpu.make_async_copy(ye_hbm.at[pl.ds(0, N_EXPERTS * rows), :],
                              buf.at[slot, pl.ds(0, N_EXPERTS * rows), :], sem.at[slot]).wait()

    @pl.when(t == 0)
    def _():
        fetch(0, 0)

    @pl.when(t + 1 < n_steps)
    def _():
        fetch(t + 1, 1 - slot)

    @pl.when(sparse_ref[t] == 1)
    def _():
        wait(SPARSE_WIN)
        gate = aff_ref[...]
        gate_hi = gate.astype(BF16).astype(F32)
        gate_lo = gate - gate_hi
        lane = lax.broadcasted_iota(I32, (1, GROUP * SPARSE_WIN), 1)
        total = base_ref[...]
        for grp in range(N_EXPERTS // GROUP):
            hi = jnp.zeros((TOK_TILE, GROUP * SPARSE_WIN), F32)
            lo = hi
            for k in range(GROUP):
                e = grp * GROUP + k
                rank = rank_ref[:, e:e + 1]
                target = jnp.where(rank >= 0, rank - window(t, e, SPARSE_WIN) + k * SPARSE_WIN, -1)
                hot = lane == target
                hi = jnp.where(hot, gate_hi[:, e:e + 1], hi)
                lo = jnp.where(hot, gate_lo[:, e:e + 1], lo)
            rows = buf[slot, grp * GROUP * SPARSE_WIN:(grp + 1) * GROUP * SPARSE_WIN, :]
            y = jnp.dot(jnp.concatenate([hi, lo], axis=0).astype(BF16), rows, preferred_element_type=F32)
            total = total + y[0:TOK_TILE] + y[TOK_TILE:2 * TOK_TILE]
        o_ref[...] = _layer_norm(total, g_ref[...], b_ref[...])

    @pl.when(sparse_ref[t] == 0)
    def _():
        wait(WIN)
        lane = lax.broadcasted_iota(I32, (1, WIN), 1)
        total = base_ref[...]
        for e in range(N_EXPERTS):
            hot = rank_ref[:, e:e + 1] == lane + window(t, e, WIN)
            y = jnp.dot(jnp.where(hot, 1.0, 0.0).astype(BF16), buf[slot, e * WIN:(e + 1) * WIN, :],
                        preferred_element_type=F32)
            total = total + aff_ref[:, e:e + 1] * y
        o_ref[...] = _layer_norm(total, g_ref[...], b_ref[...])


def _final(starts, sparse, base, rank_t, aff_t, g, b, ye, cap):
    n = base.shape[0]
    tm = TOK_TILE

    def rows(w):
        return pl.BlockSpec((tm, w), lambda i, *_: (i, 0))

    def full(shape):
        return pl.BlockSpec(shape, lambda i, *_: (0,) * len(shape))

    return pl.pallas_call(
        functools.partial(_final_kernel, cap=cap, n_steps=n // tm),
        grid_spec=pltpu.PrefetchScalarGridSpec(
            num_scalar_prefetch=2,
            grid=(n // tm,),
            in_specs=[rows(D_MODEL), rows(N_EXPERTS), rows(N_EXPERTS), full((1, D_MODEL)), full((1, D_MODEL)),
                      pl.BlockSpec(memory_space=pl.ANY)],
            out_specs=rows(D_MODEL),
            scratch_shapes=[pltpu.VMEM((2, N_EXPERTS * WIN, D_MODEL), BF16), pltpu.SemaphoreType.DMA((2,))],
        ),
        out_shape=jax.ShapeDtypeStruct((n, D_MODEL), F32),
        compiler_params=_params("arbitrary"),
        name="final",
    )(starts, sparse, base, rank_t, aff_t, g, b, ye)


def _rope_tables(seq):
    n_rows = seq // GRID_W
    rows = jnp.repeat(jnp.arange(n_rows, dtype=F32), GRID_W)
    cols = jnp.tile(jnp.arange(GRID_W, dtype=F32), n_rows)
    sec = HEAD_DIM // 2
    inv_freq = ROPE_THETA ** (-jnp.arange(0, sec, 2, dtype=F32) / sec)
    ang_r = rows[:, None] * inv_freq[None, :]
    ang_c = cols[:, None] * inv_freq[None, :]
    cos = jnp.concatenate([jnp.cos(ang_r)] * 2 + [jnp.cos(ang_c)] * 2, axis=-1)
    sin = jnp.concatenate([-jnp.sin(ang_r), jnp.sin(ang_r), -jnp.sin(ang_c), jnp.sin(ang_c)], axis=-1)
    return jnp.tile(cos, (1, 2)), jnp.tile(sin, (1, 2))


def _trunk(x, p, w):
    b, seq, _ = x.shape
    n = b * seq
    cap = max(1, CAPACITY_FACTOR * n // N_EXPERTS)
    assert seq % ROW_TILE == 0 and n % (8 * LANES) == 0 and cap % TOK_TILE == 0 and cap >= WIN
    cos_t, sin_t = _rope_tables(seq)

    h, qa, ka, va, qb, kb, vb, gates = _in_proj(
        x.reshape(n, D_MODEL), w["ln_emb_g"], w["ln_emb_b"], w["w_in"], w["q_norm_g"], w["k_norm_g"],
        cos_t, sin_t, w["seg"], seq)

    def per_batch(t):
        return t.reshape(t.shape[0], b, seq, t.shape[2])

    ya = _gqa_attention(per_batch(qa), per_batch(ka), per_batch(va))
    lam_init = 0.8 - 0.6 * math.exp(-0.3 * 0)
    yb = _diff_attention(per_batch(qb), per_batch(kb), per_batch(vb),
                         w["lambda_q1"], w["lambda_k1"], w["lambda_q2"], w["lambda_k2"], w["subln_g"], lam_init)

    x1, base, aff = _merge(ya.reshape(n, -1), yb.reshape(n, -1), gates, h, p.reshape(n, PLE_DIM),
                           w["w_branch_a"], w["w_branch_b"], w["w_out"], w["ln1_g"], w["ln1_b"],
                           w["wr_hi"], w["wr_lo"], w["w_ple_gate"], w["w_ple_proj"])

    nb = n // LANES
    rank2, idx, blkp = _route(aff.reshape(N_EXPERTS * nb, LANES), n, cap)
    ye = _ffn(idx.reshape(N_EXPERTS * cap), x1, w["w_gate"], w["w_up"], w["w_down"], cap)

    starts = blkp.reshape(N_EXPERTS, nb)[:, ::TOK_TILE // LANES]
    ends = jnp.concatenate([starts[:, 1:], jnp.full((N_EXPERTS, 1), cap, I32)], axis=1)
    sparse = (jnp.max(ends - starts, axis=0) <= SPARSE_CNT).astype(I32)
    out = _final(starts.T.reshape(-1), sparse, base, rank2.reshape(N_EXPERTS, n).T, aff.T,
                 w["ln2_g"], w["ln2_b"], ye, cap)
    return out.reshape(b, seq, D_MODEL)


def _prep_weights(ln_emb_g, ln_emb_b, w_in, q_norm_g, k_norm_g, lambda_q1, lambda_k1, lambda_q2, lambda_k2,
                  subln_g, w_branch_a, w_branch_b, w_out, ln1_g, ln1_b, w_router, w_gate, w_up, w_down,
                  w_ple_gate, w_ple_proj, ln2_g, ln2_b):
    i = 0
    seg_i = lax.broadcasted_iota(I32, (A_WIDTH, A_WIDTH), 0) // HEAD_DIM
    seg_j = lax.broadcasted_iota(I32, (A_WIDTH, A_WIDTH), 1) // HEAD_DIM
    wr_t = w_router[i].T
    wr_hi = wr_t.astype(BF16)
    return {
        "ln_emb_g": ln_emb_g.reshape(1, D_MODEL), "ln_emb_b": ln_emb_b.reshape(1, D_MODEL),
        "w_in": w_in[i].astype(BF16),
        "q_norm_g": jnp.tile(q_norm_g[i], 2).reshape(1, LANES), "k_norm_g": jnp.tile(k_norm_g[i], 2).reshape(1, LANES),
        "seg": (seg_i == seg_j).astype(BF16),
        "lambda_q1": lambda_q1[i].reshape(1, HEAD_DIM), "lambda_k1": lambda_k1[i].reshape(1, HEAD_DIM),
        "lambda_q2": lambda_q2[i].reshape(1, HEAD_DIM), "lambda_k2": lambda_k2[i].reshape(1, HEAD_DIM),
        "subln_g": subln_g[i].reshape(1, B_VDIM),
        "w_branch_a": w_branch_a[i].astype(BF16), "w_branch_b": w_branch_b[i].astype(BF16),
        "w_out": w_out[i].astype(BF16),
        "ln1_g": ln1_g[i].reshape(1, D_MODEL), "ln1_b": ln1_b[i].reshape(1, D_MODEL),
        "wr_hi": wr_hi, "wr_lo": (wr_t - wr_hi.astype(F32)).astype(BF16),
        "w_gate": w_gate[i].astype(BF16), "w_up": w_up[i].astype(BF16), "w_down": w_down[i].astype(BF16),
        "w_ple_gate": w_ple_gate[i].astype(BF16), "w_ple_proj": w_ple_proj[i].astype(BF16),
        "ln2_g": ln2_g[i].reshape(1, D_MODEL), "ln2_b": ln2_b[i].reshape(1, D_MODEL),
    }


def kernel(x_prompt, x_sample, p_prompt, p_sample, ln_emb_g, ln_emb_b, w_in, q_norm_g, k_norm_g, lambda_q1, lambda_k1, lambda_q2, lambda_k2, subln_g, w_branch_a, w_branch_b, w_out, ln1_g, ln1_b, w_router, w_gate, w_up, w_down, w_ple_gate, w_ple_proj, ln2_g, ln2_b):
    w = _prep_weights(ln_emb_g, ln_emb_b, w_in, q_norm_g, k_norm_g, lambda_q1, lambda_k1, lambda_q2, lambda_k2,
                      subln_g, w_branch_a, w_branch_b, w_out, ln1_g, ln1_b, w_router, w_gate, w_up, w_down,
                      w_ple_gate, w_ple_proj, ln2_g, ln2_b)
    return (_trunk(x_prompt, p_prompt[0], w), _trunk(x_sample, p_sample[0], w))
```

```python
import functools
import math

import jax
import jax.numpy as jnp
from jax import lax
from jax.experimental import pallas as pl
from jax.experimental.pallas import tpu as pltpu

F32 = jnp.float32
BF16 = jnp.bfloat16
I32 = jnp.int32

D_MODEL = 1024
DEPTH = 1
HEAD_DIM = 64
A_HEADS = 8
A_KV_HEADS = 2
A_GROUP = A_HEADS // A_KV_HEADS
A_WIDTH = A_HEADS * HEAD_DIM
B_HEADS = 4
B_VDIM = 2 * HEAD_DIM
B_WIDTH = B_HEADS * B_VDIM
GRID_W = 64
ROPE_THETA = 10000.0
N_EXPERTS = 16
EXPERT_FF = 2048
CAPACITY_FACTOR = 2
PLE_DIM = 256
LN_EPS = 1e-5
RMS_EPS = 1e-6
ALPHA = (2 * DEPTH) ** 0.25
LOG2E = math.log2(math.e)
QSCALE = HEAD_DIM ** -0.5 * LOG2E
LANES = 128
ROPE_HALF = HEAD_DIM // 4

C_QA = 0
C_KA = C_QA + A_WIDTH
C_VA = C_KA + A_KV_HEADS * HEAD_DIM
C_QB = C_VA + A_KV_HEADS * HEAD_DIM
C_KB = C_QB + 2 * B_HEADS * HEAD_DIM
C_VB = C_KB + 2 * B_HEADS * HEAD_DIM
C_G = C_VB + B_WIDTH
IN_COLS = C_G + 2 * D_MODEL

VMEM_LIMIT = 56 * 1024 * 1024
MXU_DEPTH = 256

ROW_TILE = 512
IN_PROJ_PARTS = 2
GQA_TQ, GQA_TK = 256, 1024
DIFF_TQ, DIFF_TK, DIFF_SPAN = 512, 1024, 2048
NT_DIMS = (((1,), (1,)), ((), ()))


def _params(*sem):
    return pltpu.CompilerParams(dimension_semantics=sem, vmem_limit_bytes=VMEM_LIMIT)


def _full(shape):
    n = len(shape)
    return pl.BlockSpec(shape, lambda *_: (0,) * n, pipeline_mode=pl.Buffered(1))


def _layer_norm(x, g, b):
    mu = jnp.mean(x, axis=-1, keepdims=True)
    xc = x - mu
    var = jnp.mean(xc * xc, axis=-1, keepdims=True)
    return xc * lax.rsqrt(var + LN_EPS) * g + b


def _sigmoid(x):
    return 1.0 / (1.0 + jnp.exp(-x))


def _split_bf16(x):
    hi = x.astype(BF16)
    lo = (x - hi.astype(F32)).astype(BF16)
    return hi, lo


def _head_mean_sq(x, seg):
    hi, lo = _split_bf16(x * x)
    s = jnp.dot(hi, seg, preferred_element_type=F32) + jnp.dot(lo, seg, preferred_element_type=F32)
    return s * (1.0 / HEAD_DIM)


def _rope(x, cos, sin, first_half):
    fwd = pltpu.roll(x, LANES - ROPE_HALF, 1)
    bwd = pltpu.roll(x, ROPE_HALF, 1)
    return x * cos + jnp.where(first_half, fwd, bwd) * sin


def _in_proj_kernel(x_ref, lng_ref, lnb_ref, w_ref, qg_ref, kg_ref, cos_ref, sin_ref, seg_ref, qc_ref, kc_ref,
                    h_ref, qa_ref, ka_ref, va_ref, qb_ref, kb_ref, vb_ref, gate_ref, *, tiles_per_seq):
    n_rows = x_ref.shape[0]
    part = n_rows // IN_PROJ_PARTS
    lane = lax.broadcasted_iota(I32, (part, LANES), 1)
    first_half = (lane & ROPE_HALF) == 0
    one_at_0 = jnp.where(lane == 0, 1.0, 0.0)
    pair_of = lane >> 1

    for r0 in range(0, n_rows, part):
        rs = slice(r0, r0 + part)
        xn = _layer_norm(x_ref[rs, :], lng_ref[...], lnb_ref[...])
        h_ref[rs, :] = xn
        xb = xn.astype(BF16)

        def proj(lo, hi):
            return jnp.dot(xb, w_ref[:, lo:hi], preferred_element_type=F32)

        def store_heads(ref, x):
            for i in range(x.shape[1] // HEAD_DIM):
                ref[i, rs, :] = x[:, i * HEAD_DIM:(i + 1) * HEAD_DIM].astype(BF16)

        cos = cos_ref[rs, :]
        sin = sin_ref[rs, :]

        qa = proj(C_QA, C_KA)
        qn = qa * lax.rsqrt(_head_mean_sq(qa, seg_ref[...]) + RMS_EPS)
        for c in range(A_WIDTH // LANES):
            pair = _rope(qn[:, c * LANES:(c + 1) * LANES] * qg_ref[...], cos, sin, first_half) * QSCALE
            qa_ref[2 * c, rs, :] = pair[:, 0:HEAD_DIM].astype(BF16)
            qa_ref[2 * c + 1, rs, :] = pair[:, HEAD_DIM:LANES].astype(BF16)

        ka = proj(C_KA, C_VA)
        kn = ka * lax.rsqrt(_head_mean_sq(ka, seg_ref[0:LANES, 0:LANES]) + RMS_EPS)
        store_heads(ka_ref, _rope(kn * kg_ref[...], cos, sin, first_half))
        va = proj(C_VA, C_QB)
        for g in range(A_KV_HEADS):
            v_first = va if g == 0 else pltpu.roll(va, HEAD_DIM, 1)
            va_ref[g, rs, :] = jnp.where(lane < HEAD_DIM, v_first, jnp.where(lane == HEAD_DIM, 1.0, 0.0)).astype(BF16)
        pos = ((pl.program_id(0) % tiles_per_seq) * n_rows + r0) + lax.broadcasted_iota(I32, (part, LANES), 0)
        hi = (pos >> 7).astype(F32)
        lo = (pos & (LANES - 1)).astype(F32)
        q_pos = jnp.where(pair_of == 32, hi, jnp.where(pair_of == 33, lo, 0.0))
        k_pos = jnp.where(pair_of == 34, hi, jnp.where(pair_of == 35, lo, 0.0))

        def store_with_features(ref, x, pos_feat, const_ref):
            for i in range(2 * B_HEADS):
                chunk = x[:, (i // 2) * LANES:(i // 2 + 1) * LANES]
                first = chunk if i % 2 == 0 else pltpu.roll(chunk, HEAD_DIM, 1)
                feat = pos_feat + const_ref[i % B_HEADS:i % B_HEADS + 1, :]
                ref[i, rs, :] = jnp.where(lane < HEAD_DIM, first, feat).astype(BF16)

        store_with_features(qb_ref, proj(C_QB, C_KB) * QSCALE, q_pos, qc_ref)
        store_with_features(kb_ref, proj(C_KB, C_VB), k_pos, kc_ref)
        vb = proj(C_VB, C_G)
        for i in range(B_HEADS):
            vb_ref[i, rs, 0:B_VDIM] = vb[:, i * B_VDIM:(i + 1) * B_VDIM].astype(BF16)
            vb_ref[i, rs, B_VDIM:2 * B_VDIM] = one_at_0.astype(BF16)
        gate_ref[rs, :] = proj(C_G, IN_COLS)


def _alibi_feature_constants():
    c = jnp.asarray([2.0 ** (-8.0 * (h + 1) / B_HEADS) * LOG2E for h in range(B_HEADS)], F32)
    c1 = c.astype(BF16).astype(F32)
    c2 = (c - c1).astype(BF16).astype(F32)
    parts = jnp.stack([LANES * c1, LANES * c2, c1, c2], axis=1)
    zeros = jnp.zeros((B_HEADS, LANES), F32)
    return (zeros.at[:, HEAD_DIM + 4:HEAD_DIM + 8].set(-parts), zeros.at[:, HEAD_DIM:HEAD_DIM + 4].set(parts))


def _in_proj(x2, lng, lnb, w_in, qg, kg, cos_t, sin_t, seg, seq):
    n = x2.shape[0]
    tm = ROW_TILE
    tiles_per_seq = seq // tm

    def rows(w):
        return pl.BlockSpec((tm, w), lambda i: (i, 0))

    def heads(nh, w):
        return pl.BlockSpec((nh, tm, w), lambda i: (0, i, 0))

    tab = pl.BlockSpec((tm, LANES), lambda i: (i % tiles_per_seq, 0))
    head_outs = [(A_HEADS, HEAD_DIM), (A_KV_HEADS, HEAD_DIM), (A_KV_HEADS, LANES),
                 (2 * B_HEADS, LANES), (2 * B_HEADS, LANES), (B_HEADS, 2 * B_VDIM)]
    q_const, k_const = _alibi_feature_constants()
    return pl.pallas_call(
        functools.partial(_in_proj_kernel, tiles_per_seq=tiles_per_seq),
        grid=(n // tm,),
        in_specs=[rows(D_MODEL), _full((1, D_MODEL)), _full((1, D_MODEL)), _full((D_MODEL, IN_COLS)),
                  _full((1, LANES)), _full((1, LANES)), tab, tab, _full((A_WIDTH, A_WIDTH)),
                  _full((B_HEADS, LANES)), _full((B_HEADS, LANES))],
        out_specs=[rows(D_MODEL)] + [heads(nh, w) for nh, w in head_outs] + [rows(2 * D_MODEL)],
        out_shape=([jax.ShapeDtypeStruct((n, D_MODEL), F32)]
                   + [jax.ShapeDtypeStruct((nh, n, w), BF16) for nh, w in head_outs]
                   + [jax.ShapeDtypeStruct((n, 2 * D_MODEL), F32)]),
        compiler_params=_params("parallel"),
        name="in_proj",
    )(x2, lng, lnb, w_in, qg, kg, cos_t, sin_t, seg, q_const, k_const)


PIPE_SLOTS = 2


def _run_units(n_units, scores, softmax, values):
    scores(0, 0)
    scores(1, 1)
    softmax(0, 0)

    def two_units(i, carry):
        t = PIPE_SLOTS * (i + 1)
        scores(t, 0)
        softmax(t - 1, 1)
        values(t - 2, 0)
        scores(t + 1, 1)
        softmax(t, 0)
        values(t - 1, 1)
        return carry

    lax.fori_loop(0, (n_units >> 1) - 1, two_units, 0)
    softmax(n_units - 1, 1)
    values(n_units - 2, 0)
    values(n_units - 1, 1)


def _chunk_max(s):
    cm = s[:, 0:LANES]
    for c in range(1, s.shape[1] // LANES):
        cm = jnp.maximum(cm, s[:, c * LANES:(c + 1) * LANES])
    return cm


def _softmax_update(load_s, m_ref, i, chunk_max=None):
    m_prev = m_ref[i]
    row_max = jnp.max(load_s() if chunk_max is None else chunk_max, axis=1, keepdims=True)
    m_new = jnp.maximum(m_prev, row_max)
    alpha = jnp.exp2(m_prev - m_new)
    s = load_s()
    p = jnp.exp2(s - jnp.tile(m_new, (1, s.shape[1] // LANES)))
    m_ref[i] = m_new
    return p.astype(BF16), alpha


def _init_softmax_state(m_ref, acc_ref):
    m_ref[...] = jnp.full(m_ref.shape, -jnp.inf, F32)
    acc_ref[...] = jnp.zeros(acc_ref.shape, F32)


def _gqa_kernel(q_ref, k_ref, v_ref, o_ref, s_buf, p_buf, a_buf, x_buf, m_ref, acc_ref, *, tq, tk, n_sub, n_steps):
    step = pl.program_id(2)

    @pl.when(step == 0)
    def _():
        _init_softmax_state(m_ref, acc_ref)

    def keys(u):
        return u & (A_KV_HEADS - 1), pl.ds(pl.multiple_of((u >> 1) * tk, tk), tk)

    def scores(u, slot):
        g, rows = keys(u)
        q = q_ref[pl.ds(g * A_GROUP, A_GROUP)].reshape(A_GROUP * tq, HEAD_DIM)
        s = lax.dot_general(q, k_ref[g, rows, :], NT_DIMS, preferred_element_type=F32)
        s_buf[slot] = s
        x_buf[slot] = _chunk_max(s)

    def softmax(u, slot):
        p, alpha = _softmax_update(lambda: s_buf[slot], m_ref, u & (A_KV_HEADS - 1), chunk_max=x_buf[slot])
        p_buf[slot] = p
        a_buf[slot] = alpha

    def values(u, slot):
        g, rows = keys(u)
        pv = jnp.dot(p_buf[slot], v_ref[g, rows, :], preferred_element_type=F32)
        acc_ref[g] = a_buf[slot] * acc_ref[g] + pv

    _run_units(n_sub * A_KV_HEADS, scores, softmax, values)

    @pl.when(step == n_steps - 1)
    def _():
        for h in range(A_HEADS):
            g, r = divmod(h, A_GROUP)
            acc = acc_ref[g, r * tq:(r + 1) * tq, :]
            y = acc[:, 0:HEAD_DIM] / acc[:, HEAD_DIM:HEAD_DIM + 1]
            o_ref[:, h * HEAD_DIM:(h + 1) * HEAD_DIM] = y.astype(BF16)


def _gqa_attention(qa, ka, va):
    _, b, seq, _ = qa.shape
    tq = min(GQA_TQ, seq)
    tk = min(GQA_TK, seq)
    span = seq
    n_steps = seq // span
    rows = A_GROUP * tq
    resident = pl.Buffered(1)
    return pl.pallas_call(
        functools.partial(_gqa_kernel, tq=tq, tk=tk, n_sub=span // tk, n_steps=n_steps),
        grid=(b, seq // tq, n_steps),
        in_specs=[pl.BlockSpec((A_HEADS, None, tq, HEAD_DIM), lambda bi, qi, si: (0, bi, qi, 0)),
                  pl.BlockSpec((A_KV_HEADS, None, span, HEAD_DIM), lambda bi, qi, si: (0, bi, si, 0),
                               pipeline_mode=resident),
                  pl.BlockSpec((A_KV_HEADS, None, span, LANES), lambda bi, qi, si: (0, bi, si, 0),
                               pipeline_mode=resident)],
        out_specs=pl.BlockSpec((None, tq, A_WIDTH), lambda bi, qi, si: (bi, qi, 0)),
        out_shape=jax.ShapeDtypeStruct((b, seq, A_WIDTH), BF16),
        scratch_shapes=[pltpu.VMEM((PIPE_SLOTS, rows, tk), F32), pltpu.VMEM((PIPE_SLOTS, rows, tk), BF16),
                        pltpu.VMEM((PIPE_SLOTS, rows, LANES), F32), pltpu.VMEM((PIPE_SLOTS, rows, LANES), F32),
                        pltpu.VMEM((A_KV_HEADS, rows, LANES), F32), pltpu.VMEM((A_KV_HEADS, rows, LANES), F32)],
        compiler_params=_params("parallel", "parallel", "arbitrary"),
        name="gqa_attn",
    )(qa, ka, va)


def _diff_kernel(slope_ref, q_ref, k_ref, v_ref, lq1_ref, lk1_ref, lq2_ref, lk2_ref, g_ref, o_ref,
                 s_buf, p_buf, a_buf, x_buf, m_ref, acc_ref, *, tq, tk, span, n_steps, lam_init):
    qi = pl.program_id(1)
    step = pl.program_id(2)
    n_sub = span // tk
    q_pos = qi * tq
    diag_step = q_pos // span
    diag_sub = (q_pos // tk) % n_sub
    lane = lax.broadcasted_iota(I32, (1, LANES), 1)

    @pl.when(step == 0)
    def _():
        _init_softmax_state(m_ref, acc_ref)

    @pl.when(step == diag_step)
    def _():
        rows = pl.ds(pl.multiple_of(diag_sub * tk, tk), tk)
        offset = q_pos - (q_pos // tk) * tk
        dist = jnp.abs(lax.broadcasted_iota(I32, (tq, tk), 0) - lax.broadcasted_iota(I32, (tq, tk), 1) + offset)
        dist = dist.astype(F32)
        for h in range(B_HEADS):
            bias = dist * slope_ref[h]
            v = v_ref[h, rows, :]
            for m in range(2):
                i = m * B_HEADS + h
                s = lax.dot_general(q_ref[i][:, 0:HEAD_DIM], k_ref[i, rows, 0:HEAD_DIM], NT_DIMS,
                                    preferred_element_type=F32) - bias
                p, alpha = _softmax_update(lambda: s, m_ref, i)
                acc_ref[i] = jnp.tile(alpha, (1, 2)) * acc_ref[i] + jnp.dot(p, v, preferred_element_type=F32)

    def keys(u):
        j = u >> 2
        j = j + jnp.where((step == diag_step) & (j >= diag_sub), 1, 0)
        return u & (B_HEADS - 1), j, pl.ds(pl.multiple_of(j * tk, tk), tk)

    def scores(u, slot):
        h, j, rows = keys(u)
        keys_before = q_pos >= step * span + (j + 1) * tk
        sign = jnp.where(keys_before, -1.0, 1.0)
        lane_scale = jnp.where(lane < HEAD_DIM, 1.0, sign).astype(BF16)
        for m in range(2):
            i = m * B_HEADS + h
            s = lax.dot_general(q_ref[i] * lane_scale, k_ref[i, rows, :], NT_DIMS, preferred_element_type=F32)
            s_buf[slot, m] = s
            x_buf[slot, m] = _chunk_max(s)

    def softmax(u, slot):
        h = u & (B_HEADS - 1)
        for m in range(2):
            p, alpha = _softmax_update(lambda: s_buf[slot, m], m_ref, m * B_HEADS + h, chunk_max=x_buf[slot, m])
            p_buf[slot, m] = p
            a_buf[slot, m] = alpha

    def values(u, slot):
        h, _, rows = keys(u)
        v = v_ref[h, rows, :]
        for m in range(2):
            i = m * B_HEADS + h
            pv = jnp.dot(p_buf[slot, m], v, preferred_element_type=F32)
            acc_ref[i] = jnp.tile(a_buf[slot, m], (1, 2)) * acc_ref[i] + pv

    _run_units(jnp.where(step == diag_step, n_sub - 1, n_sub) * B_HEADS, scores, softmax, values)

    @pl.when(step == n_steps - 1)
    def _():
        lam = (jnp.exp(jnp.sum(lq1_ref[...] * lk1_ref[...], axis=1, keepdims=True))
               - jnp.exp(jnp.sum(lq2_ref[...] * lk2_ref[...], axis=1, keepdims=True)) + lam_init)

        def normalized(i):
            acc = acc_ref[i]
            return acc[:, 0:B_VDIM] / acc[:, B_VDIM:B_VDIM + 1]

        for h in range(B_HEADS):
            y = normalized(h) - lam * normalized(B_HEADS + h)
            ms = jnp.mean(y * y, axis=1, keepdims=True)
            y = y * lax.rsqrt(ms + RMS_EPS) * g_ref[...] * (1.0 - lam_init)
            o_ref[:, h * B_VDIM:(h + 1) * B_VDIM] = y.astype(BF16)


def _diff_attention(qb, kb, vb, lq1, lk1, lq2, lk2, subln_g, lam_init):
    _, b, seq, _ = qb.shape
    tq = min(DIFF_TQ, seq)
    tk = min(DIFF_TK, seq)
    span = min(DIFF_SPAN, seq)
    n_steps = seq // span
    assert tk % tq == 0 and span // tk >= 2
    nh = 2 * B_HEADS
    slopes = jnp.asarray([2.0 ** (-8.0 * (h + 1) / B_HEADS) * LOG2E for h in range(B_HEADS)], F32)
    vec = pl.BlockSpec((1, HEAD_DIM), lambda bi, qi, si, *_: (0, 0))
    return pl.pallas_call(
        functools.partial(_diff_kernel, tq=tq, tk=tk, span=span, n_steps=n_steps, lam_init=lam_init),
        grid_spec=pltpu.PrefetchScalarGridSpec(
            num_scalar_prefetch=1,
            grid=(b, seq // tq, n_steps),
            in_specs=[pl.BlockSpec((nh, None, tq, LANES), lambda bi, qi, si, *_: (0, bi, qi, 0)),
                      pl.BlockSpec((nh, None, span, LANES), lambda bi, qi, si, *_: (0, bi, si, 0)),
                      pl.BlockSpec((B_HEADS, None, span, 2 * B_VDIM), lambda bi, qi, si, *_: (0, bi, si, 0)),
                      vec, vec, vec, vec, pl.BlockSpec((1, B_VDIM), lambda bi, qi, si, *_: (0, 0))],
            out_specs=pl.BlockSpec((None, tq, B_WIDTH), lambda bi, qi, si, *_: (bi, qi, 0)),
            scratch_shapes=[pltpu.VMEM((PIPE_SLOTS, 2, tq, tk), F32), pltpu.VMEM((PIPE_SLOTS, 2, tq, tk), BF16),
                            pltpu.VMEM((PIPE_SLOTS, 2, tq, LANES), F32), pltpu.VMEM((PIPE_SLOTS, 2, tq, LANES), F32),
                            pltpu.VMEM((nh, tq, LANES), F32), pltpu.VMEM((nh, tq, 2 * B_VDIM), F32)],
        ),
        out_shape=jax.ShapeDtypeStruct((b, seq, B_WIDTH), BF16),
        compiler_params=_params("parallel", "parallel", "arbitrary"),
        name="diff_attn",
    )(slopes, qb, kb, vb, lq1, lk1, lq2, lk2, subln_g)


def _merge_kernel(ya_ref, yb_ref, gate_ref, h_ref, p_ref, wa_ref, wb_ref, wo_ref, g_ref, b_ref,
                  wrh_ref, wrl_ref, wpg_ref, wpp_ref, x1_ref, base_ref, aff_ref):
    ma = jnp.dot(ya_ref[...], wa_ref[...], preferred_element_type=F32)
    mb = jnp.dot(yb_ref[...], wb_ref[...], preferred_element_type=F32)
    merged = _sigmoid(gate_ref[:, 0:D_MODEL]) * ma + _sigmoid(gate_ref[:, D_MODEL:2 * D_MODEL]) * mb
    out = jnp.dot(merged.astype(BF16), wo_ref[...], preferred_element_type=F32)
    x1 = _layer_norm(ALPHA * h_ref[...] + out, g_ref[...], b_ref[...])
    x1_ref[...] = x1
    xh, xl = _split_bf16(x1)
    ple_gate = _sigmoid(jnp.dot(xh, wpg_ref[...], preferred_element_type=F32))
    base_ref[...] = ALPHA * x1 + ple_gate * jnp.dot(p_ref[...].astype(BF16), wpp_ref[...], preferred_element_type=F32)
    wh = wrh_ref[...]
    logits = (lax.dot_general(wh, xh, NT_DIMS, preferred_element_type=F32)
              + lax.dot_general(wh, xl, NT_DIMS, preferred_element_type=F32)
              + lax.dot_general(wrl_ref[...], xh, NT_DIMS, preferred_element_type=F32))
    e = jnp.exp(logits - jnp.max(logits, axis=0, keepdims=True))
    aff_ref[...] = e / jnp.sum(e, axis=0, keepdims=True)


def _merge(ya, yb, gates, h, p2, wa, wb, wo, g, b, wrh, wrl, wpg, wpp):
    n = h.shape[0]
    tm = ROW_TILE

    def rows(w):
        return pl.BlockSpec((tm, w), lambda i: (i, 0))

    return pl.pallas_call(
        _merge_kernel,
        grid=(n // tm,),
        in_specs=[rows(A_WIDTH), rows(B_WIDTH), rows(2 * D_MODEL), rows(D_MODEL), rows(PLE_DIM),
                  _full((A_WIDTH, D_MODEL)), _full((B_WIDTH, D_MODEL)), _full((D_MODEL, D_MODEL)),
                  _full((1, D_MODEL)), _full((1, D_MODEL)),
                  _full((N_EXPERTS, D_MODEL)), _full((N_EXPERTS, D_MODEL)),
                  _full((D_MODEL, D_MODEL)), _full((PLE_DIM, D_MODEL))],
        out_specs=[rows(D_MODEL), rows(D_MODEL), pl.BlockSpec((N_EXPERTS, tm), lambda i: (0, i))],
        out_shape=[jax.ShapeDtypeStruct((n, D_MODEL), F32), jax.ShapeDtypeStruct((n, D_MODEL), F32),
                   jax.ShapeDtypeStruct((N_EXPERTS, n), F32)],
        compiler_params=_params("parallel"),
        name="merge",
    )(ya, yb, gates, h, p2, wa, wb, wo, g, b, wrh, wrl, wpg, wpp)


def _route_kernel(aff_ref, rank_ref, idx_ref, blkp_ref, sel_scr, wi_scr, p_scr, *, nb, cap):
    ne = N_EXPERTS
    shape3 = (ne, nb, LANES)
    bits = pltpu.bitcast(aff_ref[...], I32).reshape(shape3)
    capf = float(cap)

    def count(mask3):
        part = jnp.sum(jnp.where(mask3, 1.0, 0.0), axis=1, keepdims=True)
        return jnp.sum(part, axis=2, keepdims=True)

    def search(i, thr):
        cand = thr | jnp.left_shift(jnp.int32(1), 30 - i)
        return jnp.where(count(bits >= cand) >= capf, cand, thr)

    thr = lax.fori_loop(0, 31, search, jnp.zeros((ne, 1, 1), I32))
    gt = bits > thr
    eq = bits == thr
    need = capf - count(gt)

    li = lax.broadcasted_iota(I32, (LANES, LANES), 0)
    lj = lax.broadcasted_iota(I32, (LANES, LANES), 1)
    upper_incl = jnp.where(li <= lj, 1.0, 0.0).astype(BF16)
    lower_incl = jnp.where(lj <= li, 1.0, 0.0).astype(BF16)
    bi = lax.broadcasted_iota(I32, (nb, nb), 0)
    bj = lax.broadcasted_iota(I32, (nb, nb), 1)
    lower_strict = jnp.where(bj < bi, 1.0, 0.0).astype(BF16)
    upper_strict = jnp.where(bi < bj, 1.0, 0.0).astype(BF16)

    def prefix(mask2):
        wi = jnp.dot(mask2.astype(BF16), upper_incl, preferred_element_type=F32)
        tot = jnp.broadcast_to(wi[:, LANES - 1:LANES], wi.shape).astype(BF16)
        blk = [jnp.dot(lower_strict, tot[e * nb:(e + 1) * nb], preferred_element_type=F32)
               for e in range(ne)]
        return wi, jnp.concatenate(blk, axis=0)

    eqf = jnp.where(eq, 1.0, 0.0).reshape(ne * nb, LANES)
    wi, bp = prefix(eqf)
    eq_rank = (wi + bp - eqf).reshape(shape3)
    sel = gt | (eq & (eq_rank < need))
    self_ = jnp.where(sel, 1.0, 0.0).reshape(ne * nb, LANES)
    wi, bp = prefix(self_)
    rank = wi + bp - self_
    rank_ref[...] = jnp.where(self_ > 0.0, rank, -1.0).astype(I32)
    sel_scr[...] = self_
    wi_scr[...] = wi
    p_scr[...] = bp

    chunk = min(cap, 1024)
    ones8 = jnp.ones((8, LANES), BF16)
    b_col = lax.broadcasted_iota(I32, (nb, 1), 0).astype(F32)

    def per_expert(e, carry):
        r0 = pl.multiple_of(e * nb, 8)
        sel_e = sel_scr[pl.ds(r0, nb), :].astype(BF16)
        wi_e = wi_scr[pl.ds(r0, nb), :]
        p_e = p_scr[pl.ds(r0, nb), :][:, 0:1]
        p_end = p_e + wi_e[:, LANES - 1:LANES]
        wi_t = lax.dot_general(lower_incl, sel_e, NT_DIMS, preferred_element_type=F32).astype(BF16)
        tot_row = lax.dot_general(ones8, sel_e, NT_DIMS, preferred_element_type=F32)
        p_row = jnp.dot(tot_row.astype(BF16), upper_strict, preferred_element_type=F32)
        blkp_ref[e] = p_row[0:1, :].astype(I32)
        for c in range(cap // chunk):
            slot = (lax.broadcasted_iota(I32, (1, chunk), 1) + c * chunk).astype(F32)
            hot = (p_e <= slot) & (slot < p_end)
            w_slot = jnp.dot(wi_t, jnp.where(hot, 1.0, 0.0).astype(BF16), preferred_element_type=F32)
            p_slot = jnp.sum(jnp.where(hot, p_e, 0.0), axis=0, keepdims=True)
            b_slot = jnp.sum(jnp.where(hot, b_col, 0.0), axis=0, keepdims=True)
            lane = jnp.sum(jnp.where(w_slot <= slot - p_slot, 1.0, 0.0), axis=0, keepdims=True)
            idx_ref[e, :, c * chunk:(c + 1) * chunk] = (b_slot * LANES + lane).astype(I32)
        return carry

    lax.fori_loop(0, ne, per_expert, 0)


def _route(aff2, n, cap):
    nb = n // LANES
    return pl.pallas_call(
        functools.partial(_route_kernel, nb=nb, cap=cap),
        out_shape=[jax.ShapeDtypeStruct((N_EXPERTS * nb, LANES), I32),
                   jax.ShapeDtypeStruct((N_EXPERTS, 1, cap), I32),
                   jax.ShapeDtypeStruct((N_EXPERTS, 1, nb), I32)],
        scratch_shapes=[pltpu.VMEM((N_EXPERTS * nb, LANES), F32)] * 3,
        compiler_params=pltpu.CompilerParams(vmem_limit_bytes=VMEM_LIMIT),
        name="route",
    )(aff2)


FFN_SLOTS = 3


def _ffn_kernel(idx_ref, x_hbm, wg_ref, wu_ref, wd_ref, y_ref, xbuf, sem, *, tm, tiles, n_steps):
    step = pl.program_id(0) * tiles + pl.program_id(1)
    slot = lax.rem(step, FFN_SLOTS)

    def fetch(src_step, dst_slot):
        for r in range(tm):
            pltpu.make_async_copy(x_hbm.at[pl.ds(idx_ref[src_step * tm + r], 1), :],
                                  xbuf.at[dst_slot, pl.ds(r, 1), :], sem.at[dst_slot]).start(priority=r % 2)

    def wait(which):
        pltpu.make_async_copy(x_hbm.at[pl.ds(0, tm), :], xbuf.at[which], sem.at[which]).wait()

    @pl.when(step == 0)
    def _():
        fetch(0, 0)
        fetch(jnp.minimum(1, n_steps - 1), 1)

    wait(slot)
    x = xbuf[slot].astype(BF16)
    fc = 512
    acc = None
    for c in range(EXPERT_FF // fc):
        sl = slice(c * fc, (c + 1) * fc)
        a = jnp.dot(x, wg_ref[:, sl], preferred_element_type=F32)
        u = jnp.dot(x, wu_ref[:, sl], preferred_element_type=F32)
        mid = (a * _sigmoid(a) * u).astype(BF16)
        part = jnp.dot(mid, wd_ref[sl, :], preferred_element_type=F32)
        acc = part if acc is None else acc + part
    y_ref[...] = acc.astype(BF16)
    fetch(jnp.minimum(step + 2, n_steps - 1), lax.rem(step + 2, FFN_SLOTS))

    @pl.when(step == n_steps - 1)
    def _():
        wait(lax.rem(step + 1, FFN_SLOTS))
        wait(lax.rem(step + 2, FFN_SLOTS))


def _ffn(idx_flat, x1, wg, wu, wd, cap):
    tm = min(ROW_TILE, cap)
    tiles = cap // tm
    return pl.pallas_call(
        functools.partial(_ffn_kernel, tm=tm, tiles=tiles, n_steps=N_EXPERTS * tiles),
        grid_spec=pltpu.PrefetchScalarGridSpec(
            num_scalar_prefetch=1,
            grid=(N_EXPERTS, tiles),
            in_specs=[pl.BlockSpec(memory_space=pl.ANY),
                      pl.BlockSpec((None, D_MODEL, EXPERT_FF), lambda e, i, idx: (e, 0, 0)),
                      pl.BlockSpec((None, D_MODEL, EXPERT_FF), lambda e, i, idx: (e, 0, 0)),
                      pl.BlockSpec((None, EXPERT_FF, D_MODEL), lambda e, i, idx: (e, 0, 0))],
            out_specs=pl.BlockSpec((tm, D_MODEL), lambda e, i, idx: (e * tiles + i, 0)),
            scratch_shapes=[pltpu.VMEM((FFN_SLOTS, tm, D_MODEL), F32), pltpu.SemaphoreType.DMA((FFN_SLOTS,))],
        ),
        out_shape=jax.ShapeDtypeStruct((N_EXPERTS * cap, D_MODEL), BF16),
        compiler_params=_params("arbitrary", "arbitrary"),
        name="ffn",
    )(idx_flat, x1, wg, wu, wd)


TOK_TILE = 128
ROW_ALIGN = 16
WIN = TOK_TILE + ROW_ALIGN
SPARSE_CNT = 32
SPARSE_WIN = SPARSE_CNT + ROW_ALIGN
GROUP = 4
assert GROUP * SPARSE_WIN <= MXU_DEPTH and N_EXPERTS % GROUP == 0


def _final_kernel(start_ref, sparse_ref, base_ref, rank_ref, aff_ref, g_ref, b_ref, ye_hbm, o_ref, buf, sem,
                  *, cap, n_steps):
    t = pl.program_id(0)
    slot = t & 1

    def window(step, e, rows):
        first = start_ref[step * N_EXPERTS + e]
        ws = jnp.minimum((first // ROW_ALIGN) * ROW_ALIGN, cap - rows)
        return pl.multiple_of(ws, ROW_ALIGN)

    def fetch(step, dst_slot):
        def start_copies(rows):
            for e in range(N_EXPERTS):
                pltpu.make_async_copy(ye_hbm.at[pl.ds(e * cap + window(step, e, rows), rows), :],
                                      buf.at[dst_slot, pl.ds(e * rows, rows), :], sem.at[dst_slot]).start()

        @pl.when(sparse_ref[step] == 1)
        def _():
            start_copies(SPARSE_WIN)

        @pl.when(sparse_ref[step] == 0)
        def _():
            start_copies(WIN)

    def wait(rows):
        pltpu.make_async_copy(ye_hbm.at[pl.ds(0, N_EXPERTS * rows), :],
                              buf.at[slot, pl.ds(0, N_EXPERTS * rows), :], sem.at[slot]).wait()

    @pl.when(t == 0)
    def _():
        fetch(0, 0)

    @pl.when(t + 1 < n_steps)
    def _():
        fetch(t + 1, 1 - slot)

    @pl.when(sparse_ref[t] == 1)
    def _():
        wait(SPARSE_WIN)
        gate = aff_ref[...]
        gate_hi = gate.astype(BF16).astype(F32)
        gate_lo = gate - gate_hi
        lane = lax.broadcasted_iota(I32, (1, GROUP * SPARSE_WIN), 1)
        total = base_ref[...]
        for grp in range(N_EXPERTS // GROUP):
            hi = jnp.zeros((TOK_TILE, GROUP * SPARSE_WIN), F32)
            lo = hi
            for k in range(GROUP):
                e = grp * GROUP + k
                rank = rank_ref[:, e:e + 1]
                target = jnp.where(rank >= 0, rank - window(t, e, SPARSE_WIN) + k * SPARSE_WIN, -1)
                hot = lane == target
                hi = jnp.where(hot, gate_hi[:, e:e + 1], hi)
                lo = jnp.where(hot, gate_lo[:, e:e + 1], lo)
            rows = buf[slot, grp * GROUP * SPARSE_WIN:(grp + 1) * GROUP * SPARSE_WIN, :]
            y = jnp.dot(jnp.concatenate([hi, lo], axis=0).astype(BF16), rows, preferred_element_type=F32)
            total = total + y[0:TOK_TILE] + y[TOK_TILE:2 * TOK_TILE]
        o_ref[...] = _layer_norm(total, g_ref[...], b_ref[...])

    @pl.when(sparse_ref[t] == 0)
    def _():
        wait(WIN)
        lane = lax.broadcasted_iota(I32, (1, WIN), 1)
        total = base_ref[...]
        for e in range(N_EXPERTS):
            hot = rank_ref[:, e:e + 1] == lane + window(t, e, WIN)
            y = jnp.dot(jnp.where(hot, 1.0, 0.0).astype(BF16), buf[slot, e * WIN:(e + 1) * WIN, :],
                        preferred_element_type=F32)
            total = total + aff_ref[:, e:e + 1] * y
        o_ref[...] = _layer_norm(total, g_ref[...], b_ref[...])


def _final(starts, sparse, base, rank_t, aff_t, g, b, ye, cap):
    n = base.shape[0]
    tm = TOK_TILE

    def rows(w):
        return pl.BlockSpec((tm, w), lambda i, *_: (i, 0))

    def full(shape):
        return pl.BlockSpec(shape, lambda i, *_: (0,) * len(shape))

    return pl.pallas_call(
        functools.partial(_final_kernel, cap=cap, n_steps=n // tm),
        grid_spec=pltpu.PrefetchScalarGridSpec(
            num_scalar_prefetch=2,
            grid=(n // tm,),
            in_specs=[rows(D_MODEL), rows(N_EXPERTS), rows(N_EXPERTS), full((1, D_MODEL)), full((1, D_MODEL)),
                      pl.BlockSpec(memory_space=pl.ANY)],
            out_specs=rows(D_MODEL),
            scratch_shapes=[pltpu.VMEM((2, N_EXPERTS * WIN, D_MODEL), BF16), pltpu.SemaphoreType.DMA((2,))],
        ),
        out_shape=jax.ShapeDtypeStruct((n, D_MODEL), F32),
        compiler_params=_params("arbitrary"),
        name="final",
    )(starts, sparse, base, rank_t, aff_t, g, b, ye)


def _rope_tables(seq):
    n_rows = seq // GRID_W
    rows = jnp.repeat(jnp.arange(n_rows, dtype=F32), GRID_W)
    cols = jnp.tile(jnp.arange(GRID_W, dtype=F32), n_rows)
    sec = HEAD_DIM // 2
    inv_freq = ROPE_THETA ** (-jnp.arange(0, sec, 2, dtype=F32) / sec)
    ang_r = rows[:, None] * inv_freq[None, :]
    ang_c = cols[:, None] * inv_freq[None, :]
    cos = jnp.concatenate([jnp.cos(ang_r)] * 2 + [jnp.cos(ang_c)] * 2, axis=-1)
    sin = jnp.concatenate([-jnp.sin(ang_r), jnp.sin(ang_r), -jnp.sin(ang_c), jnp.sin(ang_c)], axis=-1)
    return jnp.tile(cos, (1, 2)), jnp.tile(sin, (1, 2))


def _trunk(x, p, w):
    b, seq, _ = x.shape
    n = b * seq
    cap = max(1, CAPACITY_FACTOR * n // N_EXPERTS)
    assert seq % ROW_TILE == 0 and n % (8 * LANES) == 0 and cap % TOK_TILE == 0 and cap >= WIN
    cos_t, sin_t = _rope_tables(seq)

    h, qa, ka, va, qb, kb, vb, gates = _in_proj(
        x.reshape(n, D_MODEL), w["ln_emb_g"], w["ln_emb_b"], w["w_in"], w["q_norm_g"], w["k_norm_g"],
        cos_t, sin_t, w["seg"], seq)

    def per_batch(t):
        return t.reshape(t.shape[0], b, seq, t.shape[2])

    ya = _gqa_attention(per_batch(qa), per_batch(ka), per_batch(va))
    lam_init = 0.8 - 0.6 * math.exp(-0.3 * 0)
    yb = _diff_attention(per_batch(qb), per_batch(kb), per_batch(vb),
                         w["lambda_q1"], w["lambda_k1"], w["lambda_q2"], w["lambda_k2"], w["subln_g"], lam_init)

    x1, base, aff = _merge(ya.reshape(n, -1), yb.reshape(n, -1), gates, h, p.reshape(n, PLE_DIM),
                           w["w_branch_a"], w["w_branch_b"], w["w_out"], w["ln1_g"], w["ln1_b"],
                           w["wr_hi"], w["wr_lo"], w["w_ple_gate"], w["w_ple_proj"])

    nb = n // LANES
    rank2, idx, blkp = _route(aff.reshape(N_EXPERTS * nb, LANES), n, cap)
    ye = _ffn(idx.reshape(N_EXPERTS * cap), x1, w["w_gate"], w["w_up"], w["w_down"], cap)

    starts = blkp.reshape(N_EXPERTS, nb)[:, ::TOK_TILE // LANES]
    ends = jnp.concatenate([starts[:, 1:], jnp.full((N_EXPERTS, 1), cap, I32)], axis=1)
    sparse = (jnp.max(ends - starts, axis=0) <= SPARSE_CNT).astype(I32)
    out = _final(starts.T.reshape(-1), sparse, base, rank2.reshape(N_EXPERTS, n).T, aff.T,
                 w["ln2_g"], w["ln2_b"], ye, cap)
    return out.reshape(b, seq, D_MODEL)


def _prep_weights(ln_emb_g, ln_emb_b, w_in, q_norm_g, k_norm_g, lambda_q1, lambda_k1, lambda_q2, lambda_k2,
                  subln_g, w_branch_a, w_branch_b, w_out, ln1_g, ln1_b, w_router, w_gate, w_up, w_down,
                  w_ple_gate, w_ple_proj, ln2_g, ln2_b):
    i = 0
    seg_i = lax.broadcasted_iota(I32, (A_WIDTH, A_WIDTH), 0) // HEAD_DIM
    seg_j = lax.broadcasted_iota(I32, (A_WIDTH, A_WIDTH), 1) // HEAD_DIM
    wr_t = w_router[i].T
    wr_hi = wr_t.astype(BF16)
    return {
        "ln_emb_g": ln_emb_g.reshape(1, D_MODEL), "ln_emb_b": ln_emb_b.reshape(1, D_MODEL),
        "w_in": w_in[i].astype(BF16),
        "q_norm_g": jnp.tile(q_norm_g[i], 2).reshape(1, LANES), "k_norm_g": jnp.tile(k_norm_g[i], 2).reshape(1, LANES),
        "seg": (seg_i == seg_j).astype(BF16),
        "lambda_q1": lambda_q1[i].reshape(1, HEAD_DIM), "lambda_k1": lambda_k1[i].reshape(1, HEAD_DIM),
        "lambda_q2": lambda_q2[i].reshape(1, HEAD_DIM), "lambda_k2": lambda_k2[i].reshape(1, HEAD_DIM),
        "subln_g": subln_g[i].reshape(1, B_VDIM),
        "w_branch_a": w_branch_a[i].astype(BF16), "w_branch_b": w_branch_b[i].astype(BF16),
        "w_out": w_out[i].astype(BF16),
        "ln1_g": ln1_g[i].reshape(1, D_MODEL), "ln1_b": ln1_b[i].reshape(1, D_MODEL),
        "wr_hi": wr_hi, "wr_lo": (wr_t - wr_hi.astype(F32)).astype(BF16),
        "w_gate": w_gate[i].astype(BF16), "w_up": w_up[i].astype(BF16), "w_down": w_down[i].astype(BF16),
        "w_ple_gate": w_ple_gate[i].astype(BF16), "w_ple_proj": w_ple_proj[i].astype(BF16),
        "ln2_g": ln2_g[i].reshape(1, D_MODEL), "ln2_b": ln2_b[i].reshape(1, D_MODEL),
    }


def kernel(x_prompt, x_sample, p_prompt, p_sample, ln_emb_g, ln_emb_b, w_in, q_norm_g, k_norm_g, lambda_q1, lambda_k1, lambda_q2, lambda_k2, subln_g, w_branch_a, w_branch_b, w_out, ln1_g, ln1_b, w_router, w_gate, w_up, w_down, w_ple_gate, w_ple_proj, ln2_g, ln2_b):
    w = _prep_weights(ln_emb_g, ln_emb_b, w_in, q_norm_g, k_norm_g, lambda_q1, lambda_k1, lambda_q2, lambda_k2,
                      subln_g, w_branch_a, w_branch_b, w_out, ln1_g, ln1_b, w_router, w_gate, w_up, w_down,
                      w_ple_gate, w_ple_proj, ln2_g, ln2_b)
    return (_trunk(x_prompt, p_prompt[0], w), _trunk(x_sample, p_sample[0], w))
```
